```python
import math
import jax, jax.numpy as jnp
from jax import lax
import numpy as np

D_MODEL = 1024
BATCH = 4
SEQ = 4096
DEPTH = 2

HEAD_DIM = 64
SSM_HEADS = 16
SSM_HEAD_DIM = HEAD_DIM
SSM_D_INNER = SSM_HEADS * SSM_HEAD_DIM
SSM_GROUPS = 2
SSM_HEADS_PER_GROUP = SSM_HEADS // SSM_GROUPS
SSM_STATE = 128
SSM_CONV = 4
SSM_CHUNK = 128
SSM_CONV_DIM = SSM_D_INNER + 2 * SSM_GROUPS * SSM_STATE
SB_HEADS = 8
SB_WIDTH = SB_HEADS * HEAD_DIM
DIFF_HEADS = 4
DIFF_QK_DIM = HEAD_DIM
DIFF_V_DIM = 2 * HEAD_DIM
DIFF_QK_WIDTH = DIFF_HEADS * 2 * DIFF_QK_DIM
DIFF_WIDTH = DIFF_HEADS * DIFF_V_DIM
MIX_WIDTH = SSM_D_INNER + SB_WIDTH + DIFF_WIDTH
IN_SPLITS = (SSM_D_INNER, SSM_CONV_DIM, SSM_HEADS,
             SB_WIDTH, SB_WIDTH, SB_WIDTH,
             DIFF_QK_WIDTH, DIFF_QK_WIDTH, DIFF_WIDTH)
IN_DIM = sum(IN_SPLITS)
D_FF = 2816
FFN_HALF = 0.5
N_REL_BUCKETS = 32
REL_MAX_DIST = 128
Q_BLOCK = 128
N_MOD = 9
EPS = 1e-6

kernel_name = "hybrid_ssd_stickbreak_diffattn_macaron_block"


def _rmsnorm(x, g):
    xf = x.astype(jnp.float32)
    y = xf * lax.rsqrt(jnp.mean(xf * xf, axis=-1, keepdims=True) + EPS)
    return (y * g.astype(jnp.float32)).astype(x.dtype)


def _modulate(h, shift, scale):
    return h * (1 + scale[:, None, :]) + shift[:, None, :]


def _swiglu(h, w13, w2):
    a, u = jnp.split(h @ w13, 2, axis=-1)
    return (jax.nn.silu(a) * u) @ w2


def _causal_depthwise_conv(u, w, b):
    k = w.shape[-1]
    rhs = jnp.transpose(w)[:, None, :].astype(u.dtype)
    out = lax.conv_general_dilated(u, rhs, window_strides=(1,), padding=[(k - 1, 0)],
                                   dimension_numbers=("NWC", "WIO", "NWC"),
                                   feature_group_count=u.shape[-1])
    return out + b.astype(u.dtype)


def _t5_bucket(dist):
    max_exact = N_REL_BUCKETS // 2
    d = jnp.maximum(dist.astype(jnp.float32), float(max_exact))
    large = max_exact + (jnp.log(d / max_exact) / math.log(REL_MAX_DIST / max_exact)
                         * (N_REL_BUCKETS - max_exact)).astype(jnp.int32)
    large = jnp.minimum(large, N_REL_BUCKETS - 1)
    return jnp.where(dist < max_exact, dist, large)


def _sweep_query_blocks(fn, seq):
    starts = jnp.arange(seq // Q_BLOCK, dtype=jnp.int32) * Q_BLOCK
    out = lax.map(fn, starts)
    nb, b, qb = out.shape[:3]
    return jnp.moveaxis(out, 0, 1).reshape((b, nb * qb) + out.shape[3:])


def _ssd_mixer(z, xbc, dt_raw, conv_w, conv_b, dt_bias, a_log, d_skip, norm_g):
    f32 = jnp.float32
    b, s, _ = z.shape
    nc = s // SSM_CHUNK
    xbc = jax.nn.silu(_causal_depthwise_conv(xbc, conv_w, conv_b))
    xs, bm, cm = jnp.split(xbc, [SSM_D_INNER, SSM_D_INNER + SSM_GROUPS * SSM_STATE], axis=-1)
    dt = jax.nn.softplus(dt_raw.astype(f32) + dt_bias.astype(f32))
    a = -jnp.exp(a_log.astype(f32))
    xh = xs.astype(f32).reshape(b, s, SSM_HEADS, SSM_HEAD_DIM)
    xd = (xh * dt[..., None]).reshape(b, nc, SSM_CHUNK, SSM_GROUPS, SSM_HEADS_PER_GROUP, SSM_HEAD_DIM)
    bc = bm.astype(f32).reshape(b, nc, SSM_CHUNK, SSM_GROUPS, SSM_STATE)
    cc = cm.astype(f32).reshape(b, nc, SSM_CHUNK, SSM_GROUPS, SSM_STATE)
    da = (dt * a).reshape(b, nc, SSM_CHUNK, SSM_GROUPS, SSM_HEADS_PER_GROUP)
    da = jnp.transpose(da, (0, 3, 4, 1, 2))
    a_cum = jnp.cumsum(da, axis=-1)
    causal = jnp.tril(jnp.ones((SSM_CHUNK, SSM_CHUNK), dtype=bool))
    seg = a_cum[..., :, None] - a_cum[..., None, :]
    decay = jnp.exp(jnp.where(causal, seg, -jnp.inf))
    cb = jnp.einsum("bclgn,bcsgn->bgcls", cc, bc)
    y_diag = jnp.einsum("bgecls,bcsgep->bclgep", cb[:, :, None] * decay, xd)
    decay_to_end = jnp.exp(a_cum[..., -1:] - a_cum)
    chunk_states = jnp.einsum("bclgn,bgecl,bclgep->bcgepn", bc, decay_to_end, xd)
    chunk_decay = jnp.exp(a_cum[..., -1])

    def carry_state(h, inp):
        st, dec = inp
        return h * dec[..., None, None] + st, h

    h0 = jnp.zeros_like(chunk_states[:, 0])
    _, prev = lax.scan(carry_state, h0,
                       (jnp.moveaxis(chunk_states, 1, 0), jnp.moveaxis(chunk_decay, -1, 0)))
    prev = jnp.moveaxis(prev, 0, 1)
    y_off = jnp.einsum("bclgn,bcgepn,bgecl->bclgep", cc, prev, jnp.exp(a_cum))
    y = (y_diag + y_off).reshape(b, s, SSM_HEADS, SSM_HEAD_DIM) + d_skip.astype(f32)[:, None] * xh
    y = y.reshape(b, s, SSM_D_INNER) * jax.nn.silu(z.astype(f32))
    yg = y.reshape(b, s, SSM_GROUPS, SSM_D_INNER // SSM_GROUPS)
    yg = yg * lax.rsqrt(jnp.mean(yg * yg, axis=-1, keepdims=True) + EPS)
    return (yg.reshape(b, s, SSM_D_INNER) * norm_g.astype(f32)).astype(z.dtype)


def _stick_breaking_attention(q, k, v):
    b, s, h, d = q.shape
    scale = d ** -0.5
    kpos = jnp.arange(s, dtype=jnp.int32)

    def block(start):
        qb = lax.dynamic_slice_in_dim(q, start, Q_BLOCK, axis=1)
        qpos = start + jnp.arange(Q_BLOCK, dtype=jnp.int32)
        strict = kpos[None, :] < qpos[:, None]
        logits = jnp.einsum("bqhd,bkhd->bhqk", qb, k).astype(jnp.float32) * scale
        log_1m_beta = jnp.where(strict, jax.nn.log_sigmoid(-logits), 0.0)
        log_stick = lax.cumsum(log_1m_beta, axis=3, reverse=True) - log_1m_beta
        weights = jnp.where(strict, jnp.exp(jax.nn.log_sigmoid(logits) + log_stick), 0.0)
        return jnp.einsum("bhqk,bkhd->bqhd", weights.astype(v.dtype), v)

    return _sweep_query_blocks(block, s).reshape(b, s, h * d)


def _diff_attention(q, k, v, rel_bias, lam, lambda_init, subln_g):
    b, s, h, _, dk = q.shape
    scale = dk ** -0.5
    kpos = jnp.arange(s, dtype=jnp.int32)

    def block(start):
        qb = lax.dynamic_slice_in_dim(q, start, Q_BLOCK, axis=1)
        qpos = start + jnp.arange(Q_BLOCK, dtype=jnp.int32)
        causal = kpos[None, :] <= qpos[:, None]
        bucket = _t5_bucket(jnp.maximum(qpos[:, None] - kpos[None, :], 0))
        bias = jnp.transpose(rel_bias[bucket], (2, 0, 1)).astype(jnp.float32)
        logits = jnp.einsum("bqhmd,bkhmd->bmhqk", qb, k).astype(jnp.float32) * scale + bias
        p = jax.nn.softmax(jnp.where(causal, logits, -jnp.inf), axis=-1)
        attn = p[:, 0] - lam * p[:, 1]
        return jnp.einsum("bhqk,bkhe->bqhe", attn.astype(v.dtype), v)

    o = _sweep_query_blocks(block, s)
    o = _rmsnorm(o, subln_g) * (1.0 - lambda_init)
    return o.reshape(b, s, h * DIFF_V_DIM)


def _hybrid_mixer(h, w_in, conv_w, conv_b, dt_bias, a_log, d_skip, ssm_norm,
                  lq1, lk1, lq2, lk2, subln_g, rel_bias, w_out, lambda_init):
    b, s, _ = h.shape
    offsets = [int(o) for o in np.cumsum(IN_SPLITS)[:-1]]
    z, xbc, dt_raw, sq, sk, sv, dq, dkk, dv = jnp.split(h @ w_in, offsets, axis=-1)
    y_ssm = _ssd_mixer(z, xbc, dt_raw, conv_w, conv_b, dt_bias, a_log, d_skip, ssm_norm)
    y_sb = _stick_breaking_attention(sq.reshape(b, s, SB_HEADS, HEAD_DIM),
                                     sk.reshape(b, s, SB_HEADS, HEAD_DIM),
                                     sv.reshape(b, s, SB_HEADS, HEAD_DIM))
    lam = (jnp.exp(jnp.sum(lq1.astype(jnp.float32) * lk1.astype(jnp.float32)))
           - jnp.exp(jnp.sum(lq2.astype(jnp.float32) * lk2.astype(jnp.float32))) + lambda_init)
    y_diff = _diff_attention(dq.reshape(b, s, DIFF_HEADS, 2, DIFF_QK_DIM),
                             dkk.reshape(b, s, DIFF_HEADS, 2, DIFF_QK_DIM),
                             dv.reshape(b, s, DIFF_HEADS, DIFF_V_DIM),
                             rel_bias, lam, lambda_init, subln_g)
    y = jnp.concatenate([y_ssm.astype(h.dtype), y_sb.astype(h.dtype), y_diff.astype(h.dtype)], axis=-1)
    return y @ w_out


def setup_inputs(seed: int = 0) -> dict:
    key = jax.random.key(seed)
    ks = jax.random.split(key, 32)
    f32 = jnp.float32

    def nrm(k, shape, scale):
        return jax.random.normal(k, shape, f32) * scale

    def gain(k, shape):
        return 1.0 + 0.01 * jax.random.normal(k, shape, f32)

    gate_rows = jnp.zeros((N_MOD, D_MODEL), f32).at[2::3].set(1.0).reshape(-1)
    dt = jnp.exp(jax.random.uniform(ks[9], (DEPTH, SSM_HEADS), f32)
                 * (math.log(0.1) - math.log(0.001)) + math.log(0.001))
    dt = jnp.maximum(dt, 1e-4)
    return {
        "x": jax.random.normal(ks[0], (BATCH, SEQ, D_MODEL), f32),
        "c": jax.random.normal(ks[1], (BATCH, D_MODEL), f32),
        "ada_w": nrm(ks[2], (DEPTH, D_MODEL, N_MOD * D_MODEL), 0.1 * D_MODEL ** -0.5),
        "ada_b": 0.01 * jax.random.normal(ks[3], (DEPTH, N_MOD * D_MODEL), f32) + gate_rows,
        "ffn1_norm": gain(ks[4], (DEPTH, D_MODEL)),
        "ffn1_w13": nrm(ks[5], (DEPTH, D_MODEL, 2 * D_FF), D_MODEL ** -0.5),
        "ffn1_w2": nrm(ks[6], (DEPTH, D_FF, D_MODEL), D_FF ** -0.5),
        "mix_norm": gain(ks[7], (DEPTH, D_MODEL)),
        "w_in": nrm(ks[8], (DEPTH, D_MODEL, IN_DIM), D_MODEL ** -0.5),
        "ssm_conv_w": nrm(ks[10], (DEPTH, SSM_CONV_DIM, SSM_CONV), SSM_CONV ** -0.5),
        "ssm_conv_b": nrm(ks[11], (DEPTH, SSM_CONV_DIM), 0.01),
        "ssm_dt_bias": dt + jnp.log(-jnp.expm1(-dt)),
        "ssm_a_log": jnp.log(jax.random.uniform(ks[12], (DEPTH, SSM_HEADS), f32, 1.0, 16.0)),
        "ssm_d": gain(ks[13], (DEPTH, SSM_HEADS)),
        "ssm_norm": gain(ks[14], (DEPTH, SSM_D_INNER)),
        "diff_lambda_q1": nrm(ks[15], (DEPTH, DIFF_QK_DIM), 0.1),
        "diff_lambda_k1": nrm(ks[16], (DEPTH, DIFF_QK_DIM), 0.1),
        "diff_lambda_q2": nrm(ks[17], (DEPTH, DIFF_QK_DIM), 0.1),
        "diff_lambda_k2": nrm(ks[18], (DEPTH, DIFF_QK_DIM), 0.1),
        "diff_subln": gain(ks[19], (DEPTH, DIFF_V_DIM)),
        "rel_bias": nrm(ks[20], (N_REL_BUCKETS, DIFF_HEADS), 0.5),
        "w_out": nrm(ks[21], (DEPTH, MIX_WIDTH, D_MODEL), MIX_WIDTH ** -0.5),
        "ffn2_norm": gain(ks[22], (DEPTH, D_MODEL)),
        "ffn2_w13": nrm(ks[23], (DEPTH, D_MODEL, 2 * D_FF), D_MODEL ** -0.5),
        "ffn2_w2": nrm(ks[24], (DEPTH, D_FF, D_MODEL), D_FF ** -0.5),
        "final_norm": gain(ks[25], (D_MODEL,)),
    }


def reference(x, c, ada_w, ada_b, ffn1_norm, ffn1_w13, ffn1_w2, mix_norm, w_in,
              ssm_conv_w, ssm_conv_b, ssm_dt_bias, ssm_a_log, ssm_d, ssm_norm,
              diff_lambda_q1, diff_lambda_k1, diff_lambda_q2, diff_lambda_k2, diff_subln,
              rel_bias, w_out, ffn2_norm, ffn2_w13, ffn2_w2, final_norm):
    b = x.shape[0]
    cond = jax.nn.silu(c)
    for l in range(DEPTH):
        mod = (cond @ ada_w[l] + ada_b[l]).reshape(b, N_MOD, D_MODEL)
        h = _modulate(_rmsnorm(x, ffn1_norm[l]), mod[:, 0], mod[:, 1])
        x = x + FFN_HALF * mod[:, 2][:, None, :] * _swiglu(h, ffn1_w13[l], ffn1_w2[l])
        lambda_init = 0.8 - 0.6 * math.exp(-0.3 * l)
        h = _modulate(_rmsnorm(x, mix_norm[l]), mod[:, 3], mod[:, 4])
        y = _hybrid_mixer(h, w_in[l], ssm_conv_w[l], ssm_conv_b[l], ssm_dt_bias[l], ssm_a_log[l],
                          ssm_d[l], ssm_norm[l], diff_lambda_q1[l], diff_lambda_k1[l],
                          diff_lambda_q2[l], diff_lambda_k2[l], diff_subln[l], rel_bias,
                          w_out[l], lambda_init)
        x = x + mod[:, 5][:, None, :] * y
        h = _modulate(_rmsnorm(x, ffn2_norm[l]), mod[:, 6], mod[:, 7])
        x = x + FFN_HALF * mod[:, 8][:, None, :] * _swiglu(h, ffn2_w13[l], ffn2_w2[l])
    return _rmsnorm(x, final_norm)
```

```python
import functools
import math

import numpy as np
import jax
import jax.numpy as jnp
from jax import lax
from jax.experimental import pallas as pl
from jax.experimental.pallas import tpu as pltpu

F32 = jnp.float32
BF16 = jnp.bfloat16
HIGHEST = lax.Precision.HIGHEST

HEAD_DIM = 64
SSM_HEADS = 16
SSM_GROUPS = 2
SSM_STATE = 128
SSM_CONV = 4
SSM_CHUNK = 128
SSM_D_INNER = SSM_HEADS * HEAD_DIM
SSM_CONV_DIM = SSM_D_INNER + 2 * SSM_GROUPS * SSM_STATE
SB_HEADS = 8
SB_WIDTH = SB_HEADS * HEAD_DIM
DIFF_HEADS = 4
DIFF_V_DIM = 2 * HEAD_DIM
DIFF_WIDTH = DIFF_HEADS * DIFF_V_DIM
N_MOD = 9
N_REL_BUCKETS = 32
REL_MAX_DIST = 128
EPS = 1e-6

LANES = 128
MXU_DIM = 256
VMEM_LIMIT = 56 * 1024 * 1024

TOKEN_TILE = 512
FF_CHUNK = 256
ATT_BLOCK = 256

OFF_Z = 0
OFF_XBC = OFF_Z + SSM_D_INNER
OFF_SQ = OFF_XBC + SSM_CONV_DIM
OFF_SK = OFF_SQ + SB_WIDTH
OFF_SV = OFF_SK + SB_WIDTH
OFF_DQ = OFF_SV + SB_WIDTH
OFF_DK = OFF_DQ + DIFF_WIDTH
OFF_DV = OFF_DK + DIFF_WIDTH
OFF_DT = OFF_DV + DIFF_WIDTH
IN_PACKED = OFF_DT + LANES


def _cparams(semantics):
    return pltpu.CompilerParams(dimension_semantics=semantics, vmem_limit_bytes=VMEM_LIMIT)


def _resident(block_shape, index_map):
    return pl.BlockSpec(block_shape, index_map, pipeline_mode=pl.Buffered(1))


def _sigmoid(x):
    return 1.0 / (1.0 + jnp.exp(-x))


def _silu(x):
    return x * _sigmoid(x)


def _softplus(x):
    return jnp.maximum(x, 0.0) + jnp.log1p(jnp.exp(-jnp.abs(x)))


def _norm_modulate(x, g, shift, scale):
    ms = jnp.mean(x * x, axis=-1, keepdims=True)
    y = x * lax.rsqrt(ms + EPS) * g
    return y * (1.0 + scale) + shift


def _ada_kernel(c_ref, w_ref, b_ref, o_ref):
    cond = _silu(c_ref[...])
    o_ref[...] = jnp.dot(cond, w_ref[...], preferred_element_type=F32, precision=HIGHEST) + b_ref[...]


def _ada_modulation(c, ada_w, ada_b):
    depth, d, nmod = ada_w.shape
    b = c.shape[0]
    rows = 8 * pl.cdiv(b, 8)
    c_pad = jnp.zeros((rows, d), F32).at[:b].set(c)
    out = pl.pallas_call(
        _ada_kernel,
        grid=(depth, nmod // d),
        in_specs=[
            pl.BlockSpec((rows, d), lambda l, j: (0, 0)),
            pl.BlockSpec((None, d, d), lambda l, j: (l, 0, j)),
            pl.BlockSpec((None, 1, d), lambda l, j: (l, 0, j)),
        ],
        out_specs=pl.BlockSpec((None, rows, d), lambda l, j: (l, 0, j)),
        out_shape=jax.ShapeDtypeStruct((depth, rows, nmod), F32),
        compiler_params=_cparams(("arbitrary", "arbitrary")),
        name="ada_modulation",
    )(c_pad, ada_w, ada_b.reshape(depth, 1, nmod))
    return out[:, :b].reshape(depth, b, N_MOD, d)


def _ffn_kernel(x_ref, mod_ref, g_ref, w1_ref, w3_ref, w2_ref, *rest, mod_row, final_norm):
    if final_norm:
        fg_ref, o_ref, acc_ref = rest
    else:
        o_ref, acc_ref = rest
    x = x_ref[...]
    shift = mod_ref[mod_row:mod_row + 1, :]
    scale = mod_ref[mod_row + 1:mod_row + 2, :]
    gate = mod_ref[mod_row + 2:mod_row + 3, :]
    h = _norm_modulate(x, g_ref[...], shift, scale).astype(BF16)
    n_chunks = w1_ref.shape[0]
    for j in range(n_chunks):
        a = jnp.dot(h, w1_ref[j], preferred_element_type=F32)
        u = jnp.dot(h, w3_ref[j], preferred_element_type=F32)
        act = (_silu(a) * u).astype(BF16)
        part = jnp.dot(act, w2_ref[j], preferred_element_type=F32)
        if j == 0:
            acc_ref[...] = part
        else:
            acc_ref[...] += part
    y = x + (0.5 * gate) * acc_ref[...]
    if final_norm:
        ms = jnp.mean(y * y, axis=-1, keepdims=True)
        y = y * lax.rsqrt(ms + EPS) * fg_ref[...]
    o_ref[...] = y


def _ffn(x, mod, g, w13, w2, mod_row, final_g=None):
    b, s, d = x.shape
    d_ff = w2.shape[0]
    n_chunks = d_ff // FF_CHUNK
    tm = min(TOKEN_TILE, s)
    w13b = w13.astype(BF16)
    w1c = w13b[:, :d_ff].reshape(d, n_chunks, FF_CHUNK).transpose(1, 0, 2)
    w3c = w13b[:, d_ff:].reshape(d, n_chunks, FF_CHUNK).transpose(1, 0, 2)
    w2c = w2.astype(BF16).reshape(n_chunks, FF_CHUNK, d)
    final_norm = final_g is not None
    in_specs = [
        pl.BlockSpec((None, tm, d), lambda bi, i: (bi, i, 0)),
        pl.BlockSpec((None, N_MOD, d), lambda bi, i: (bi, 0, 0)),
        _resident((1, d), lambda bi, i: (0, 0)),
        _resident((n_chunks, d, FF_CHUNK), lambda bi, i: (0, 0, 0)),
        _resident((n_chunks, d, FF_CHUNK), lambda bi, i: (0, 0, 0)),
        _resident((n_chunks, FF_CHUNK, d), lambda bi, i: (0, 0, 0)),
    ]
    args = [x, mod, g.reshape(1, d), w1c, w3c, w2c]
    if final_norm:
        in_specs.append(_resident((1, d), lambda bi, i: (0, 0)))
        args.append(final_g.reshape(1, d))
    return pl.pallas_call(
        functools.partial(_ffn_kernel, mod_row=mod_row, final_norm=final_norm),
        grid=(b, s // tm),
        in_specs=in_specs,
        out_specs=pl.BlockSpec((None, tm, d), lambda bi, i: (bi, i, 0)),
        out_shape=jax.ShapeDtypeStruct((b, s, d), F32),
        scratch_shapes=[pltpu.VMEM((tm, d), F32)],
        compiler_params=_cparams(("parallel", "parallel")),
        name="ffn_final" if final_norm else "ffn",
    )(*args)


def _inproj_kernel(x_ref, mod_ref, g_ref, w_ref, z_ref, xbc_ref, dt_ref,
                   sq_ref, sk_ref, sv_ref, dq_ref, dk_ref, dv_ref):
    x = x_ref[...]
    h = _norm_modulate(x, g_ref[...], mod_ref[3:4, :], mod_ref[4:5, :]).astype(BF16)

    def proj(off, width):
        return jnp.dot(h, w_ref[:, off:off + width], preferred_element_type=F32)

    qk_scale = HEAD_DIM ** -0.5
    z_ref[...] = proj(OFF_Z, SSM_D_INNER).astype(BF16)
    xbc_ref[...] = proj(OFF_XBC, SSM_CONV_DIM).astype(BF16)
    dt_ref[...] = proj(OFF_DT, LANES)
    sq_ref[...] = (proj(OFF_SQ, SB_WIDTH) * qk_scale).astype(BF16)
    sk_ref[...] = proj(OFF_SK, SB_WIDTH).astype(BF16)
    sv_ref[...] = proj(OFF_SV, SB_WIDTH).astype(BF16)
    dq_ref[...] = (proj(OFF_DQ, DIFF_WIDTH) * qk_scale).astype(BF16)
    dk_ref[...] = proj(OFF_DK, DIFF_WIDTH).astype(BF16)
    dv_ref[...] = proj(OFF_DV, DIFF_WIDTH).astype(BF16)


def _pack_w_in(w_in):
    d = w_in.shape[0]
    dt0 = SSM_D_INNER + SSM_CONV_DIM
    dt1 = dt0 + SSM_HEADS
    packed = jnp.concatenate(
        [w_in[:, :dt0], w_in[:, dt1:], w_in[:, dt0:dt1], jnp.zeros((d, LANES - SSM_HEADS), w_in.dtype)], axis=1)
    return packed.astype(BF16)


def _inproj(x, mod, g, w_in):
    b, s, d = x.shape
    tm = min(TOKEN_TILE, s)
    w = _pack_w_in(w_in)
    widths = (SSM_D_INNER, SSM_CONV_DIM, LANES) + (SB_WIDTH,) * 3 + (DIFF_WIDTH,) * 3
    dtypes = (BF16, BF16, F32) + (BF16,) * 6
    return pl.pallas_call(
        _inproj_kernel,
        grid=(b, s // tm),
        in_specs=[
            pl.BlockSpec((None, tm, d), lambda bi, i: (bi, i, 0)),
            pl.BlockSpec((None, N_MOD, d), lambda bi, i: (bi, 0, 0)),
            _resident((1, d), lambda bi, i: (0, 0)),
            _resident((d, IN_PACKED), lambda bi, i: (0, 0)),
        ],
        out_specs=[pl.BlockSpec((None, tm, wd), lambda bi, i: (bi, i, 0)) for wd in widths],
        out_shape=[jax.ShapeDtypeStruct((b, s, wd), dt) for wd, dt in zip(widths, dtypes)],
        compiler_params=_cparams(("parallel", "parallel")),
        name="mixer_inproj",
    )(x, mod, g.reshape(1, d), w)


def _ssd_kernel(z_ref, xbc_ref, dtr_ref, cw_ref, cb_ref, dtb_ref, alog_ref, dsk_ref, ng_ref, e_ref,
                y_ref, cbuf_ref, state_ref):
    L = z_ref.shape[0]
    pad = 8
    c = pl.program_id(1)

    @pl.when(c == 0)
    def _():
        cbuf_ref[0:pad, :] = jnp.zeros((pad, SSM_CONV_DIM), F32)
        state_ref[...] = jnp.zeros(state_ref.shape, F32)

    cbuf_ref[pad:pad + L, :] = xbc_ref[...].astype(F32)
    conv = cb_ref[...]
    for k in range(SSM_CONV):
        conv = conv + cw_ref[k:k + 1, :] * cbuf_ref[pl.ds(pad - (SSM_CONV - 1) + k, L), :]
    cbuf_ref[0:pad, :] = cbuf_ref[L:L + pad, :]
    xa = _silu(conv)
    xs = xa[:, :SSM_D_INNER]
    bmat = xa[:, SSM_D_INNER:SSM_D_INNER + SSM_GROUPS * SSM_STATE]
    cmat = xa[:, SSM_D_INNER + SSM_GROUPS * SSM_STATE:]

    dt = _softplus(dtr_ref[...] + dtb_ref[...])
    a = -jnp.exp(alog_ref[...])
    da = dt * a
    row = lax.broadcasted_iota(jnp.int32, (L, L), 0)
    col = lax.broadcasted_iota(jnp.int32, (L, L), 1)
    causal = row >= col
    a_cum = jnp.dot(causal.astype(F32), da, preferred_element_type=F32, precision=HIGHEST)
    a_cum_t = a_cum.T
    exp_a = jnp.exp(a_cum)
    decay_to_end = jnp.exp(a_cum[L - 1:L, :] - a_cum)

    stacked = jnp.concatenate([dt, exp_a, decay_to_end], axis=0)
    hi = stacked.astype(BF16)
    lo = (stacked - hi.astype(F32)).astype(BF16)
    spread = jnp.dot(jnp.concatenate([hi, lo], axis=1), e_ref[...], preferred_element_type=F32)
    dt_e = spread[0:L]
    exp_a_e = spread[L:2 * L]
    dte_e = spread[2 * L:3 * L]

    xd = xs * dt_e
    xd_b = xd.astype(BF16)
    xdd_b = (xd * dte_e).astype(BF16)
    prev_b = state_ref[...].astype(BF16)
    lane = lax.broadcasted_iota(jnp.int32, (L, LANES), 1)
    heads_per_group = SSM_HEADS // SSM_GROUPS
    group_width = heads_per_group * HEAD_DIM

    y_diag_parts = []
    y_off_parts = []
    s_new_parts = []
    for g in range(SSM_GROUPS):
        bg = bmat[:, g * SSM_STATE:(g + 1) * SSM_STATE]
        cg_b = cmat[:, g * SSM_STATE:(g + 1) * SSM_STATE].astype(BF16)
        bg_b = bg.astype(BF16)
        bg_t_b = bg.T.astype(BF16)
        gs = slice(g * group_width, (g + 1) * group_width)
        cb = lax.dot_general(cg_b, bg_b, (((1,), (1,)), ((), ())), preferred_element_type=F32)
        y_off_parts.append(jnp.dot(cg_b, prev_b[:, gs], preferred_element_type=F32))
        s_new_parts.append(jnp.dot(bg_t_b, xdd_b[:, gs], preferred_element_type=F32))
        for hp in range(heads_per_group // 2):
            ms = []
            for h in (g * heads_per_group + 2 * hp, g * heads_per_group + 2 * hp + 1):
                seg = a_cum[:, h:h + 1] - a_cum_t[h:h + 1, :]
                decay = jnp.exp(jnp.where(causal, seg, -jnp.inf))
                ms.append((cb * decay).astype(BF16))
            pair = (g * heads_per_group) // 2 + hp
            yp = jnp.dot(jnp.concatenate(ms, axis=0), xd_b[:, pair * LANES:(pair + 1) * LANES],
                         preferred_element_type=F32)
            y_diag_parts.append(jnp.where(lane < HEAD_DIM, yp[:L], yp[L:]))
    y_diag = jnp.concatenate(y_diag_parts, axis=1)
    y_off = jnp.concatenate(y_off_parts, axis=1)
    s_new = jnp.concatenate(s_new_parts, axis=1)

    y = y_diag + y_off * exp_a_e + dsk_ref[...] * xs
    state_ref[...] = state_ref[...] * exp_a_e[L - 1:L, :] + s_new

    y = y * _silu(z_ref[...].astype(F32))
    outs = []
    for g in range(SSM_GROUPS):
        yg = y[:, g * group_width:(g + 1) * group_width]
        ms = jnp.mean(yg * yg, axis=-1, keepdims=True)
        outs.append(yg * lax.rsqrt(ms + EPS))
    y_ref[...] = (jnp.concatenate(outs, axis=1) * ng_ref[...]).astype(BF16)


def _head_spread_matrix():
    e = np.zeros((2 * LANES, SSM_D_INNER), np.float32)
    for h in range(SSM_HEADS):
        e[h, h * HEAD_DIM:(h + 1) * HEAD_DIM] = 1.0
        e[LANES + h, h * HEAD_DIM:(h + 1) * HEAD_DIM] = 1.0
    return jnp.asarray(e, BF16)


def _pad_lanes(v):
    return jnp.zeros((1, LANES), F32).at[0, :v.shape[0]].set(v)


def _ssd(z, xbc, dt_raw, conv_w, conv_b, dt_bias, a_log, d_skip, norm_g):
    b, s, _ = z.shape
    L = SSM_CHUNK
    row = lambda width: _resident((1, width), lambda bi, c: (0, 0))
    return pl.pallas_call(
        _ssd_kernel,
        grid=(b, s // L),
        in_specs=[
            pl.BlockSpec((None, L, SSM_D_INNER), lambda bi, c: (bi, c, 0)),
            pl.BlockSpec((None, L, SSM_CONV_DIM), lambda bi, c: (bi, c, 0)),
            pl.BlockSpec((None, L, LANES), lambda bi, c: (bi, c, 0)),
            _resident((SSM_CONV, SSM_CONV_DIM), lambda bi, c: (0, 0)),
            row(SSM_CONV_DIM), row(LANES), row(LANES), row(SSM_D_INNER), row(SSM_D_INNER),
            _resident((2 * LANES, SSM_D_INNER), lambda bi, c: (0, 0)),
        ],
        out_specs=pl.BlockSpec((None, L, SSM_D_INNER), lambda bi, c: (bi, c, 0)),
        out_shape=jax.ShapeDtypeStruct((b, s, SSM_D_INNER), BF16),
        scratch_shapes=[pltpu.VMEM((L + 8, SSM_CONV_DIM), F32), pltpu.VMEM((SSM_STATE, SSM_D_INNER), F32)],
        compiler_params=_cparams(("parallel", "arbitrary")),
        name="ssd_mixer",
    )(z, xbc, dt_raw, conv_w.T, conv_b.reshape(1, -1), _pad_lanes(dt_bias), _pad_lanes(a_log),
      jnp.repeat(d_skip, HEAD_DIM).reshape(1, -1), norm_g.reshape(1, -1), _head_spread_matrix())


def _split_head_pair(x2):
    xf = x2.astype(F32)
    lane = lax.broadcasted_iota(jnp.int32, xf.shape, 1)
    lo = jnp.where(lane < HEAD_DIM, xf, 0.0)
    hi = jnp.where(lane >= HEAD_DIM, xf, 0.0)
    return jnp.concatenate([lo, hi], axis=0).astype(BF16)


def _sb_kernel(q_ref, k_ref, v_ref, o_ref):
    T = q_ref.shape[0]
    i = pl.program_id(2)
    qs = _split_head_pair(q_ref[...])

    row = lax.broadcasted_iota(jnp.int32, (2 * T, T), 0)
    col = lax.broadcasted_iota(jnp.int32, (2 * T, T), 1)
    strict = col < jnp.where(row >= T, row - T, row)
    ur = lax.broadcasted_iota(jnp.int32, (T, T), 0)
    uc = lax.broadcasted_iota(jnp.int32, (T, T), 1)
    suffix = (ur >= uc).astype(F32).astype(BF16)

    def block(j, carry, acc, diagonal):
        start = pl.multiple_of(j * T, T)
        kb = k_ref[pl.ds(start, T), :]
        vb = v_ref[pl.ds(start, T), :]
        z = lax.dot_general(qs, kb, (((1,), (1,)), ((), ())), preferred_element_type=F32)
        nz = -z
        lsm = jnp.minimum(nz, 0.0) - jnp.log1p(jnp.exp(jnp.minimum(z, nz)))
        if diagonal:
            lsm = jnp.where(strict, lsm, 0.0)
        cs = jnp.dot(lsm.astype(BF16), suffix, preferred_element_type=F32)
        w = jnp.exp(z + cs + carry)
        if diagonal:
            w = jnp.where(strict, w, 0.0)
        acc = acc + jnp.dot(w.astype(BF16), vb, preferred_element_type=F32)
        carry = carry + cs[:, 0:1]
        return carry, acc

    carry0 = jnp.zeros((2 * T, 1), F32)
    acc0 = jnp.zeros((2 * T, LANES), F32)
    carry, acc = block(i, carry0, acc0, True)

    def body(t, ca):
        return block(i - 1 - t, ca[0], ca[1], False)

    carry, acc = lax.fori_loop(0, i, body, (carry, acc))
    lane = lax.broadcasted_iota(jnp.int32, (T, LANES), 1)
    o_ref[...] = jnp.where(lane < HEAD_DIM, acc[:T], acc[T:]).astype(BF16)


def _stick_breaking(q, k, v):
    b, s, width = q.shape
    T = min(ATT_BLOCK, s)
    pairs = width // LANES
    return pl.pallas_call(
        _sb_kernel,
        grid=(b, pairs, s // T),
        in_specs=[
            pl.BlockSpec((None, T, LANES), lambda bi, p, i: (bi, i, p)),
            pl.BlockSpec((None, s, LANES), lambda bi, p, i: (bi, 0, p)),
            pl.BlockSpec((None, s, LANES), lambda bi, p, i: (bi, 0, p)),
        ],
        out_specs=pl.BlockSpec((None, T, LANES), lambda bi, p, i: (bi, i, p)),
        out_shape=jax.ShapeDtypeStruct((b, s, width), BF16),
        compiler_params=_cparams(("parallel", "parallel", "arbitrary")),
        name="stick_breaking_attention",
    )(q, k, v)


def _bucket_starts():
    max_exact = N_REL_BUCKETS // 2
    dist = np.arange(0, 4 * REL_MAX_DIST, dtype=np.float64)
    ratio = np.log(np.maximum(dist, max_exact) / max_exact) / math.log(REL_MAX_DIST / max_exact)
    scaled = ratio * (N_REL_BUCKETS - max_exact)
    frac = np.abs(scaled - np.round(scaled))
    interior = (dist > max_exact) & (dist < REL_MAX_DIST)
    assert frac[interior].min() > 1e-3, "a bucket edge sits on an integer distance"
    large = np.minimum(max_exact + np.floor(scaled + 1e-9).astype(np.int64), N_REL_BUCKETS - 1)
    bucket = np.where(dist < max_exact, dist.astype(np.int64), large)
    assert np.all(np.diff(bucket) >= 0) and bucket[-1] == N_REL_BUCKETS - 1
    return [int(np.argmax(bucket >= bkt)) for bkt in range(N_REL_BUCKETS)]


def _bias_table_kernel(rel_ref, o_ref, *, starts):
    h = pl.program_id(0)
    T = o_ref.shape[-1]
    row = lax.broadcasted_iota(jnp.int32, (T, T), 0)
    col = lax.broadcasted_iota(jnp.int32, (T, T), 1)
    for o_blk in range(o_ref.shape[0]):
        dist = row - col + o_blk * T
        bias = jnp.full((T, T), rel_ref[0, h], F32)
        for bkt in range(1, N_REL_BUCKETS):
            bias = jnp.where(dist >= starts[bkt], rel_ref[bkt, h], bias)
        o_ref[o_blk] = jnp.where(dist >= 0, bias, -jnp.inf)


def _bias_tables(rel_bias, T):
    starts = _bucket_starts()
    assert T + 1 >= starts[-1]
    return pl.pallas_call(
        functools.partial(_bias_table_kernel, starts=starts),
        grid=(DIFF_HEADS,),
        in_specs=[pl.BlockSpec(memory_space=pltpu.SMEM)],
        out_specs=pl.BlockSpec((None, 3, T, T), lambda h: (h, 0, 0, 0)),
        out_shape=jax.ShapeDtypeStruct((DIFF_HEADS, 3, T, T), F32),
        compiler_params=_cparams(("arbitrary",)),
        name="t5_bias_tables",
    )(rel_bias)


def _diff_kernel(q_ref, k_ref, v_ref, tb_ref, lq1_ref, lk1_ref, lq2_ref, lk2_ref, sg_ref, o_ref,
                 *, lambda_init):
    T = q_ref.shape[0]
    i = pl.program_id(2)
    qs = _split_head_pair(q_ref[...])

    def body(j, mla):
        m, l, acc = mla
        start = pl.multiple_of(j * T, T)
        kb = k_ref[pl.ds(start, T), :]
        vb = v_ref[pl.ds(start, T), :]
        bias = tb_ref[jnp.minimum(i - j, 2)]
        s = lax.dot_general(qs, kb, (((1,), (1,)), ((), ())), preferred_element_type=F32)
        s = s + jnp.concatenate([bias, bias], axis=0)
        m_new = jnp.maximum(m, jnp.max(s, axis=-1, keepdims=True))
        alpha = jnp.exp(m - m_new)
        p = jnp.exp(s - m_new)
        l = alpha * l + jnp.sum(p, axis=-1, keepdims=True)
        acc = alpha * acc + jnp.dot(p.astype(BF16), vb, preferred_element_type=F32)
        return m_new, l, acc

    m0 = jnp.full((2 * T, 1), -jnp.inf, F32)
    l0 = jnp.zeros((2 * T, 1), F32)
    acc0 = jnp.zeros((2 * T, DIFF_V_DIM), F32)
    _, l, acc = lax.fori_loop(0, i + 1, body, (m0, l0, acc0))

    lam = (jnp.exp(jnp.sum(lq1_ref[...] * lk1_ref[...], axis=-1, keepdims=True))
           - jnp.exp(jnp.sum(lq2_ref[...] * lk2_ref[...], axis=-1, keepdims=True)) + lambda_init)
    o = acc[:T] / l[:T] - lam * (acc[T:] / l[T:])
    ms = jnp.mean(o * o, axis=-1, keepdims=True)
    o = o * lax.rsqrt(ms + EPS) * sg_ref[...] * (1.0 - lambda_init)
    o_ref[...] = o.astype(BF16)


def _diff_attention(q, k, v, tables, lq1, lk1, lq2, lk2, subln_g, lambda_init):
    b, s, width = q.shape
    T = tables.shape[-1]
    vec = lambda n: _resident((1, n), lambda bi, h, i: (0, 0))
    return pl.pallas_call(
        functools.partial(_diff_kernel, lambda_init=lambda_init),
        grid=(b, DIFF_HEADS, s // T),
        in_specs=[
            pl.BlockSpec((None, T, LANES), lambda bi, h, i: (bi, i, h)),
            pl.BlockSpec((None, s, LANES), lambda bi, h, i: (bi, 0, h)),
            pl.BlockSpec((None, s, LANES), lambda bi, h, i: (bi, 0, h)),
            pl.BlockSpec((None, 3, T, T), lambda bi, h, i: (h, 0, 0, 0)),
            vec(HEAD_DIM), vec(HEAD_DIM), vec(HEAD_DIM), vec(HEAD_DIM), vec(DIFF_V_DIM),
        ],
        out_specs=pl.BlockSpec((None, T, LANES), lambda bi, h, i: (bi, i, h)),
        out_shape=jax.ShapeDtypeStruct((b, s, width), BF16),
        compiler_params=_cparams(("parallel", "parallel", "arbitrary")),
        name="differential_attention",
    )(q, k, v, tables, lq1.reshape(1, -1), lk1.reshape(1, -1), lq2.reshape(1, -1), lk2.reshape(1, -1),
      subln_g.reshape(1, -1))


def _outproj_kernel(x_ref, mod_ref, ys_ref, yb_ref, yd_ref, w_ref, o_ref):
    o1 = SSM_D_INNER
    o2 = o1 + SB_WIDTH
    y = jnp.dot(ys_ref[...], w_ref[0:o1, :], preferred_element_type=F32)
    y = y + jnp.dot(yb_ref[...], w_ref[o1:o2, :], preferred_element_type=F32)
    y = y + jnp.dot(yd_ref[...], w_ref[o2:, :], preferred_element_type=F32)
    o_ref[...] = x_ref[...] + mod_ref[5:6, :] * y


def _outproj(x, mod, y_ssm, y_sb, y_diff, w_out):
    b, s, d = x.shape
    tm = min(TOKEN_TILE, s)
    tok = lambda width: pl.BlockSpec((None, tm, width), lambda bi, i: (bi, i, 0))
    return pl.pallas_call(
        _outproj_kernel,
        grid=(b, s // tm),
        in_specs=[
            tok(d),
            pl.BlockSpec((None, N_MOD, d), lambda bi, i: (bi, 0, 0)),
            tok(SSM_D_INNER), tok(SB_WIDTH), tok(DIFF_WIDTH),
            _resident(w_out.shape, lambda bi, i: (0, 0)),
        ],
        out_specs=tok(d),
        out_shape=jax.ShapeDtypeStruct((b, s, d), F32),
        compiler_params=_cparams(("parallel", "parallel")),
        name="mixer_outproj",
    )(x, mod, y_ssm, y_sb, y_diff, w_out.astype(BF16))


def kernel(x, c, ada_w, ada_b, ffn1_norm, ffn1_w13, ffn1_w2, mix_norm, w_in, ssm_conv_w, ssm_conv_b, ssm_dt_bias, ssm_a_log, ssm_d, ssm_norm, diff_lambda_q1, diff_lambda_k1, diff_lambda_q2, diff_lambda_k2, diff_subln, rel_bias, w_out, ffn2_norm, ffn2_w13, ffn2_w2, final_norm):
    depth = ada_w.shape[0]
    s = x.shape[1]
    mods = _ada_modulation(c, ada_w, ada_b)
    tables = _bias_tables(rel_bias, min(ATT_BLOCK, s))
    for l in range(depth):
        mod = mods[l]
        x = _ffn(x, mod, ffn1_norm[l], ffn1_w13[l], ffn1_w2[l], mod_row=0)
        lambda_init = 0.8 - 0.6 * math.exp(-0.3 * l)
        z, xbc, dt_raw, sq, sk, sv, dq, dk, dv = _inproj(x, mod, mix_norm[l], w_in[l])
        y_ssm = _ssd(z, xbc, dt_raw, ssm_conv_w[l], ssm_conv_b[l], ssm_dt_bias[l], ssm_a_log[l], ssm_d[l],
                     ssm_norm[l])
        y_sb = _stick_breaking(sq, sk, sv)
        y_diff = _diff_attention(dq, dk, dv, tables, diff_lambda_q1[l], diff_lambda_k1[l], diff_lambda_q2[l],
                                 diff_lambda_k2[l], diff_subln[l], lambda_init)
        x = _outproj(x, mod, y_ssm, y_sb, y_diff, w_out[l])
        x = _ffn(x, mod, ffn2_norm[l], ffn2_w13[l], ffn2_w2[l], mod_row=6,
                 final_g=final_norm if l == depth - 1 else None)
    return x
```

```python
import functools
import math

import numpy as np
import jax
import jax.numpy as jnp
from jax import lax
from jax.experimental import pallas as pl
from jax.experimental.pallas import tpu as pltpu

F32 = jnp.float32
BF16 = jnp.bfloat16
HIGHEST = lax.Precision.HIGHEST
LOG2E = math.log2(math.e)

HEAD_DIM = 64
SSM_HEADS = 16
SSM_GROUPS = 2
SSM_STATE = 128
SSM_CONV = 4
SSM_CHUNK = 128
SSM_D_INNER = SSM_HEADS * HEAD_DIM
SSM_CONV_DIM = SSM_D_INNER + 2 * SSM_GROUPS * SSM_STATE
SB_HEADS = 8
SB_WIDTH = SB_HEADS * HEAD_DIM
DIFF_HEADS = 4
DIFF_V_DIM = 2 * HEAD_DIM
DIFF_WIDTH = DIFF_HEADS * DIFF_V_DIM
N_MOD = 9
N_REL_BUCKETS = 32
REL_MAX_DIST = 128
EPS = 1e-6

LANES = 128
VMEM_LIMIT = 56 * 1024 * 1024

TOKEN_TILE = 512
FF_CHUNK = 256
ATT_BLOCK = 256

OFF_Z = 0
OFF_XBC = OFF_Z + SSM_D_INNER
OFF_SQ = OFF_XBC + SSM_CONV_DIM
OFF_SK = OFF_SQ + SB_WIDTH
OFF_SV = OFF_SK + SB_WIDTH
OFF_DQ = OFF_SV + SB_WIDTH
OFF_DK = OFF_DQ + DIFF_WIDTH
OFF_DV = OFF_DK + DIFF_WIDTH
OFF_DT = OFF_DV + DIFF_WIDTH
IN_PACKED = OFF_DT + LANES


def _cparams(semantics):
    return pltpu.CompilerParams(dimension_semantics=semantics, vmem_limit_bytes=VMEM_LIMIT)


def _resident(block_shape, index_map):
    return pl.BlockSpec(block_shape, index_map, pipeline_mode=pl.Buffered(1))


def _sigmoid(x):
    return 1.0 / (1.0 + jnp.exp(-x))


def _silu(x):
    return x * _sigmoid(x)


def _softplus(x):
    return jnp.maximum(x, 0.0) + jnp.log1p(jnp.exp(-jnp.abs(x)))


def _norm_modulate(x, g, shift, scale):
    ms = jnp.mean(x * x, axis=-1, keepdims=True)
    y = x * lax.rsqrt(ms + EPS) * g
    return y * (1.0 + scale) + shift


def _ada_kernel(c_ref, w_ref, b_ref, o_ref):
    cond = _silu(c_ref[...])
    o_ref[...] = jnp.dot(cond, w_ref[...], preferred_element_type=F32, precision=HIGHEST) + b_ref[...]


def _ada_modulation(c, ada_w, ada_b):
    depth, d, nmod = ada_w.shape
    b = c.shape[0]
    rows = 8 * pl.cdiv(b, 8)
    c_pad = jnp.zeros((rows, d), F32).at[:b].set(c)
    out = pl.pallas_call(
        _ada_kernel,
        grid=(depth, nmod // d),
        in_specs=[
            pl.BlockSpec((rows, d), lambda l, j: (0, 0)),
            pl.BlockSpec((None, d, d), lambda l, j: (l, 0, j)),
            pl.BlockSpec((None, 1, d), lambda l, j: (l, 0, j)),
        ],
        out_specs=pl.BlockSpec((None, rows, d), lambda l, j: (l, 0, j)),
        out_shape=jax.ShapeDtypeStruct((depth, rows, nmod), F32),
        compiler_params=_cparams(("arbitrary", "arbitrary")),
        name="ada_modulation",
    )(c_pad, ada_w, ada_b.reshape(depth, 1, nmod))
    return out[:, :b].reshape(depth, b, N_MOD, d)


def _ffn_kernel(x_ref, mod_ref, g_ref, w1_ref, w3_ref, w2_ref, *rest, mod_row, final_norm):
    if final_norm:
        fg_ref, o_ref, acc_ref = rest
    else:
        o_ref, acc_ref = rest
    x = x_ref[...]
    shift = mod_ref[mod_row:mod_row + 1, :]
    scale = mod_ref[mod_row + 1:mod_row + 2, :]
    gate = mod_ref[mod_row + 2:mod_row + 3, :]
    h = _norm_modulate(x, g_ref[...], shift, scale).astype(BF16)
    n_chunks = w1_ref.shape[0]
    for j in range(n_chunks):
        a = jnp.dot(h, w1_ref[j], preferred_element_type=F32)
        u = jnp.dot(h, w3_ref[j], preferred_element_type=F32)
        act = (_silu(a) * u).astype(BF16)
        part = jnp.dot(act, w2_ref[j], preferred_element_type=F32)
        if j == 0:
            acc_ref[...] = part
        else:
            acc_ref[...] += part
    y = x + (0.5 * gate) * acc_ref[...]
    if final_norm:
        ms = jnp.mean(y * y, axis=-1, keepdims=True)
        y = y * lax.rsqrt(ms + EPS) * fg_ref[...]
    o_ref[...] = y


def _ffn(x, mod, g, w13, w2, mod_row, final_g=None):
    b, s, d = x.shape
    d_ff = w2.shape[0]
    n_chunks = d_ff // FF_CHUNK
    tm = min(TOKEN_TILE, s)
    w13b = w13.astype(BF16)
    w1c = w13b[:, :d_ff].reshape(d, n_chunks, FF_CHUNK).transpose(1, 0, 2)
    w3c = w13b[:, d_ff:].reshape(d, n_chunks, FF_CHUNK).transpose(1, 0, 2)
    w2c = w2.astype(BF16).reshape(n_chunks, FF_CHUNK, d)
    final_norm = final_g is not None
    in_specs = [
        pl.BlockSpec((None, tm, d), lambda bi, i: (bi, i, 0)),
        pl.BlockSpec((None, N_MOD, d), lambda bi, i: (bi, 0, 0)),
        _resident((1, d), lambda bi, i: (0, 0)),
        _resident((n_chunks, d, FF_CHUNK), lambda bi, i: (0, 0, 0)),
        _resident((n_chunks, d, FF_CHUNK), lambda bi, i: (0, 0, 0)),
        _resident((n_chunks, FF_CHUNK, d), lambda bi, i: (0, 0, 0)),
    ]
    args = [x, mod, g.reshape(1, d), w1c, w3c, w2c]
    if final_norm:
        in_specs.append(_resident((1, d), lambda bi, i: (0, 0)))
        args.append(final_g.reshape(1, d))
    return pl.pallas_call(
        functools.partial(_ffn_kernel, mod_row=mod_row, final_norm=final_norm),
        grid=(b, s // tm),
        in_specs=in_specs,
        out_specs=pl.BlockSpec((None, tm, d), lambda bi, i: (bi, i, 0)),
        out_shape=jax.ShapeDtypeStruct((b, s, d), F32),
        scratch_shapes=[pltpu.VMEM((tm, d), F32)],
        compiler_params=_cparams(("parallel", "parallel")),
        name="ffn_final" if final_norm else "ffn",
    )(*args)


def _inproj_kernel(x_ref, mod_ref, g_ref, w_ref, z_ref, xbc_ref, dt_ref,
                   sq_ref, sk_ref, sv_ref, dq_ref, dk_ref, dv_ref):
    x = x_ref[...]
    h = _norm_modulate(x, g_ref[...], mod_ref[3:4, :], mod_ref[4:5, :]).astype(BF16)

    def proj(off, width):
        return jnp.dot(h, w_ref[:, off:off + width], preferred_element_type=F32)

    qk_scale = HEAD_DIM ** -0.5 * LOG2E
    z_ref[...] = proj(OFF_Z, SSM_D_INNER).astype(BF16)
    xbc_ref[...] = proj(OFF_XBC, SSM_CONV_DIM).astype(BF16)
    dt_ref[...] = proj(OFF_DT, LANES)
    sq_ref[...] = (proj(OFF_SQ, SB_WIDTH) * qk_scale).astype(BF16)
    sk_ref[...] = proj(OFF_SK, SB_WIDTH).astype(BF16)
    sv_ref[...] = proj(OFF_SV, SB_WIDTH).astype(BF16)
    dq_ref[...] = (proj(OFF_DQ, DIFF_WIDTH) * qk_scale).astype(BF16)
    dk_ref[...] = proj(OFF_DK, DIFF_WIDTH).astype(BF16)
    dv_ref[...] = proj(OFF_DV, DIFF_WIDTH).astype(BF16)


def _pack_w_in(w_in):
    d = w_in.shape[0]
    dt0 = SSM_D_INNER + SSM_CONV_DIM
    dt1 = dt0 + SSM_HEADS
    packed = jnp.concatenate(
        [w_in[:, :dt0], w_in[:, dt1:], w_in[:, dt0:dt1], jnp.zeros((d, LANES - SSM_HEADS), w_in.dtype)], axis=1)
    return packed.astype(BF16)


def _inproj(x, mod, g, w_in):
    b, s, d = x.shape
    tm = min(TOKEN_TILE, s)
    w = _pack_w_in(w_in)
    widths = (SSM_D_INNER, SSM_CONV_DIM, LANES) + (SB_WIDTH,) * 3 + (DIFF_WIDTH,) * 3
    dtypes = (BF16, BF16, F32) + (BF16,) * 6
    return pl.pallas_call(
        _inproj_kernel,
        grid=(b, s // tm),
        in_specs=[
            pl.BlockSpec((None, tm, d), lambda bi, i: (bi, i, 0)),
            pl.BlockSpec((None, N_MOD, d), lambda bi, i: (bi, 0, 0)),
            _resident((1, d), lambda bi, i: (0, 0)),
            _resident((d, IN_PACKED), lambda bi, i: (0, 0)),
        ],
        out_specs=[pl.BlockSpec((None, tm, wd), lambda bi, i: (bi, i, 0)) for wd in widths],
        out_shape=[jax.ShapeDtypeStruct((b, s, wd), dt) for wd, dt in zip(widths, dtypes)],
        compiler_params=_cparams(("parallel", "parallel")),
        name="mixer_inproj",
    )(x, mod, g.reshape(1, d), w)


def _ssd_kernel(z_ref, xbc_ref, dtr_ref, cw_ref, cb_ref, dtb_ref, alog_ref, dsk_ref, ng_ref, e_ref,
                y_ref, cbuf_ref, state_ref):
    L = z_ref.shape[0]
    pad = 8
    c = pl.program_id(1)

    @pl.when(c == 0)
    def _():
        cbuf_ref[0:pad, :] = jnp.zeros((pad, SSM_CONV_DIM), F32)
        state_ref[...] = jnp.zeros(state_ref.shape, F32)

    cbuf_ref[pad:pad + L, :] = xbc_ref[...].astype(F32)
    conv = cb_ref[...]
    for k in range(SSM_CONV):
        conv = conv + cw_ref[k:k + 1, :] * cbuf_ref[pl.ds(pad - (SSM_CONV - 1) + k, L), :]
    cbuf_ref[0:pad, :] = cbuf_ref[L:L + pad, :]
    xa = _silu(conv)
    xs = xa[:, :SSM_D_INNER]
    bmat = xa[:, SSM_D_INNER:SSM_D_INNER + SSM_GROUPS * SSM_STATE]
    cmat = xa[:, SSM_D_INNER + SSM_GROUPS * SSM_STATE:]

    dt = _softplus(dtr_ref[...] + dtb_ref[...])
    a = -jnp.exp(alog_ref[...])
    da = dt * a
    row = lax.broadcasted_iota(jnp.int32, (L, L), 0)
    col = lax.broadcasted_iota(jnp.int32, (L, L), 1)
    causal = row >= col
    a_cum = jnp.dot(causal.astype(F32), da, preferred_element_type=F32, precision=HIGHEST)
    a_cum_t = a_cum.T
    exp_a = jnp.exp(a_cum)
    decay_to_end = jnp.exp(a_cum[L - 1:L, :] - a_cum)

    stacked = jnp.concatenate([dt, exp_a, decay_to_end], axis=0)
    hi = stacked.astype(BF16)
    lo = (stacked - hi.astype(F32)).astype(BF16)
    spread = jnp.dot(jnp.concatenate([hi, lo], axis=1), e_ref[...], preferred_element_type=F32)
    dt_e = spread[0:L]
    exp_a_e = spread[L:2 * L]
    dte_e = spread[2 * L:3 * L]

    xd = xs * dt_e
    xd_b = xd.astype(BF16)
    xdd_b = (xd * dte_e).astype(BF16)
    prev_b = state_ref[...].astype(BF16)
    lane = lax.broadcasted_iota(jnp.int32, (L, LANES), 1)
    heads_per_group = SSM_HEADS // SSM_GROUPS
    group_width = heads_per_group * HEAD_DIM

    y_diag_parts = []
    y_off_parts = []
    s_new_parts = []
    for g in range(SSM_GROUPS):
        bg = bmat[:, g * SSM_STATE:(g + 1) * SSM_STATE]
        cg_b = cmat[:, g * SSM_STATE:(g + 1) * SSM_STATE].astype(BF16)
        bg_b = bg.astype(BF16)
        bg_t_b = bg.T.astype(BF16)
        gs = slice(g * group_width, (g + 1) * group_width)
        cb = lax.dot_general(cg_b, bg_b, (((1,), (1,)), ((), ())), preferred_element_type=F32)
        y_off_parts.append(jnp.dot(cg_b, prev_b[:, gs], preferred_element_type=F32))
        s_new_parts.append(jnp.dot(bg_t_b, xdd_b[:, gs], preferred_element_type=F32))
        for hp in range(heads_per_group // 2):
            ms = []
            for h in (g * heads_per_group + 2 * hp, g * heads_per_group + 2 * hp + 1):
                seg = a_cum[:, h:h + 1] - a_cum_t[h:h + 1, :]
                decay = jnp.exp(jnp.where(causal, seg, -jnp.inf))
                ms.append((cb * decay).astype(BF16))
            pair = (g * heads_per_group) // 2 + hp
            yp = jnp.dot(jnp.concatenate(ms, axis=0), xd_b[:, pair * LANES:(pair + 1) * LANES],
                         preferred_element_type=F32)
            y_diag_parts.append(jnp.where(lane < HEAD_DIM, yp[:L], yp[L:]))
    y_diag = jnp.concatenate(y_diag_parts, axis=1)
    y_off = jnp.concatenate(y_off_parts, axis=1)
    s_new = jnp.concatenate(s_new_parts, axis=1)

    y = y_diag + y_off * exp_a_e + dsk_ref[...] * xs
    state_ref[...] = state_ref[...] * exp_a_e[L - 1:L, :] + s_new

    y = y * _silu(z_ref[...].astype(F32))
    outs = []
    for g in range(SSM_GROUPS):
        yg = y[:, g * group_width:(g + 1) * group_width]
        ms = jnp.mean(yg * yg, axis=-1, keepdims=True)
        outs.append(yg * lax.rsqrt(ms + EPS))
    y_ref[...] = (jnp.concatenate(outs, axis=1) * ng_ref[...]).astype(BF16)


def _head_spread_matrix():
    e = np.zeros((2 * LANES, SSM_D_INNER), np.float32)
    for h in range(SSM_HEADS):
        e[h, h * HEAD_DIM:(h + 1) * HEAD_DIM] = 1.0
        e[LANES + h, h * HEAD_DIM:(h + 1) * HEAD_DIM] = 1.0
    return jnp.asarray(e, BF16)


def _pad_lanes(v):
    return jnp.zeros((1, LANES), F32).at[0, :v.shape[0]].set(v)


def _ssd(z, xbc, dt_raw, conv_w, conv_b, dt_bias, a_log, d_skip, norm_g):
    b, s, _ = z.shape
    L = SSM_CHUNK
    row = lambda width: _resident((1, width), lambda bi, c: (0, 0))
    return pl.pallas_call(
        _ssd_kernel,
        grid=(b, s // L),
        in_specs=[
            pl.BlockSpec((None, L, SSM_D_INNER), lambda bi, c: (bi, c, 0)),
            pl.BlockSpec((None, L, SSM_CONV_DIM), lambda bi, c: (bi, c, 0)),
            pl.BlockSpec((None, L, LANES), lambda bi, c: (bi, c, 0)),
            _resident((SSM_CONV, SSM_CONV_DIM), lambda bi, c: (0, 0)),
            row(SSM_CONV_DIM), row(LANES), row(LANES), row(SSM_D_INNER), row(SSM_D_INNER),
            _resident((2 * LANES, SSM_D_INNER), lambda bi, c: (0, 0)),
        ],
        out_specs=pl.BlockSpec((None, L, SSM_D_INNER), lambda bi, c: (bi, c, 0)),
        out_shape=jax.ShapeDtypeStruct((b, s, SSM_D_INNER), BF16),
        scratch_shapes=[pltpu.VMEM((L + 8, SSM_CONV_DIM), F32), pltpu.VMEM((SSM_STATE, SSM_D_INNER), F32)],
        compiler_params=_cparams(("parallel", "arbitrary")),
        name="ssd_mixer",
    )(z, xbc, dt_raw, conv_w.T, conv_b.reshape(1, -1), _pad_lanes(dt_bias), _pad_lanes(a_log),
      jnp.repeat(d_skip, HEAD_DIM).reshape(1, -1), norm_g.reshape(1, -1), _head_spread_matrix())


def _split_lane_halves(x2):
    xf = x2.astype(F32)
    lane = lax.broadcasted_iota(jnp.int32, xf.shape, 1)
    lo = jnp.where(lane < HEAD_DIM, xf, 0.0)
    hi = jnp.where(lane >= HEAD_DIM, xf, 0.0)
    return jnp.concatenate([lo, hi], axis=0).astype(BF16)


def _scores_t(kb, qs):
    return lax.dot_general(kb, qs, (((1,), (1,)), ((), ())), preferred_element_type=F32)


def _pv_t(vb, wt):
    return lax.dot_general(vb, wt, (((0,), (0,)), ((), ())), preferred_element_type=F32)


def _sb_kernel(q_ref, k_ref, v_ref, o_ref):
    T = q_ref.shape[0]
    n_streams = q_ref.shape[1] // LANES
    i = pl.program_id(1)
    qs = [_split_lane_halves(q_ref[:, p * LANES:(p + 1) * LANES]) for p in range(n_streams)]

    key = lax.broadcasted_iota(jnp.int32, (T, 2 * T), 0)
    qry = lax.broadcasted_iota(jnp.int32, (T, 2 * T), 1)
    strict = key < jnp.where(qry >= T, qry - T, qry)
    ur = lax.broadcasted_iota(jnp.int32, (T, T), 0)
    uc = lax.broadcasted_iota(jnp.int32, (T, T), 1)
    suffix = (uc >= ur).astype(F32).astype(BF16)

    def blocks(j, st, diagonal):
        start = pl.multiple_of(j * T, T)
        streams = range(n_streams)
        zs = [_scores_t(k_ref[pl.ds(start, T), p * LANES:(p + 1) * LANES], qs[p]) for p in streams]
        lsms = []
        for z in zs:
            lsm = jnp.minimum(-z, 0.0) - jnp.log(1.0 + jnp.exp2(-jnp.abs(z))) * LOG2E
            if diagonal:
                lsm = jnp.where(strict, lsm, 0.0)
            lsms.append(lsm.astype(BF16))
        css = [jnp.dot(suffix, lsm, preferred_element_type=F32) for lsm in lsms]
        ws = []
        for p in streams:
            w = jnp.exp2(zs[p] + css[p] + st[2 * p])
            if diagonal:
                w = jnp.where(strict, w, 0.0)
            ws.append(w.astype(BF16))
        out = []
        for p in streams:
            vb = v_ref[pl.ds(start, T), p * LANES:(p + 1) * LANES]
            out.extend([st[2 * p] + css[p][0:1, :], st[2 * p + 1] + _pv_t(vb, ws[p])])
        return tuple(out)

    init = []
    for p in range(n_streams):
        init.extend([jnp.zeros((1, 2 * T), F32), jnp.zeros((LANES, 2 * T), F32)])
    state = blocks(i, tuple(init), True)
    state = lax.fori_loop(0, i, lambda t, st: blocks(i - 1 - t, st, False), state)
    for p in range(n_streams):
        acc = state[2 * p + 1]
        pair_t = jnp.concatenate([acc[:HEAD_DIM, :T], acc[HEAD_DIM:, T:]], axis=0)
        o_ref[:, p * LANES:(p + 1) * LANES] = pair_t.T.astype(BF16)


def _stick_breaking(q, k, v):
    b, s, width = q.shape
    T = min(ATT_BLOCK, s)
    return pl.pallas_call(
        _sb_kernel,
        grid=(b, s // T),
        in_specs=[
            pl.BlockSpec((None, T, width), lambda bi, i: (bi, i, 0)),
            pl.BlockSpec((None, s, width), lambda bi, i: (bi, 0, 0)),
            pl.BlockSpec((None, s, width), lambda bi, i: (bi, 0, 0)),
        ],
        out_specs=pl.BlockSpec((None, T, width), lambda bi, i: (bi, i, 0)),
        out_shape=jax.ShapeDtypeStruct((b, s, width), BF16),
        compiler_params=_cparams(("parallel", "arbitrary")),
        name="stick_breaking_attention",
    )(q, k, v)


def _bucket_starts():
    max_exact = N_REL_BUCKETS // 2
    dist = np.arange(0, 4 * REL_MAX_DIST, dtype=np.float64)
    ratio = np.log(np.maximum(dist, max_exact) / max_exact) / math.log(REL_MAX_DIST / max_exact)
    scaled = ratio * (N_REL_BUCKETS - max_exact)
    frac = np.abs(scaled - np.round(scaled))
    interior = (dist > max_exact) & (dist < REL_MAX_DIST)
    assert frac[interior].min() > 1e-3, "a bucket edge sits on an integer distance"
    large = np.minimum(max_exact + np.floor(scaled + 1e-9).astype(np.int64), N_REL_BUCKETS - 1)
    bucket = np.where(dist < max_exact, dist.astype(np.int64), large)
    assert np.all(np.diff(bucket) >= 0) and bucket[-1] == N_REL_BUCKETS - 1
    return [int(np.argmax(bucket >= bkt)) for bkt in range(N_REL_BUCKETS)]


def _bias_table_kernel(rel_ref, o_ref, *, starts):
    h = pl.program_id(0)
    T = o_ref.shape[-1]
    key = lax.broadcasted_iota(jnp.int32, (T, T), 0)
    qry = lax.broadcasted_iota(jnp.int32, (T, T), 1)
    for o_blk in range(o_ref.shape[0]):
        dist = qry - key + o_blk * T
        bias = jnp.full((T, T), rel_ref[0, h], F32)
        for bkt in range(1, N_REL_BUCKETS):
            bias = jnp.where(dist >= starts[bkt], rel_ref[bkt, h], bias)
        o_ref[o_blk] = jnp.where(dist >= 0, bias * LOG2E, -jnp.inf)


def _bias_tables(rel_bias, T):
    starts = _bucket_starts()
    assert T + 1 >= starts[-1]
    return pl.pallas_call(
        functools.partial(_bias_table_kernel, starts=starts),
        grid=(DIFF_HEADS,),
        in_specs=[pl.BlockSpec(memory_space=pltpu.SMEM)],
        out_specs=pl.BlockSpec((None, 3, T, T), lambda h: (h, 0, 0, 0)),
        out_shape=jax.ShapeDtypeStruct((DIFF_HEADS, 3, T, T), F32),
        compiler_params=_cparams(("arbitrary",)),
        name="t5_bias_tables",
    )(rel_bias)


def _diff_kernel(q_ref, k_ref, v_ref, tb_ref, lq1_ref, lk1_ref, lq2_ref, lk2_ref, sg_ref, o_ref,
                 *, lambda_init):
    T = q_ref.shape[0]
    n_streams = q_ref.shape[1] // LANES
    i = pl.program_id(1)
    qs = [_split_lane_halves(q_ref[:, h * LANES:(h + 1) * LANES]) for h in range(n_streams)]

    def body(j, st):
        start = pl.multiple_of(j * T, T)
        off = jnp.minimum(i - j, 2)
        heads = range(n_streams)
        ss = [_scores_t(k_ref[pl.ds(start, T), h * LANES:(h + 1) * LANES], qs[h]) for h in heads]
        stats, ps = [], []
        for h in heads:
            m, l = st[3 * h], st[3 * h + 1]
            bias = tb_ref[h, off]
            s = ss[h] + jnp.concatenate([bias, bias], axis=1)
            m_new = jnp.maximum(m, jnp.max(s, axis=0, keepdims=True))
            alpha = jnp.exp2(m - m_new)
            p = jnp.exp2(s - m_new)
            stats.append((m_new, alpha, alpha * l + jnp.sum(p, axis=0, keepdims=True)))
            ps.append(p.astype(BF16))
        out = []
        for h in heads:
            m_new, alpha, l = stats[h]
            vb = v_ref[pl.ds(start, T), h * LANES:(h + 1) * LANES]
            out.extend([m_new, l, alpha * st[3 * h + 2] + _pv_t(vb, ps[h])])
        return tuple(out)

    init = []
    for h in range(n_streams):
        init.extend([jnp.full((1, 2 * T), -jnp.inf, F32), jnp.zeros((1, 2 * T), F32),
                     jnp.zeros((DIFF_V_DIM, 2 * T), F32)])
    state = lax.fori_loop(0, i + 1, body, tuple(init))

    lam = (jnp.exp(jnp.sum(lq1_ref[...] * lk1_ref[...], axis=-1, keepdims=True))
           - jnp.exp(jnp.sum(lq2_ref[...] * lk2_ref[...], axis=-1, keepdims=True)) + lambda_init)
    for h in range(n_streams):
        l, acc = state[3 * h + 1], state[3 * h + 2]
        o_t = acc[:, :T] / l[:, :T] - lam * (acc[:, T:] / l[:, T:])
        o = o_t.T
        ms = jnp.mean(o * o, axis=-1, keepdims=True)
        o = o * lax.rsqrt(ms + EPS) * sg_ref[...] * (1.0 - lambda_init)
        o_ref[:, h * LANES:(h + 1) * LANES] = o.astype(BF16)


def _diff_attention(q, k, v, tables, lq1, lk1, lq2, lk2, subln_g, lambda_init):
    b, s, width = q.shape
    T = tables.shape[-1]
    vec = lambda n: _resident((1, n), lambda bi, i: (0, 0))
    return pl.pallas_call(
        functools.partial(_diff_kernel, lambda_init=lambda_init),
        grid=(b, s // T),
        in_specs=[
            pl.BlockSpec((None, T, width), lambda bi, i: (bi, i, 0)),
            pl.BlockSpec((None, s, width), lambda bi, i: (bi, 0, 0)),
            pl.BlockSpec((None, s, width), lambda bi, i: (bi, 0, 0)),
            _resident(tables.shape, lambda bi, i: (0, 0, 0, 0)),
            vec(HEAD_DIM), vec(HEAD_DIM), vec(HEAD_DIM), vec(HEAD_DIM), vec(DIFF_V_DIM),
        ],
        out_specs=pl.BlockSpec((None, T, width), lambda bi, i: (bi, i, 0)),
        out_shape=jax.ShapeDtypeStruct((b, s, width), BF16),
        compiler_params=_cparams(("parallel", "arbitrary")),
        name="differential_attention",
    )(q, k, v, tables, lq1.reshape(1, -1), lk1.reshape(1, -1), lq2.reshape(1, -1), lk2.reshape(1, -1),
      subln_g.reshape(1, -1))


def _outproj_kernel(x_ref, mod_ref, ys_ref, yb_ref, yd_ref, w_ref, o_ref):
    o1 = SSM_D_INNER
    o2 = o1 + SB_WIDTH
    y = jnp.dot(ys_ref[...], w_ref[0:o1, :], preferred_element_type=F32)
    y = y + jnp.dot(yb_ref[...], w_ref[o1:o2, :], preferred_element_type=F32)
    y = y + jnp.dot(yd_ref[...], w_ref[o2:, :], preferred_element_type=F32)
    o_ref[...] = x_ref[...] + mod_ref[5:6, :] * y


def _outproj(x, mod, y_ssm, y_sb, y_diff, w_out):
    b, s, d = x.shape
    tm = min(TOKEN_TILE, s)
    tok = lambda width: pl.BlockSpec((None, tm, width), lambda bi, i: (bi, i, 0))
    return pl.pallas_call(
        _outproj_kernel,
        grid=(b, s // tm),
        in_specs=[
            tok(d),
            pl.BlockSpec((None, N_MOD, d), lambda bi, i: (bi, 0, 0)),
            tok(SSM_D_INNER), tok(SB_WIDTH), tok(DIFF_WIDTH),
            _resident(w_out.shape, lambda bi, i: (0, 0)),
        ],
        out_specs=tok(d),
        out_shape=jax.ShapeDtypeStruct((b, s, d), F32),
        compiler_params=_cparams(("parallel", "parallel")),
        name="mixer_outproj",
    )(x, mod, y_ssm, y_sb, y_diff, w_out.astype(BF16))


def kernel(x, c, ada_w, ada_b, ffn1_norm, ffn1_w13, ffn1_w2, mix_norm, w_in, ssm_conv_w, ssm_conv_b, ssm_dt_bias, ssm_a_log, ssm_d, ssm_norm, diff_lambda_q1, diff_lambda_k1, diff_lambda_q2, diff_lambda_k2, diff_subln, rel_bias, w_out, ffn2_norm, ffn2_w13, ffn2_w2, final_norm):
    depth = ada_w.shape[0]
    s = x.shape[1]
    mods = _ada_modulation(c, ada_w, ada_b)
    tables = _bias_tables(rel_bias, min(ATT_BLOCK, s))
    for l in range(depth):
        mod = mods[l]
        x = _ffn(x, mod, ffn1_norm[l], ffn1_w13[l], ffn1_w2[l], mod_row=0)
        lambda_init = 0.8 - 0.6 * math.exp(-0.3 * l)
        z, xbc, dt_raw, sq, sk, sv, dq, dk, dv = _inproj(x, mod, mix_norm[l], w_in[l])
        y_ssm = _ssd(z, xbc, dt_raw, ssm_conv_w[l], ssm_conv_b[l], ssm_dt_bias[l], ssm_a_log[l], ssm_d[l],
                     ssm_norm[l])
        y_sb = _stick_breaking(sq, sk, sv)
        y_diff = _diff_attention(dq, dk, dv, tables, diff_lambda_q1[l], diff_lambda_k1[l], diff_lambda_q2[l],
                                 diff_lambda_k2[l], diff_subln[l], lambda_init)
        x = _outproj(x, mod, y_ssm, y_sb, y_diff, w_out[l])
        x = _ffn(x, mod, ffn2_norm[l], ffn2_w13[l], ffn2_w2[l], mod_row=6,
                 final_g=final_norm if l == depth - 1 else None)
    return x
```

```python
import functools
import math

import numpy as np
import jax
import jax.numpy as jnp
from jax import lax
from jax.experimental import pallas as pl
from jax.experimental.pallas import tpu as pltpu

F32 = jnp.float32
BF16 = jnp.bfloat16
HIGHEST = lax.Precision.HIGHEST
LOG2E = math.log2(math.e)

HEAD_DIM = 64
SSM_HEADS = 16
SSM_GROUPS = 2
SSM_STATE = 128
SSM_CONV = 4
SSM_CHUNK = 128
SSM_D_INNER = SSM_HEADS * HEAD_DIM
SSM_CONV_DIM = SSM_D_INNER + 2 * SSM_GROUPS * SSM_STATE
SB_HEADS = 8
SB_WIDTH = SB_HEADS * HEAD_DIM
DIFF_HEADS = 4
DIFF_V_DIM = 2 * HEAD_DIM
DIFF_WIDTH = DIFF_HEADS * DIFF_V_DIM
N_MOD = 9
N_REL_BUCKETS = 32
REL_MAX_DIST = 128
EPS = 1e-6

LANES = 128
VMEM_LIMIT = 56 * 1024 * 1024

TOKEN_TILE = 512
FF_CHUNK = 256
ATT_BLOCK = 256

OFF_Z = 0
OFF_XBC = OFF_Z + SSM_D_INNER
OFF_SQ = OFF_XBC + SSM_CONV_DIM
OFF_SK = OFF_SQ + SB_WIDTH
OFF_SV = OFF_SK + SB_WIDTH
OFF_DQ = OFF_SV + SB_WIDTH
OFF_DK = OFF_DQ + DIFF_WIDTH
OFF_DV = OFF_DK + DIFF_WIDTH
OFF_DT = OFF_DV + DIFF_WIDTH
IN_PACKED = OFF_DT + LANES


def _cparams(semantics, flags=None):
    return pltpu.CompilerParams(dimension_semantics=semantics, vmem_limit_bytes=VMEM_LIMIT, flags=flags)


ATT_FLAGS = None
SB_STREAMS = 4
DIFF_STREAMS = 4


def _resident(block_shape, index_map):
    return pl.BlockSpec(block_shape, index_map, pipeline_mode=pl.Buffered(1))


def _sigmoid(x):
    return 1.0 / (1.0 + jnp.exp(-x))


def _silu(x):
    return x * _sigmoid(x)


def _softplus(x):
    return jnp.maximum(x, 0.0) + jnp.log1p(jnp.exp(-jnp.abs(x)))


def _norm_modulate(x, g, shift, scale):
    ms = jnp.mean(x * x, axis=-1, keepdims=True)
    y = x * lax.rsqrt(ms + EPS) * g
    return y * (1.0 + scale) + shift


def _ada_kernel(c_ref, w_ref, b_ref, o_ref):
    cond = _silu(c_ref[...])
    o_ref[...] = jnp.dot(cond, w_ref[...], preferred_element_type=F32, precision=HIGHEST) + b_ref[...]


def _ada_modulation(c, ada_w, ada_b):
    depth, d, nmod = ada_w.shape
    b = c.shape[0]
    rows = 8 * pl.cdiv(b, 8)
    c_pad = jnp.zeros((rows, d), F32).at[:b].set(c)
    out = pl.pallas_call(
        _ada_kernel,
        grid=(depth, nmod // d),
        in_specs=[
            pl.BlockSpec((rows, d), lambda l, j: (0, 0)),
            pl.BlockSpec((None, d, d), lambda l, j: (l, 0, j)),
            pl.BlockSpec((None, 1, d), lambda l, j: (l, 0, j)),
        ],
        out_specs=pl.BlockSpec((None, rows, d), lambda l, j: (l, 0, j)),
        out_shape=jax.ShapeDtypeStruct((depth, rows, nmod), F32),
        compiler_params=_cparams(("arbitrary", "arbitrary")),
        name="ada_modulation",
    )(c_pad, ada_w, ada_b.reshape(depth, 1, nmod))
    return out[:, :b].reshape(depth, b, N_MOD, d)


def _ffn_kernel(x_ref, mod_ref, g_ref, w1_ref, w3_ref, w2_ref, *rest, mod_row, final_norm):
    if final_norm:
        fg_ref, o_ref, acc_ref = rest
    else:
        o_ref, acc_ref = rest
    x = x_ref[...]
    shift = mod_ref[mod_row:mod_row + 1, :]
    scale = mod_ref[mod_row + 1:mod_row + 2, :]
    gate = mod_ref[mod_row + 2:mod_row + 3, :]
    h = _norm_modulate(x, g_ref[...], shift, scale).astype(BF16)
    n_chunks = w1_ref.shape[0]
    for j in range(n_chunks):
        a = jnp.dot(h, w1_ref[j], preferred_element_type=F32)
        u = jnp.dot(h, w3_ref[j], preferred_element_type=F32)
        act = (_silu(a) * u).astype(BF16)
        part = jnp.dot(act, w2_ref[j], preferred_element_type=F32)
        if j == 0:
            acc_ref[...] = part
        else:
            acc_ref[...] += part
    y = x + (0.5 * gate) * acc_ref[...]
    if final_norm:
        ms = jnp.mean(y * y, axis=-1, keepdims=True)
        y = y * lax.rsqrt(ms + EPS) * fg_ref[...]
    o_ref[...] = y


def _ffn(x, mod, g, w13, w2, mod_row, final_g=None):
    b, s, d = x.shape
    d_ff = w2.shape[0]
    n_chunks = d_ff // FF_CHUNK
    tm = min(TOKEN_TILE, s)
    w13b = w13.astype(BF16)
    w1c = w13b[:, :d_ff].reshape(d, n_chunks, FF_CHUNK).transpose(1, 0, 2)
    w3c = w13b[:, d_ff:].reshape(d, n_chunks, FF_CHUNK).transpose(1, 0, 2)
    w2c = w2.astype(BF16).reshape(n_chunks, FF_CHUNK, d)
    final_norm = final_g is not None
    in_specs = [
        pl.BlockSpec((None, tm, d), lambda bi, i: (bi, i, 0)),
        pl.BlockSpec((None, N_MOD, d), lambda bi, i: (bi, 0, 0)),
        _resident((1, d), lambda bi, i: (0, 0)),
        _resident((n_chunks, d, FF_CHUNK), lambda bi, i: (0, 0, 0)),
        _resident((n_chunks, d, FF_CHUNK), lambda bi, i: (0, 0, 0)),
        _resident((n_chunks, FF_CHUNK, d), lambda bi, i: (0, 0, 0)),
    ]
    args = [x, mod, g.reshape(1, d), w1c, w3c, w2c]
    if final_norm:
        in_specs.append(_resident((1, d), lambda bi, i: (0, 0)))
        args.append(final_g.reshape(1, d))
    return pl.pallas_call(
        functools.partial(_ffn_kernel, mod_row=mod_row, final_norm=final_norm),
        grid=(b, s // tm),
        in_specs=in_specs,
        out_specs=pl.BlockSpec((None, tm, d), lambda bi, i: (bi, i, 0)),
        out_shape=jax.ShapeDtypeStruct((b, s, d), F32),
        scratch_shapes=[pltpu.VMEM((tm, d), F32)],
        compiler_params=_cparams(("parallel", "parallel")),
        name="ffn_final" if final_norm else "ffn",
    )(*args)


def _inproj_kernel(x_ref, mod_ref, g_ref, w_ref, z_ref, xbc_ref, dt_ref,
                   sq_ref, sk_ref, sv_ref, dq_ref, dk_ref, dv_ref):
    x = x_ref[...]
    h = _norm_modulate(x, g_ref[...], mod_ref[3:4, :], mod_ref[4:5, :]).astype(BF16)

    def proj(off, width):
        return jnp.dot(h, w_ref[:, off:off + width], preferred_element_type=F32)

    qk_scale = HEAD_DIM ** -0.5 * LOG2E
    z_ref[...] = proj(OFF_Z, SSM_D_INNER).astype(BF16)
    xbc_ref[...] = proj(OFF_XBC, SSM_CONV_DIM).astype(BF16)
    dt_ref[...] = proj(OFF_DT, LANES)
    sq_ref[...] = (proj(OFF_SQ, SB_WIDTH) * qk_scale).astype(BF16)
    sk_ref[...] = proj(OFF_SK, SB_WIDTH).astype(BF16)
    sv_ref[...] = proj(OFF_SV, SB_WIDTH).astype(BF16)
    dq_ref[...] = (proj(OFF_DQ, DIFF_WIDTH) * qk_scale).astype(BF16)
    dk_ref[...] = proj(OFF_DK, DIFF_WIDTH).astype(BF16)
    dv_ref[...] = proj(OFF_DV, DIFF_WIDTH).astype(BF16)


def _pack_w_in(w_in):
    d = w_in.shape[0]
    dt0 = SSM_D_INNER + SSM_CONV_DIM
    dt1 = dt0 + SSM_HEADS
    packed = jnp.concatenate(
        [w_in[:, :dt0], w_in[:, dt1:], w_in[:, dt0:dt1], jnp.zeros((d, LANES - SSM_HEADS), w_in.dtype)], axis=1)
    return packed.astype(BF16)


def _inproj(x, mod, g, w_in):
    b, s, d = x.shape
    tm = min(TOKEN_TILE, s)
    w = _pack_w_in(w_in)
    widths = (SSM_D_INNER, SSM_CONV_DIM, LANES) + (SB_WIDTH,) * 3 + (DIFF_WIDTH,) * 3
    dtypes = (BF16, BF16, F32) + (BF16,) * 6
    return pl.pallas_call(
        _inproj_kernel,
        grid=(b, s // tm),
        in_specs=[
            pl.BlockSpec((None, tm, d), lambda bi, i: (bi, i, 0)),
            pl.BlockSpec((None, N_MOD, d), lambda bi, i: (bi, 0, 0)),
            _resident((1, d), lambda bi, i: (0, 0)),
            _resident((d, IN_PACKED), lambda bi, i: (0, 0)),
        ],
        out_specs=[pl.BlockSpec((None, tm, wd), lambda bi, i: (bi, i, 0)) for wd in widths],
        out_shape=[jax.ShapeDtypeStruct((b, s, wd), dt) for wd, dt in zip(widths, dtypes)],
        compiler_params=_cparams(("parallel", "parallel")),
        name="mixer_inproj",
    )(x, mod, g.reshape(1, d), w)


def _ssd_kernel(z_ref, xbc_ref, dtr_ref, cw_ref, cb_ref, dtb_ref, alog_ref, dsk_ref, ng_ref, e_ref,
                y_ref, cbuf_ref, state_ref):
    L = z_ref.shape[0]
    pad = 8
    c = pl.program_id(1)

    @pl.when(c == 0)
    def _():
        cbuf_ref[0:pad, :] = jnp.zeros((pad, SSM_CONV_DIM), F32)
        state_ref[...] = jnp.zeros(state_ref.shape, F32)

    cbuf_ref[pad:pad + L, :] = xbc_ref[...].astype(F32)
    conv = cb_ref[...]
    for k in range(SSM_CONV):
        conv = conv + cw_ref[k:k + 1, :] * cbuf_ref[pl.ds(pad - (SSM_CONV - 1) + k, L), :]
    cbuf_ref[0:pad, :] = cbuf_ref[L:L + pad, :]
    xa = _silu(conv)
    xs = xa[:, :SSM_D_INNER]
    bmat = xa[:, SSM_D_INNER:SSM_D_INNER + SSM_GROUPS * SSM_STATE]
    cmat = xa[:, SSM_D_INNER + SSM_GROUPS * SSM_STATE:]

    dt = _softplus(dtr_ref[...] + dtb_ref[...])
    a = -jnp.exp(alog_ref[...])
    da = dt * a
    row = lax.broadcasted_iota(jnp.int32, (L, L), 0)
    col = lax.broadcasted_iota(jnp.int32, (L, L), 1)
    causal = row >= col
    a_cum = jnp.dot(causal.astype(F32), da, preferred_element_type=F32, precision=HIGHEST)
    a_cum_t = a_cum.T
    exp_a = jnp.exp(a_cum)
    decay_to_end = jnp.exp(a_cum[L - 1:L, :] - a_cum)

    stacked = jnp.concatenate([dt, exp_a, decay_to_end], axis=0)
    hi = stacked.astype(BF16)
    lo = (stacked - hi.astype(F32)).astype(BF16)
    spread = jnp.dot(jnp.concatenate([hi, lo], axis=1), e_ref[...], preferred_element_type=F32)
    dt_e = spread[0:L]
    exp_a_e = spread[L:2 * L]
    dte_e = spread[2 * L:3 * L]

    xd = xs * dt_e
    xd_b = xd.astype(BF16)
    xdd_b = (xd * dte_e).astype(BF16)
    prev_b = state_ref[...].astype(BF16)
    lane = lax.broadcasted_iota(jnp.int32, (L, LANES), 1)
    heads_per_group = SSM_HEADS // SSM_GROUPS
    group_width = heads_per_group * HEAD_DIM

    y_diag_parts = []
    y_off_parts = []
    s_new_parts = []
    for g in range(SSM_GROUPS):
        bg = bmat[:, g * SSM_STATE:(g + 1) * SSM_STATE]
        cg_b = cmat[:, g * SSM_STATE:(g + 1) * SSM_STATE].astype(BF16)
        bg_b = bg.astype(BF16)
        bg_t_b = bg.T.astype(BF16)
        gs = slice(g * group_width, (g + 1) * group_width)
        cb = lax.dot_general(cg_b, bg_b, (((1,), (1,)), ((), ())), preferred_element_type=F32)
        y_off_parts.append(jnp.dot(cg_b, prev_b[:, gs], preferred_element_type=F32))
        s_new_parts.append(jnp.dot(bg_t_b, xdd_b[:, gs], preferred_element_type=F32))
        for hp in range(heads_per_group // 2):
            ms = []
            for h in (g * heads_per_group + 2 * hp, g * heads_per_group + 2 * hp + 1):
                seg = a_cum[:, h:h + 1] - a_cum_t[h:h + 1, :]
                decay = jnp.exp(jnp.where(causal, seg, -jnp.inf))
                ms.append((cb * decay).astype(BF16))
            pair = (g * heads_per_group) // 2 + hp
            yp = jnp.dot(jnp.concatenate(ms, axis=0), xd_b[:, pair * LANES:(pair + 1) * LANES],
                         preferred_element_type=F32)
            y_diag_parts.append(jnp.where(lane < HEAD_DIM, yp[:L], yp[L:]))
    y_diag = jnp.concatenate(y_diag_parts, axis=1)
    y_off = jnp.concatenate(y_off_parts, axis=1)
    s_new = jnp.concatenate(s_new_parts, axis=1)

    y = y_diag + y_off * exp_a_e + dsk_ref[...] * xs
    state_ref[...] = state_ref[...] * exp_a_e[L - 1:L, :] + s_new

    y = y * _silu(z_ref[...].astype(F32))
    outs = []
    for g in range(SSM_GROUPS):
        yg = y[:, g * group_width:(g + 1) * group_width]
        ms = jnp.mean(yg * yg, axis=-1, keepdims=True)
        outs.append(yg * lax.rsqrt(ms + EPS))
    y_ref[...] = (jnp.concatenate(outs, axis=1) * ng_ref[...]).astype(BF16)


def _head_spread_matrix():
    e = np.zeros((2 * LANES, SSM_D_INNER), np.float32)
    for h in range(SSM_HEADS):
        e[h, h * HEAD_DIM:(h + 1) * HEAD_DIM] = 1.0
        e[LANES + h, h * HEAD_DIM:(h + 1) * HEAD_DIM] = 1.0
    return jnp.asarray(e, BF16)


def _pad_lanes(v):
    return jnp.zeros((1, LANES), F32).at[0, :v.shape[0]].set(v)


def _ssd(z, xbc, dt_raw, conv_w, conv_b, dt_bias, a_log, d_skip, norm_g):
    b, s, _ = z.shape
    L = SSM_CHUNK
    row = lambda width: _resident((1, width), lambda bi, c: (0, 0))
    return pl.pallas_call(
        _ssd_kernel,
        grid=(b, s // L),
        in_specs=[
            pl.BlockSpec((None, L, SSM_D_INNER), lambda bi, c: (bi, c, 0)),
            pl.BlockSpec((None, L, SSM_CONV_DIM), lambda bi, c: (bi, c, 0)),
            pl.BlockSpec((None, L, LANES), lambda bi, c: (bi, c, 0)),
            _resident((SSM_CONV, SSM_CONV_DIM), lambda bi, c: (0, 0)),
            row(SSM_CONV_DIM), row(LANES), row(LANES), row(SSM_D_INNER), row(SSM_D_INNER),
            _resident((2 * LANES, SSM_D_INNER), lambda bi, c: (0, 0)),
        ],
        out_specs=pl.BlockSpec((None, L, SSM_D_INNER), lambda bi, c: (bi, c, 0)),
        out_shape=jax.ShapeDtypeStruct((b, s, SSM_D_INNER), BF16),
        scratch_shapes=[pltpu.VMEM((L + 8, SSM_CONV_DIM), F32), pltpu.VMEM((SSM_STATE, SSM_D_INNER), F32)],
        compiler_params=_cparams(("parallel", "arbitrary")),
        name="ssd_mixer",
    )(z, xbc, dt_raw, conv_w.T, conv_b.reshape(1, -1), _pad_lanes(dt_bias), _pad_lanes(a_log),
      jnp.repeat(d_skip, HEAD_DIM).reshape(1, -1), norm_g.reshape(1, -1), _head_spread_matrix())


def _split_lane_halves(x2):
    xf = x2.astype(F32)
    lane = lax.broadcasted_iota(jnp.int32, xf.shape, 1)
    lo = jnp.where(lane < HEAD_DIM, xf, 0.0)
    hi = jnp.where(lane >= HEAD_DIM, xf, 0.0)
    return jnp.concatenate([lo, hi], axis=0).astype(BF16)


def _scores_t(kb, qs):
    return lax.dot_general(kb, qs, (((1,), (1,)), ((), ())), preferred_element_type=F32)


def _pv_t(vb, wt):
    return lax.dot_general(vb, wt, (((0,), (0,)), ((), ())), preferred_element_type=F32)


def _sb_kernel(q_ref, k_ref, v_ref, o_ref, z_ref, cs_ref, tot_ref, acc_ref, carry_ref):
    T = q_ref.shape[0]
    n_streams = q_ref.shape[1] // LANES
    i = pl.program_id(2)
    qs = [_split_lane_halves(q_ref[:, p * LANES:(p + 1) * LANES]) for p in range(n_streams)]

    key = lax.broadcasted_iota(jnp.int32, (T, 2 * T), 0)
    qry = lax.broadcasted_iota(jnp.int32, (T, 2 * T), 1)
    strict = key < jnp.where(qry >= T, qry - T, qry)
    ur = lax.broadcasted_iota(jnp.int32, (T, T), 0)
    uc = lax.broadcasted_iota(jnp.int32, (T, T), 1)
    neg_suffix = jnp.where(uc >= ur, -1.0, 0.0).astype(BF16)
    streams = range(n_streams)
    sign_bit = jnp.uint32(0x80000000)

    def scores(j_lo, nb):
        rows = pl.ds(pl.multiple_of(j_lo * T, T), nb * T)
        return [_scores_t(k_ref[rows, p * LANES:(p + 1) * LANES], qs[p]) for p in streams]

    def softplus2(z, diagonal):
        neg_abs = lax.bitcast_convert_type(lax.bitcast_convert_type(z, jnp.uint32) | sign_bit, F32)
        sp = jnp.maximum(z, 0.0) + jnp.log(1.0 + jnp.exp2(neg_abs)) * LOG2E
        if diagonal:
            sp = jnp.where(strict, sp, 0.0)
        return sp.astype(BF16)

    def suffix_sums(sp, nb):
        parts = [jnp.dot(neg_suffix, sp[u * T:(u + 1) * T], preferred_element_type=F32) for u in range(nb)]
        total = parts[nb - 1][0:1, :]
        for u in reversed(range(nb - 1)):
            parts[u] = parts[u] + total
            total = parts[u][0:1, :]
        return (parts[0] if nb == 1 else jnp.concatenate(parts, axis=0)), total

    def weights(z, cs, carry, diagonal):
        w = jnp.exp2(z + (cs + carry))
        if diagonal:
            w = jnp.where(strict, w, 0.0)
        return w.astype(BF16)

    def accumulate(j_lo, nb, ws, tots):
        rows = pl.ds(pl.multiple_of(j_lo * T, T), nb * T)
        for p in streams:
            acc_ref[p] += _pv_t(v_ref[rows, p * LANES:(p + 1) * LANES], ws[p])
            carry_ref[p] += tots[p]

    def direct_step(j_lo, nb, diagonal=False):
        zs = scores(j_lo, nb)
        cts = [suffix_sums(softplus2(z, diagonal), nb) for z in zs]
        ws = [weights(zs[p], cts[p][0], carry_ref[p], diagonal) for p in streams]
        accumulate(j_lo, nb, ws, [ct[1] for ct in cts])

    def finish_produce(slot, zs):
        for p in streams:
            z_ref[slot, p] = zs[p]
        for p in streams:
            cs, tot = suffix_sums(softplus2(zs[p], False), 2)
            cs_ref[slot, p] = cs
            tot_ref[slot, p] = tot

    for p in streams:
        acc_ref[p] = jnp.zeros((LANES, 2 * T), F32)
        carry_ref[p] = jnp.zeros((1, 2 * T), F32)
    direct_step(i, 1, diagonal=True)
    n_double = i // 2
    pair_lo = lambda t: i - 2 - 2 * t

    def consume(t, slot, produce_next):
        ws = [weights(z_ref[slot, p], cs_ref[slot, p], carry_ref[p], False) for p in streams]
        tots = [tot_ref[slot, p] for p in streams]
        if produce_next:
            zs_next = scores(pair_lo(t + 1), 2)
        accumulate(pair_lo(t), 2, ws, tots)
        if produce_next:
            finish_produce(1 - slot, zs_next)

    @pl.when(n_double > 0)
    def _():
        finish_produce(0, scores(pair_lo(0), 2))

    n_twice = jnp.maximum(n_double - 1, 0) // 2

    def body(u, _):
        consume(2 * u, 0, True)
        consume(2 * u + 1, 1, True)
        return 0

    lax.fori_loop(0, n_twice, body, 0)
    left = n_double - 2 * n_twice

    @pl.when(left == 2)
    def _():
        consume(2 * n_twice, 0, True)
        consume(2 * n_twice + 1, 1, False)

    @pl.when(left == 1)
    def _():
        consume(2 * n_twice, 0, False)

    @pl.when(i - 2 * n_double > 0)
    def _():
        direct_step(0, 1)

    for p in range(n_streams):
        acc = acc_ref[p]
        pair_t = jnp.concatenate([acc[:HEAD_DIM, :T], acc[HEAD_DIM:, T:]], axis=0)
        o_ref[:, p * LANES:(p + 1) * LANES] = pair_t.T.astype(BF16)


def _stick_breaking(q, k, v):
    b, s, width = q.shape
    T = min(ATT_BLOCK, s)
    gw = SB_STREAMS * LANES
    return pl.pallas_call(
        _sb_kernel,
        grid=(b, width // gw, s // T),
        in_specs=[
            pl.BlockSpec((None, T, gw), lambda bi, g, i: (bi, i, g)),
            pl.BlockSpec((None, s, gw), lambda bi, g, i: (bi, 0, g)),
            pl.BlockSpec((None, s, gw), lambda bi, g, i: (bi, 0, g)),
        ],
        out_specs=pl.BlockSpec((None, T, gw), lambda bi, g, i: (bi, i, g)),
        out_shape=jax.ShapeDtypeStruct((b, s, width), BF16),
        scratch_shapes=[
            pltpu.VMEM((2, SB_STREAMS, 2 * T, 2 * T), F32),
            pltpu.VMEM((2, SB_STREAMS, 2 * T, 2 * T), F32),
            pltpu.VMEM((2, SB_STREAMS, 1, 2 * T), F32),
            pltpu.VMEM((SB_STREAMS, LANES, 2 * T), F32),
            pltpu.VMEM((SB_STREAMS, 1, 2 * T), F32),
        ],
        compiler_params=_cparams(("parallel", "parallel", "arbitrary"), ATT_FLAGS),
        name="stick_breaking_attention",
    )(q, k, v)


def _bucket_starts():
    max_exact = N_REL_BUCKETS // 2
    dist = np.arange(0, 4 * REL_MAX_DIST, dtype=np.float64)
    ratio = np.log(np.maximum(dist, max_exact) / max_exact) / math.log(REL_MAX_DIST / max_exact)
    scaled = ratio * (N_REL_BUCKETS - max_exact)
    frac = np.abs(scaled - np.round(scaled))
    interior = (dist > max_exact) & (dist < REL_MAX_DIST)
    assert frac[interior].min() > 1e-3, "a bucket edge sits on an integer distance"
    large = np.minimum(max_exact + np.floor(scaled + 1e-9).astype(np.int64), N_REL_BUCKETS - 1)
    bucket = np.where(dist < max_exact, dist.astype(np.int64), large)
    assert np.all(np.diff(bucket) >= 0) and bucket[-1] == N_REL_BUCKETS - 1
    return [int(np.argmax(bucket >= bkt)) for bkt in range(N_REL_BUCKETS)]


def _bias_table_kernel(rel_ref, o_ref, *, starts):
    h = pl.program_id(0)
    T = o_ref.shape[-1]
    key = lax.broadcasted_iota(jnp.int32, (T, T), 0)
    qry = lax.broadcasted_iota(jnp.int32, (T, T), 1)
    far = rel_ref[N_REL_BUCKETS - 1, h]
    for o_blk in range(o_ref.shape[0]):
        dist = qry - key + o_blk * T
        bias = jnp.full((T, T), rel_ref[0, h], F32)
        for bkt in range(1, N_REL_BUCKETS):
            bias = jnp.where(dist >= starts[bkt], rel_ref[bkt, h], bias)
        o_ref[o_blk] = jnp.where(dist >= 0, (bias - far) * LOG2E, -jnp.inf)


def _bias_tables(rel_bias, T):
    starts = _bucket_starts()
    assert T + 1 >= starts[-1]
    return pl.pallas_call(
        functools.partial(_bias_table_kernel, starts=starts),
        grid=(DIFF_HEADS,),
        in_specs=[pl.BlockSpec(memory_space=pltpu.SMEM)],
        out_specs=pl.BlockSpec((None, 2, T, T), lambda h: (h, 0, 0, 0)),
        out_shape=jax.ShapeDtypeStruct((DIFF_HEADS, 2, T, T), F32),
        compiler_params=_cparams(("arbitrary",)),
        name="t5_bias_tables",
    )(rel_bias)


def _diff_kernel(q_ref, k_ref, v_ref, tb_ref, lq1_ref, lk1_ref, lq2_ref, lk2_ref, sg_ref, o_ref,
                 s_ref, m_ref, l_ref, acc_ref, *, lambda_init):
    T = q_ref.shape[0]
    n_streams = q_ref.shape[1] // LANES
    i = pl.program_id(2)
    qs = [_split_lane_halves(q_ref[:, h * LANES:(h + 1) * LANES]) for h in range(n_streams)]

    heads = range(n_streams)

    def scores(j_lo, nb):
        rows = pl.ds(pl.multiple_of(j_lo * T, T), nb * T)
        return [_scores_t(k_ref[rows, h * LANES:(h + 1) * LANES], qs[h]) for h in heads]

    def softmax_part(ss, table_rows):
        out = []
        for h in heads:
            read = ss[h] if callable(ss[h]) else (lambda v=ss[h]: v)
            if table_rows is not None:
                biases = [tb_ref[h, o] for o in table_rows]
                bias = biases[0] if len(biases) == 1 else jnp.concatenate(biases, axis=0)
                s = read() + jnp.concatenate([bias, bias], axis=1)
                read = lambda v=s: v
            m_old = m_ref[h]
            m_new = jnp.maximum(m_old, jnp.max(read(), axis=0, keepdims=True))
            alpha = jnp.exp2(m_old - m_new)
            p = jnp.exp2(read() - m_new)
            m_ref[h] = m_new
            l_ref[h] = alpha * l_ref[h] + jnp.sum(p, axis=0, keepdims=True)
            out.append((alpha, p.astype(BF16)))
        return out

    def accumulate(j_lo, nb, aps):
        rows = pl.ds(pl.multiple_of(j_lo * T, T), nb * T)
        for h in heads:
            alpha, p = aps[h]
            acc_ref[h] = alpha * acc_ref[h] + _pv_t(v_ref[rows, h * LANES:(h + 1) * LANES], p)

    for h in heads:
        m_ref[h] = jnp.full((1, 2 * T), -jnp.inf, F32)
        l_ref[h] = jnp.zeros((1, 2 * T), F32)
        acc_ref[h] = jnp.zeros((DIFF_V_DIM, 2 * T), F32)
    n_far = jnp.maximum(i - 1, 0)
    n_double = n_far // 2

    def produce(t, slot):
        for h, s in enumerate(scores(2 * t, 2)):
            s_ref[slot, h] = s

    def consume(t, slot, produce_next):
        aps = softmax_part([lambda h=h: s_ref[slot, h] for h in heads], None)
        if produce_next:
            produce(t + 1, 1 - slot)
        accumulate(2 * t, 2, aps)

    @pl.when(n_double > 0)
    def _():
        produce(0, 0)

    n_twice = jnp.maximum(n_double - 1, 0) // 2

    def body(u, _):
        consume(2 * u, 0, True)
        consume(2 * u + 1, 1, True)
        return 0

    lax.fori_loop(0, n_twice, body, 0)
    left = n_double - 2 * n_twice

    @pl.when(left == 2)
    def _():
        consume(2 * n_twice, 0, True)
        consume(2 * n_twice + 1, 1, False)

    @pl.when(left == 1)
    def _():
        consume(2 * n_twice, 0, False)

    @pl.when(n_far - 2 * n_double > 0)
    def _():
        accumulate(n_far - 1, 1, softmax_part(scores(n_far - 1, 1), None))

    @pl.when(i > 0)
    def _():
        accumulate(i - 1, 2, softmax_part(scores(i - 1, 2), [1, 0]))

    @pl.when(i == 0)
    def _():
        accumulate(0, 1, softmax_part(scores(0, 1), [0]))

    lam = (jnp.exp(jnp.sum(lq1_ref[...] * lk1_ref[...], axis=-1, keepdims=True))
           - jnp.exp(jnp.sum(lq2_ref[...] * lk2_ref[...], axis=-1, keepdims=True)) + lambda_init)
    for h in range(n_streams):
        l, acc = l_ref[h], acc_ref[h]
        o_t = acc[:, :T] / l[:, :T] - lam * (acc[:, T:] / l[:, T:])
        o = o_t.T
        ms = jnp.mean(o * o, axis=-1, keepdims=True)
        o = o * lax.rsqrt(ms + EPS) * sg_ref[...] * (1.0 - lambda_init)
        o_ref[:, h * LANES:(h + 1) * LANES] = o.astype(BF16)


def _diff_attention(q, k, v, tables, lq1, lk1, lq2, lk2, subln_g, lambda_init):
    b, s, width = q.shape
    T = tables.shape[-1]
    vec = lambda n: _resident((1, n), lambda bi, g, i: (0, 0))
    gw = DIFF_STREAMS * LANES
    return pl.pallas_call(
        functools.partial(_diff_kernel, lambda_init=lambda_init),
        grid=(b, width // gw, s // T),
        in_specs=[
            pl.BlockSpec((None, T, gw), lambda bi, g, i: (bi, i, g)),
            pl.BlockSpec((None, s, gw), lambda bi, g, i: (bi, 0, g)),
            pl.BlockSpec((None, s, gw), lambda bi, g, i: (bi, 0, g)),
            pl.BlockSpec((DIFF_STREAMS,) + tables.shape[1:], lambda bi, g, i: (g, 0, 0, 0)),
            vec(HEAD_DIM), vec(HEAD_DIM), vec(HEAD_DIM), vec(HEAD_DIM), vec(DIFF_V_DIM),
        ],
        out_specs=pl.BlockSpec((None, T, gw), lambda bi, g, i: (bi, i, g)),
        out_shape=jax.ShapeDtypeStruct((b, s, width), BF16),
        scratch_shapes=[
            pltpu.VMEM((2, DIFF_STREAMS, 2 * T, 2 * T), F32),
            pltpu.VMEM((DIFF_STREAMS, 1, 2 * T), F32),
            pltpu.VMEM((DIFF_STREAMS, 1, 2 * T), F32),
            pltpu.VMEM((DIFF_STREAMS, DIFF_V_DIM, 2 * T), F32),
        ],
        compiler_params=_cparams(("parallel", "parallel", "arbitrary"), ATT_FLAGS),
        name="differential_attention",
    )(q, k, v, tables, lq1.reshape(1, -1), lk1.reshape(1, -1), lq2.reshape(1, -1), lk2.reshape(1, -1),
      subln_g.reshape(1, -1))


def _outproj_kernel(x_ref, mod_ref, ys_ref, yb_ref, yd_ref, w_ref, o_ref):
    o1 = SSM_D_INNER
    o2 = o1 + SB_WIDTH
    y = jnp.dot(ys_ref[...], w_ref[0:o1, :], preferred_element_type=F32)
    y = y + jnp.dot(yb_ref[...], w_ref[o1:o2, :], preferred_element_type=F32)
    y = y + jnp.dot(yd_ref[...], w_ref[o2:, :], preferred_element_type=F32)
    o_ref[...] = x_ref[...] + mod_ref[5:6, :] * y


def _outproj(x, mod, y_ssm, y_sb, y_diff, w_out):
    b, s, d = x.shape
    tm = min(TOKEN_TILE, s)
    tok = lambda width: pl.BlockSpec((None, tm, width), lambda bi, i: (bi, i, 0))
    return pl.pallas_call(
        _outproj_kernel,
        grid=(b, s // tm),
        in_specs=[
            tok(d),
            pl.BlockSpec((None, N_MOD, d), lambda bi, i: (bi, 0, 0)),
            tok(SSM_D_INNER), tok(SB_WIDTH), tok(DIFF_WIDTH),
            _resident(w_out.shape, lambda bi, i: (0, 0)),
        ],
        out_specs=tok(d),
        out_shape=jax.ShapeDtypeStruct((b, s, d), F32),
        compiler_params=_cparams(("parallel", "parallel")),
        name="mixer_outproj",
    )(x, mod, y_ssm, y_sb, y_diff, w_out.astype(BF16))


def kernel(x, c, ada_w, ada_b, ffn1_norm, ffn1_w13, ffn1_w2, mix_norm, w_in, ssm_conv_w, ssm_conv_b, ssm_dt_bias, ssm_a_log, ssm_d, ssm_norm, diff_lambda_q1, diff_lambda_k1, diff_lambda_q2, diff_lambda_k2, diff_subln, rel_bias, w_out, ffn2_norm, ffn2_w13, ffn2_w2, final_norm):
    depth = ada_w.shape[0]
    s = x.shape[1]
    mods = _ada_modulation(c, ada_w, ada_b)
    tables = _bias_tables(rel_bias, min(ATT_BLOCK, s))
    for l in range(depth):
        mod = mods[l]
        x = _ffn(x, mod, ffn1_norm[l], ffn1_w13[l], ffn1_w2[l], mod_row=0)
        lambda_init = 0.8 - 0.6 * math.exp(-0.3 * l)
        z, xbc, dt_raw, sq, sk, sv, dq, dk, dv = _inproj(x, mod, mix_norm[l], w_in[l])
        y_ssm = _ssd(z, xbc, dt_raw, ssm_conv_w[l], ssm_conv_b[l], ssm_dt_bias[l], ssm_a_log[l], ssm_d[l],
                     ssm_norm[l])
        y_sb = _stick_breaking(sq, sk, sv)
        y_diff = _diff_attention(dq, dk, dv, tables, diff_lambda_q1[l], diff_lambda_k1[l], diff_lambda_q2[l],
                                 diff_lambda_k2[l], diff_subln[l], lambda_init)
        x = _outproj(x, mod, y_ssm, y_sb, y_diff, w_out[l])
        x = _ffn(x, mod, ffn2_norm[l], ffn2_w13[l], ffn2_w2[l], mod_row=6,
                 final_g=final_norm if l == depth - 1 else None)
    return x
```

```python
import functools
import math

import numpy as np
import jax
import jax.numpy as jnp
from jax import lax
from jax.experimental import pallas as pl
from jax.experimental.pallas import tpu as pltpu

F32 = jnp.float32
BF16 = jnp.bfloat16
HIGHEST = lax.Precision.HIGHEST
LOG2E = math.log2(math.e)

HEAD_DIM = 64
SSM_HEADS = 16
SSM_GROUPS = 2
SSM_STATE = 128
SSM_CONV = 4
SSM_CHUNK = 128
SSM_D_INNER = SSM_HEADS * HEAD_DIM
SSM_CONV_DIM = SSM_D_INNER + 2 * SSM_GROUPS * SSM_STATE
SB_HEADS = 8
SB_WIDTH = SB_HEADS * HEAD_DIM
DIFF_HEADS = 4
DIFF_V_DIM = 2 * HEAD_DIM
DIFF_WIDTH = DIFF_HEADS * DIFF_V_DIM
N_MOD = 9
N_REL_BUCKETS = 32
REL_MAX_DIST = 128
EPS = 1e-6

LANES = 128
BF16_SUBLANES = 16
VMEM_LIMIT = 56 * 1024 * 1024

TOKEN_TILE = 512
FF_CHUNK = 256
ATT_BLOCK = 256

OFF_Z = 0
OFF_XBC = OFF_Z + SSM_D_INNER
OFF_SQ = OFF_XBC + SSM_CONV_DIM
OFF_SK = OFF_SQ + SB_WIDTH
OFF_SV = OFF_SK + SB_WIDTH
OFF_DQ = OFF_SV + SB_WIDTH
OFF_DK = OFF_DQ + DIFF_WIDTH
OFF_DV = OFF_DK + DIFF_WIDTH
OFF_DT = OFF_DV + DIFF_WIDTH
IN_PACKED = OFF_DT + LANES


def _cparams(semantics):
    return pltpu.CompilerParams(dimension_semantics=semantics, vmem_limit_bytes=VMEM_LIMIT)


SB_STREAMS = 4
DIFF_STREAMS = 4


def _resident(block_shape, index_map):
    return pl.BlockSpec(block_shape, index_map, pipeline_mode=pl.Buffered(1))


def _silu(x):
    hx = 0.5 * x
    return hx + hx * jnp.tanh(hx)


def _softplus(x):
    return jnp.maximum(x, 0.0) + jnp.log1p(jnp.exp(-jnp.abs(x)))


def _norm_modulate(x, g, shift, scale):
    ms = jnp.mean(x * x, axis=-1, keepdims=True)
    y = x * lax.rsqrt(ms + EPS) * g
    return y * (1.0 + scale) + shift


def _ada_kernel(c_ref, w_ref, b_ref, o_ref):
    cond = _silu(c_ref[...])
    o_ref[...] = jnp.dot(cond, w_ref[...], preferred_element_type=F32, precision=HIGHEST) + b_ref[...]


def _ada_modulation(c, ada_w, ada_b):
    depth, d, nmod = ada_w.shape
    b = c.shape[0]
    rows = 8 * pl.cdiv(b, 8)
    c_pad = jnp.zeros((rows, d), F32).at[:b].set(c)
    out = pl.pallas_call(
        _ada_kernel,
        grid=(depth, nmod // d),
        in_specs=[
            pl.BlockSpec((rows, d), lambda l, j: (0, 0)),
            pl.BlockSpec((None, d, d), lambda l, j: (l, 0, j)),
            pl.BlockSpec((None, 1, d), lambda l, j: (l, 0, j)),
        ],
        out_specs=pl.BlockSpec((None, rows, d), lambda l, j: (l, 0, j)),
        out_shape=jax.ShapeDtypeStruct((depth, rows, nmod), F32),
        compiler_params=_cparams(("arbitrary", "arbitrary")),
        name="ada_modulation",
    )(c_pad, ada_w, ada_b.reshape(depth, 1, nmod))
    return out[:, :b].reshape(depth, b, N_MOD, d)


def _ffn_kernel(x_ref, mod_ref, g_ref, w13_ref, w2_ref, *rest, mod_row, final_norm):
    if final_norm:
        fg_ref, o_ref, acc_ref = rest
    else:
        o_ref, acc_ref = rest
    x = x_ref[...]
    shift = mod_ref[mod_row:mod_row + 1, :]
    scale = mod_ref[mod_row + 1:mod_row + 2, :]
    gate = mod_ref[mod_row + 2:mod_row + 3, :]
    h = _norm_modulate(x, g_ref[...], shift, scale).astype(BF16)
    d_ff = w2_ref.shape[0]
    for j in range(d_ff // FF_CHUNK):
        cols = slice(j * FF_CHUNK, (j + 1) * FF_CHUNK)
        a = jnp.dot(h, w13_ref[:, cols], preferred_element_type=F32)
        u = jnp.dot(h, w13_ref[:, d_ff + j * FF_CHUNK:d_ff + (j + 1) * FF_CHUNK], preferred_element_type=F32)
        act = (_silu(a) * u).astype(BF16)
        part = jnp.dot(act, w2_ref[cols, :], preferred_element_type=F32)
        if j == 0:
            acc_ref[...] = part
        else:
            acc_ref[...] += part
    y = x + (0.5 * gate) * acc_ref[...]
    if final_norm:
        ms = jnp.mean(y * y, axis=-1, keepdims=True)
        y = y * lax.rsqrt(ms + EPS) * fg_ref[...]
    o_ref[...] = y


def _ffn(x, mod, g, w13, w2, mod_row, final_g=None):
    b, s, d = x.shape
    d_ff = w2.shape[0]
    assert d_ff % FF_CHUNK == 0 and d_ff % LANES == 0
    tm = min(TOKEN_TILE, s)
    final_norm = final_g is not None
    in_specs = [
        pl.BlockSpec((None, tm, d), lambda bi, i: (bi, i, 0)),
        pl.BlockSpec((None, N_MOD, d), lambda bi, i: (bi, 0, 0)),
        _resident((1, d), lambda bi, i: (0, 0)),
        _resident((d, 2 * d_ff), lambda bi, i: (0, 0)),
        _resident((d_ff, d), lambda bi, i: (0, 0)),
    ]
    args = [x, mod, g.reshape(1, d), w13.astype(BF16), w2.astype(BF16)]
    if final_norm:
        in_specs.append(_resident((1, d), lambda bi, i: (0, 0)))
        args.append(final_g.reshape(1, d))
    return pl.pallas_call(
        functools.partial(_ffn_kernel, mod_row=mod_row, final_norm=final_norm),
        grid=(b, s // tm),
        in_specs=in_specs,
        out_specs=pl.BlockSpec((None, tm, d), lambda bi, i: (bi, i, 0)),
        out_shape=jax.ShapeDtypeStruct((b, s, d), F32),
        scratch_shapes=[pltpu.VMEM((tm, d), F32)],
        compiler_params=_cparams(("parallel", "parallel")),
        name="ffn_final" if final_norm else "ffn",
    )(*args)


def _inproj_kernel(x_ref, mod_ref, g_ref, w_ref, z_ref, xbc_ref, dt_ref,
                   sq_ref, sk_ref, sv_ref, dq_ref, dk_ref, dv_ref):
    x = x_ref[...]
    h = _norm_modulate(x, g_ref[...], mod_ref[3:4, :], mod_ref[4:5, :]).astype(BF16)

    def proj(off, width):
        return jnp.dot(h, w_ref[:, off:off + width], preferred_element_type=F32)

    qk_scale = HEAD_DIM ** -0.5 * LOG2E
    z_ref[...] = proj(OFF_Z, SSM_D_INNER).astype(BF16)
    xbc_ref[...] = proj(OFF_XBC, SSM_CONV_DIM).astype(BF16)
    dt_ref[...] = proj(OFF_DT, LANES)
    sq_ref[...] = (proj(OFF_SQ, SB_WIDTH) * qk_scale).astype(BF16)
    sk_ref[...] = proj(OFF_SK, SB_WIDTH).astype(BF16)
    sv_ref[...] = proj(OFF_SV, SB_WIDTH).astype(BF16)
    dq_ref[...] = (proj(OFF_DQ, DIFF_WIDTH) * qk_scale).astype(BF16)
    dk_ref[...] = proj(OFF_DK, DIFF_WIDTH).astype(BF16)
    dv_ref[...] = proj(OFF_DV, DIFF_WIDTH).astype(BF16)


def _pack_w_in(w_in):
    d = w_in.shape[0]
    dt0 = SSM_D_INNER + SSM_CONV_DIM
    dt1 = dt0 + SSM_HEADS
    packed = jnp.concatenate(
        [w_in[:, :dt0], w_in[:, dt1:], w_in[:, dt0:dt1], jnp.zeros((d, LANES - SSM_HEADS), w_in.dtype)], axis=1)
    return packed.astype(BF16)


def _inproj(x, mod, g, w_in):
    b, s, d = x.shape
    tm = min(TOKEN_TILE, s)
    w = _pack_w_in(w_in)
    widths = (SSM_D_INNER, SSM_CONV_DIM, LANES) + (SB_WIDTH,) * 3 + (DIFF_WIDTH,) * 3
    dtypes = (BF16, BF16, F32) + (BF16,) * 6
    return pl.pallas_call(
        _inproj_kernel,
        grid=(b, s // tm),
        in_specs=[
            pl.BlockSpec((None, tm, d), lambda bi, i: (bi, i, 0)),
            pl.BlockSpec((None, N_MOD, d), lambda bi, i: (bi, 0, 0)),
            _resident((1, d), lambda bi, i: (0, 0)),
            _resident((d, IN_PACKED), lambda bi, i: (0, 0)),
        ],
        out_specs=[pl.BlockSpec((None, tm, wd), lambda bi, i: (bi, i, 0)) for wd in widths],
        out_shape=[jax.ShapeDtypeStruct((b, s, wd), dt) for wd, dt in zip(widths, dtypes)],
        compiler_params=_cparams(("parallel", "parallel")),
        name="mixer_inproj",
    )(x, mod, g.reshape(1, d), w)


def _ssd_kernel(z_ref, xbc_ref, dtr_ref, cw_ref, cb_ref, dtb_ref, alog_ref, dsk_ref, ng_ref, e_ref,
                y_ref, tail_ref, state_ref):
    L = z_ref.shape[0]
    pad = tail_ref.shape[0]
    c = pl.program_id(1)

    @pl.when(c == 0)
    def _():
        tail_ref[...] = jnp.zeros(tail_ref.shape, BF16)
        state_ref[...] = jnp.zeros(state_ref.shape, F32)

    u = xbc_ref[...]
    u_ext = jnp.concatenate([tail_ref[...], u], axis=0)
    tail_ref[...] = u[L - pad:, :]
    trow = lax.broadcasted_iota(jnp.int32, (L, pad + L), 0)
    tcol = lax.broadcasted_iota(jnp.int32, (L, pad + L), 1)
    shifts = jnp.concatenate(
        [jnp.where(tcol == trow + (pad - (SSM_CONV - 1) + k), 1.0, 0.0) for k in range(SSM_CONV)], axis=0)
    shifted = jnp.dot(shifts.astype(BF16), u_ext, preferred_element_type=F32)
    conv = cb_ref[...]
    for k in range(SSM_CONV):
        conv = conv + cw_ref[k:k + 1, :] * shifted[k * L:(k + 1) * L]
    xa = _silu(conv)
    xs = xa[:, :SSM_D_INNER]
    bmat = xa[:, SSM_D_INNER:SSM_D_INNER + SSM_GROUPS * SSM_STATE]
    cmat = xa[:, SSM_D_INNER + SSM_GROUPS * SSM_STATE:]

    dt = _softplus(dtr_ref[...] + dtb_ref[...])
    a = -jnp.exp(alog_ref[...])
    da = dt * a
    row = lax.broadcasted_iota(jnp.int32, (L, L), 0)
    col = lax.broadcasted_iota(jnp.int32, (L, L), 1)
    causal = row >= col
    a_cum = jnp.dot(causal.astype(F32), da, preferred_element_type=F32, precision=HIGHEST)
    a_cum_t = a_cum.T
    exp_a = jnp.exp(a_cum)
    decay_to_end = jnp.exp(a_cum[L - 1:L, :] - a_cum)

    stacked = jnp.concatenate([dt, exp_a, decay_to_end], axis=0)
    hi = stacked.astype(BF16)
    lo = (stacked - hi.astype(F32)).astype(BF16)
    spread = jnp.dot(jnp.concatenate([hi, lo], axis=1), e_ref[...], preferred_element_type=F32)
    dt_e = spread[0:L]
    exp_a_e = spread[L:2 * L]
    dte_e = spread[2 * L:3 * L]

    xd = xs * dt_e
    xd_b = xd.astype(BF16)
    xdd_b = (xd * dte_e).astype(BF16)
    prev_b = state_ref[...].astype(BF16)
    lane = lax.broadcasted_iota(jnp.int32, (L, LANES), 1)
    heads_per_group = SSM_HEADS // SSM_GROUPS
    group_width = heads_per_group * HEAD_DIM

    y_diag_parts = []
    y_off_parts = []
    s_new_parts = []
    for g in range(SSM_GROUPS):
        bg = bmat[:, g * SSM_STATE:(g + 1) * SSM_STATE]
        cg_b = cmat[:, g * SSM_STATE:(g + 1) * SSM_STATE].astype(BF16)
        bg_b = bg.astype(BF16)
        bg_t_b = bg.T.astype(BF16)
        gs = slice(g * group_width, (g + 1) * group_width)
        cb = lax.dot_general(cg_b, bg_b, (((1,), (1,)), ((), ())), preferred_element_type=F32)
        y_off_parts.append(jnp.dot(cg_b, prev_b[:, gs], preferred_element_type=F32))
        s_new_parts.append(jnp.dot(bg_t_b, xdd_b[:, gs], preferred_element_type=F32))
        for hp in range(heads_per_group // 2):
            ms = []
            for h in (g * heads_per_group + 2 * hp, g * heads_per_group + 2 * hp + 1):
                seg = a_cum[:, h:h + 1] - a_cum_t[h:h + 1, :]
                decay = jnp.exp(jnp.where(causal, seg, -jnp.inf))
                ms.append((cb * decay).astype(BF16))
            pair = (g * heads_per_group) // 2 + hp
            yp = jnp.dot(jnp.concatenate(ms, axis=0), xd_b[:, pair * LANES:(pair + 1) * LANES],
                         preferred_element_type=F32)
            y_diag_parts.append(jnp.where(lane < HEAD_DIM, yp[:L], yp[L:]))
    y_diag = jnp.concatenate(y_diag_parts, axis=1)
    y_off = jnp.concatenate(y_off_parts, axis=1)
    s_new = jnp.concatenate(s_new_parts, axis=1)

    y = y_diag + y_off * exp_a_e + dsk_ref[...] * xs
    state_ref[...] = state_ref[...] * exp_a_e[L - 1:L, :] + s_new

    y = y * _silu(z_ref[...].astype(F32))
    outs = []
    for g in range(SSM_GROUPS):
        yg = y[:, g * group_width:(g + 1) * group_width]
        ms = jnp.mean(yg * yg, axis=-1, keepdims=True)
        outs.append(yg * lax.rsqrt(ms + EPS))
    y_ref[...] = (jnp.concatenate(outs, axis=1) * ng_ref[...]).astype(BF16)


def _head_spread_matrix():
    e = np.zeros((2 * LANES, SSM_D_INNER), np.float32)
    for h in range(SSM_HEADS):
        e[h, h * HEAD_DIM:(h + 1) * HEAD_DIM] = 1.0
        e[LANES + h, h * HEAD_DIM:(h + 1) * HEAD_DIM] = 1.0
    return jnp.asarray(e, BF16)


def _pad_lanes(v):
    return jnp.zeros((1, LANES), F32).at[0, :v.shape[0]].set(v)


def _ssd(z, xbc, dt_raw, conv_w, conv_b, dt_bias, a_log, d_skip, norm_g):
    b, s, _ = z.shape
    L = SSM_CHUNK
    row = lambda width: _resident((1, width), lambda bi, c: (0, 0))
    return pl.pallas_call(
        _ssd_kernel,
        grid=(b, s // L),
        in_specs=[
            pl.BlockSpec((None, L, SSM_D_INNER), lambda bi, c: (bi, c, 0)),
            pl.BlockSpec((None, L, SSM_CONV_DIM), lambda bi, c: (bi, c, 0)),
            pl.BlockSpec((None, L, LANES), lambda bi, c: (bi, c, 0)),
            _resident((SSM_CONV, SSM_CONV_DIM), lambda bi, c: (0, 0)),
            row(SSM_CONV_DIM), row(LANES), row(LANES), row(SSM_D_INNER), row(SSM_D_INNER),
            _resident((2 * LANES, SSM_D_INNER), lambda bi, c: (0, 0)),
        ],
        out_specs=pl.BlockSpec((None, L, SSM_D_INNER), lambda bi, c: (bi, c, 0)),
        out_shape=jax.ShapeDtypeStruct((b, s, SSM_D_INNER), BF16),
        scratch_shapes=[pltpu.VMEM((BF16_SUBLANES, SSM_CONV_DIM), BF16), pltpu.VMEM((SSM_STATE, SSM_D_INNER), F32)],
        compiler_params=_cparams(("parallel", "arbitrary")),
        name="ssd_mixer",
    )(z, xbc, dt_raw, conv_w.T, conv_b.reshape(1, -1), _pad_lanes(dt_bias), _pad_lanes(a_log),
      jnp.repeat(d_skip, HEAD_DIM).reshape(1, -1), norm_g.reshape(1, -1), _head_spread_matrix())


def _split_lane_halves(x2):
    xf = x2.astype(F32)
    lane = lax.broadcasted_iota(jnp.int32, xf.shape, 1)
    lo = jnp.where(lane < HEAD_DIM, xf, 0.0)
    hi = jnp.where(lane >= HEAD_DIM, xf, 0.0)
    return jnp.concatenate([lo, hi], axis=0).astype(BF16)


def _scores_t(kb, qs):
    return lax.dot_general(kb, qs, (((1,), (1,)), ((), ())), preferred_element_type=F32)


def _pv_t(vb, wt):
    return lax.dot_general(vb, wt, (((0,), (0,)), ((), ())), preferred_element_type=F32)


def _sb_kernel(q_ref, k_ref, v_ref, o_ref, z_ref, cs_ref, tot_ref, acc_ref, carry_ref):
    T = q_ref.shape[0]
    n_streams = q_ref.shape[1] // LANES
    i = pl.program_id(2)
    qs = [_split_lane_halves(q_ref[:, p * LANES:(p + 1) * LANES]) for p in range(n_streams)]

    key = lax.broadcasted_iota(jnp.int32, (T, 2 * T), 0)
    qry = lax.broadcasted_iota(jnp.int32, (T, 2 * T), 1)
    strict = key < jnp.where(qry >= T, qry - T, qry)
    ur = lax.broadcasted_iota(jnp.int32, (T, T), 0)
    uc = lax.broadcasted_iota(jnp.int32, (T, T), 1)
    neg_suffix = jnp.where(uc >= ur, -1.0, 0.0).astype(BF16)
    streams = range(n_streams)
    sign_bit = jnp.uint32(0x80000000)

    def scores(j_lo, nb):
        rows = pl.ds(pl.multiple_of(j_lo * T, T), nb * T)
        return [_scores_t(k_ref[rows, p * LANES:(p + 1) * LANES], qs[p]) for p in streams]

    def softplus2(z, diagonal):
        neg_abs = lax.bitcast_convert_type(lax.bitcast_convert_type(z, jnp.uint32) | sign_bit, F32)
        sp = jnp.maximum(z, 0.0) + jnp.log(1.0 + jnp.exp2(neg_abs)) * LOG2E
        if diagonal:
            sp = jnp.where(strict, sp, 0.0)
        return sp.astype(BF16)

    def suffix_sums(sp, nb):
        parts = [jnp.dot(neg_suffix, sp[u * T:(u + 1) * T], preferred_element_type=F32) for u in range(nb)]
        total = parts[nb - 1][0:1, :]
        for u in reversed(range(nb - 1)):
            parts[u] = parts[u] + total
            total = parts[u][0:1, :]
        return (parts[0] if nb == 1 else jnp.concatenate(parts, axis=0)), total

    def weights(z, cs, carry, diagonal):
        w = jnp.exp2(z + (cs + carry))
        if diagonal:
            w = jnp.where(strict, w, 0.0)
        return w.astype(BF16)

    def accumulate(j_lo, nb, ws, tots):
        rows = pl.ds(pl.multiple_of(j_lo * T, T), nb * T)
        for p in streams:
            acc_ref[p] += _pv_t(v_ref[rows, p * LANES:(p + 1) * LANES], ws[p])
            carry_ref[p] += tots[p]

    def direct_step(j_lo, nb, diagonal=False):
        zs = scores(j_lo, nb)
        cts = [suffix_sums(softplus2(z, diagonal), nb) for z in zs]
        ws = [weights(zs[p], cts[p][0], carry_ref[p], diagonal) for p in streams]
        accumulate(j_lo, nb, ws, [ct[1] for ct in cts])

    def finish_produce(slot, zs):
        for p in streams:
            z_ref[slot, p] = zs[p]
        for p in streams:
            cs, tot = suffix_sums(softplus2(zs[p], False), 2)
            cs_ref[slot, p] = cs
            tot_ref[slot, p] = tot

    for p in streams:
        acc_ref[p] = jnp.zeros((LANES, 2 * T), F32)
        carry_ref[p] = jnp.zeros((1, 2 * T), F32)
    direct_step(i, 1, diagonal=True)
    n_double = i // 2
    pair_lo = lambda t: i - 2 - 2 * t

    def consume(t, slot, produce_next):
        ws = [weights(z_ref[slot, p], cs_ref[slot, p], carry_ref[p], False) for p in streams]
        tots = [tot_ref[slot, p] for p in streams]
        if produce_next:
            zs_next = scores(pair_lo(t + 1), 2)
        accumulate(pair_lo(t), 2, ws, tots)
        if produce_next:
            finish_produce(1 - slot, zs_next)

    @pl.when(n_double > 0)
    def _():
        finish_produce(0, scores(pair_lo(0), 2))

    n_twice = jnp.maximum(n_double - 1, 0) // 2

    def body(u, _):
        consume(2 * u, 0, True)
        consume(2 * u + 1, 1, True)
        return 0

    lax.fori_loop(0, n_twice, body, 0)
    left = n_double - 2 * n_twice

    @pl.when(left == 2)
    def _():
        consume(2 * n_twice, 0, True)
        consume(2 * n_twice + 1, 1, False)

    @pl.when(left == 1)
    def _():
        consume(2 * n_twice, 0, False)

    @pl.when(i - 2 * n_double > 0)
    def _():
        direct_step(0, 1)

    for p in range(n_streams):
        acc = acc_ref[p]
        pair_t = jnp.concatenate([acc[:HEAD_DIM, :T], acc[HEAD_DIM:, T:]], axis=0)
        o_ref[:, p * LANES:(p + 1) * LANES] = pair_t.T.astype(BF16)


def _stick_breaking(q, k, v):
    b, s, width = q.shape
    T = min(ATT_BLOCK, s)
    gw = SB_STREAMS * LANES
    return pl.pallas_call(
        _sb_kernel,
        grid=(b, width // gw, s // T),
        in_specs=[
            pl.BlockSpec((None, T, gw), lambda bi, g, i: (bi, i, g)),
            pl.BlockSpec((None, s, gw), lambda bi, g, i: (bi, 0, g)),
            pl.BlockSpec((None, s, gw), lambda bi, g, i: (bi, 0, g)),
        ],
        out_specs=pl.BlockSpec((None, T, gw), lambda bi, g, i: (bi, i, g)),
        out_shape=jax.ShapeDtypeStruct((b, s, width), BF16),
        scratch_shapes=[
            pltpu.VMEM((2, SB_STREAMS, 2 * T, 2 * T), F32),
            pltpu.VMEM((2, SB_STREAMS, 2 * T, 2 * T), F32),
            pltpu.VMEM((2, SB_STREAMS, 1, 2 * T), F32),
            pltpu.VMEM((SB_STREAMS, LANES, 2 * T), F32),
            pltpu.VMEM((SB_STREAMS, 1, 2 * T), F32),
        ],
        compiler_params=_cparams(("parallel", "parallel", "arbitrary")),
        name="stick_breaking_attention",
    )(q, k, v)


def _bucket_starts():
    max_exact = N_REL_BUCKETS // 2
    dist = np.arange(0, 4 * REL_MAX_DIST, dtype=np.float64)
    ratio = np.log(np.maximum(dist, max_exact) / max_exact) / math.log(REL_MAX_DIST / max_exact)
    scaled = ratio * (N_REL_BUCKETS - max_exact)
    frac = np.abs(scaled - np.round(scaled))
    interior = (dist > max_exact) & (dist < REL_MAX_DIST)
    assert frac[interior].min() > 1e-3, "a bucket edge sits on an integer distance"
    large = np.minimum(max_exact + np.floor(scaled + 1e-9).astype(np.int64), N_REL_BUCKETS - 1)
    bucket = np.where(dist < max_exact, dist.astype(np.int64), large)
    assert np.all(np.diff(bucket) >= 0) and bucket[-1] == N_REL_BUCKETS - 1
    return [int(np.argmax(bucket >= bkt)) for bkt in range(N_REL_BUCKETS)]


def _bias_table_kernel(rel_ref, o_ref, *, starts):
    h = pl.program_id(0)
    T = o_ref.shape[-1]
    key = lax.broadcasted_iota(jnp.int32, (T, T), 0)
    qry = lax.broadcasted_iota(jnp.int32, (T, T), 1)
    far = rel_ref[N_REL_BUCKETS - 1, h]
    for o_blk in range(o_ref.shape[0]):
        dist = qry - key + o_blk * T
        bias = jnp.full((T, T), rel_ref[0, h], F32)
        for bkt in range(1, N_REL_BUCKETS):
            bias = jnp.where(dist >= starts[bkt], rel_ref[bkt, h], bias)
        o_ref[o_blk] = jnp.where(dist >= 0, (bias - far) * LOG2E, -jnp.inf)


def _bias_tables(rel_bias, T):
    starts = _bucket_starts()
    assert T + 1 >= starts[-1]
    return pl.pallas_call(
        functools.partial(_bias_table_kernel, starts=starts),
        grid=(DIFF_HEADS,),
        in_specs=[pl.BlockSpec(memory_space=pltpu.SMEM)],
        out_specs=pl.BlockSpec((None, 2, T, T), lambda h: (h, 0, 0, 0)),
        out_shape=jax.ShapeDtypeStruct((DIFF_HEADS, 2, T, T), F32),
        compiler_params=_cparams(("arbitrary",)),
        name="t5_bias_tables",
    )(rel_bias)


def _diff_kernel(q_ref, k_ref, v_ref, tb_ref, lq1_ref, lk1_ref, lq2_ref, lk2_ref, sg_ref, o_ref,
                 s_ref, m_ref, l_ref, acc_ref, *, lambda_init):
    T = q_ref.shape[0]
    n_streams = q_ref.shape[1] // LANES
    i = pl.program_id(2)
    qs = [_split_lane_halves(q_ref[:, h * LANES:(h + 1) * LANES]) for h in range(n_streams)]

    heads = range(n_streams)

    def scores(j_lo, nb):
        rows = pl.ds(pl.multiple_of(j_lo * T, T), nb * T)
        return [_scores_t(k_ref[rows, h * LANES:(h + 1) * LANES], qs[h]) for h in heads]

    def softmax_part(ss, table_rows):
        out = []
        for h in heads:
            read = ss[h] if callable(ss[h]) else (lambda v=ss[h]: v)
            if table_rows is not None:
                biases = [tb_ref[h, o] for o in table_rows]
                bias = biases[0] if len(biases) == 1 else jnp.concatenate(biases, axis=0)
                s = read() + jnp.concatenate([bias, bias], axis=1)
                read = lambda v=s: v
            m_old = m_ref[h]
            m_new = jnp.maximum(m_old, jnp.max(read(), axis=0, keepdims=True))
            alpha = jnp.exp2(m_old - m_new)
            p = jnp.exp2(read() - m_new)
            m_ref[h] = m_new
            l_ref[h] = alpha * l_ref[h] + jnp.sum(p, axis=0, keepdims=True)
            out.append((alpha, p.astype(BF16)))
        return out

    def accumulate(j_lo, nb, aps):
        rows = pl.ds(pl.multiple_of(j_lo * T, T), nb * T)
        for h in heads:
            alpha, p = aps[h]
            acc_ref[h] = alpha * acc_ref[h] + _pv_t(v_ref[rows, h * LANES:(h + 1) * LANES], p)

    for h in heads:
        m_ref[h] = jnp.full((1, 2 * T), -jnp.inf, F32)
        l_ref[h] = jnp.zeros((1, 2 * T), F32)
        acc_ref[h] = jnp.zeros((DIFF_V_DIM, 2 * T), F32)
    n_far = jnp.maximum(i - 1, 0)
    n_double = n_far // 2

    def produce(t, slot):
        for h, s in enumerate(scores(2 * t, 2)):
            s_ref[slot, h] = s

    def consume(t, slot, produce_next):
        aps = softmax_part([lambda h=h: s_ref[slot, h] for h in heads], None)
        if produce_next:
            produce(t + 1, 1 - slot)
        accumulate(2 * t, 2, aps)

    @pl.when(n_double > 0)
    def _():
        produce(0, 0)

    n_twice = jnp.maximum(n_double - 1, 0) // 2

    def body(u, _):
        consume(2 * u, 0, True)
        consume(2 * u + 1, 1, True)
        return 0

    lax.fori_loop(0, n_twice, body, 0)
    left = n_double - 2 * n_twice

    @pl.when(left == 2)
    def _():
        consume(2 * n_twice, 0, True)
        consume(2 * n_twice + 1, 1, False)

    @pl.when(left == 1)
    def _():
        consume(2 * n_twice, 0, False)

    @pl.when(n_far - 2 * n_double > 0)
    def _():
        accumulate(n_far - 1, 1, softmax_part(scores(n_far - 1, 1), None))

    @pl.when(i > 0)
    def _():
        accumulate(i - 1, 2, softmax_part(scores(i - 1, 2), [1, 0]))

    @pl.when(i == 0)
    def _():
        accumulate(0, 1, softmax_part(scores(0, 1), [0]))

    lam = (jnp.exp(jnp.sum(lq1_ref[...] * lk1_ref[...], axis=-1, keepdims=True))
           - jnp.exp(jnp.sum(lq2_ref[...] * lk2_ref[...], axis=-1, keepdims=True)) + lambda_init)
    for h in range(n_streams):
        l, acc = l_ref[h], acc_ref[h]
        o_t = acc[:, :T] / l[:, :T] - lam * (acc[:, T:] / l[:, T:])
        o = o_t.T
        ms = jnp.mean(o * o, axis=-1, keepdims=True)
        o = o * lax.rsqrt(ms + EPS) * sg_ref[...] * (1.0 - lambda_init)
        o_ref[:, h * LANES:(h + 1) * LANES] = o.astype(BF16)


def _diff_attention(q, k, v, tables, lq1, lk1, lq2, lk2, subln_g, lambda_init):
    b, s, width = q.shape
    T = tables.shape[-1]
    vec = lambda n: _resident((1, n), lambda bi, g, i: (0, 0))
    gw = DIFF_STREAMS * LANES
    return pl.pallas_call(
        functools.partial(_diff_kernel, lambda_init=lambda_init),
        grid=(b, width // gw, s // T),
        in_specs=[
            pl.BlockSpec((None, T, gw), lambda bi, g, i: (bi, i, g)),
            pl.BlockSpec((None, s, gw), lambda bi, g, i: (bi, 0, g)),
            pl.BlockSpec((None, s, gw), lambda bi, g, i: (bi, 0, g)),
            pl.BlockSpec((DIFF_STREAMS,) + tables.shape[1:], lambda bi, g, i: (g, 0, 0, 0)),
            vec(HEAD_DIM), vec(HEAD_DIM), vec(HEAD_DIM), vec(HEAD_DIM), vec(DIFF_V_DIM),
        ],
        out_specs=pl.BlockSpec((None, T, gw), lambda bi, g, i: (bi, i, g)),
        out_shape=jax.ShapeDtypeStruct((b, s, width), BF16),
        scratch_shapes=[
            pltpu.VMEM((2, DIFF_STREAMS, 2 * T, 2 * T), F32),
            pltpu.VMEM((DIFF_STREAMS, 1, 2 * T), F32),
            pltpu.VMEM((DIFF_STREAMS, 1, 2 * T), F32),
            pltpu.VMEM((DIFF_STREAMS, DIFF_V_DIM, 2 * T), F32),
        ],
        compiler_params=_cparams(("parallel", "parallel", "arbitrary")),
        name="differential_attention",
    )(q, k, v, tables, lq1.reshape(1, -1), lk1.reshape(1, -1), lq2.reshape(1, -1), lk2.reshape(1, -1),
      subln_g.reshape(1, -1))


def _outproj_kernel(x_ref, mod_ref, ys_ref, yb_ref, yd_ref, w_ref, o_ref):
    o1 = SSM_D_INNER
    o2 = o1 + SB_WIDTH
    y = jnp.dot(ys_ref[...], w_ref[0:o1, :], preferred_element_type=F32)
    y = y + jnp.dot(yb_ref[...], w_ref[o1:o2, :], preferred_element_type=F32)
    y = y + jnp.dot(yd_ref[...], w_ref[o2:, :], preferred_element_type=F32)
    o_ref[...] = x_ref[...] + mod_ref[5:6, :] * y


def _outproj(x, mod, y_ssm, y_sb, y_diff, w_out):
    b, s, d = x.shape
    tm = min(TOKEN_TILE, s)
    tok = lambda width: pl.BlockSpec((None, tm, width), lambda bi, i: (bi, i, 0))
    return pl.pallas_call(
        _outproj_kernel,
        grid=(b, s // tm),
        in_specs=[
            tok(d),
            pl.BlockSpec((None, N_MOD, d), lambda bi, i: (bi, 0, 0)),
            tok(SSM_D_INNER), tok(SB_WIDTH), tok(DIFF_WIDTH),
            _resident(w_out.shape, lambda bi, i: (0, 0)),
        ],
        out_specs=tok(d),
        out_shape=jax.ShapeDtypeStruct((b, s, d), F32),
        compiler_params=_cparams(("parallel", "parallel")),
        name="mixer_outproj",
    )(x, mod, y_ssm, y_sb, y_diff, w_out.astype(BF16))


def kernel(x, c, ada_w, ada_b, ffn1_norm, ffn1_w13, ffn1_w2, mix_norm, w_in, ssm_conv_w, ssm_conv_b, ssm_dt_bias, ssm_a_log, ssm_d, ssm_norm, diff_lambda_q1, diff_lambda_k1, diff_lambda_q2, diff_lambda_k2, diff_subln, rel_bias, w_out, ffn2_norm, ffn2_w13, ffn2_w2, final_norm):
    depth = ada_w.shape[0]
    s = x.shape[1]
    mods = _ada_modulation(c, ada_w, ada_b)
    tables = _bias_tables(rel_bias, min(ATT_BLOCK, s))
    for l in range(depth):
        mod = mods[l]
        x = _ffn(x, mod, ffn1_norm[l], ffn1_w13[l], ffn1_w2[l], mod_row=0)
        lambda_init = 0.8 - 0.6 * math.exp(-0.3 * l)
        z, xbc, dt_raw, sq, sk, sv, dq, dk, dv = _inproj(x, mod, mix_norm[l], w_in[l])
        y_ssm = _ssd(z, xbc, dt_raw, ssm_conv_w[l], ssm_conv_b[l], ssm_dt_bias[l], ssm_a_log[l], ssm_d[l],
                     ssm_norm[l])
        y_sb = _stick_breaking(sq, sk, sv)
        y_diff = _diff_attention(dq, dk, dv, tables, diff_lambda_q1[l], diff_lambda_k1[l], diff_lambda_q2[l],
                                 diff_lambda_k2[l], diff_subln[l], lambda_init)
        x = _outproj(x, mod, y_ssm, y_sb, y_diff, w_out[l])
        x = _ffn(x, mod, ffn2_norm[l], ffn2_w13[l], ffn2_w2[l], mod_row=6,
                 final_g=final_norm if l == depth - 1 else None)
    return x
```

```python
import functools
import math

import numpy as np
import jax
import jax.numpy as jnp
from jax import lax
from jax.experimental import pallas as pl
from jax.experimental.pallas import tpu as pltpu

F32 = jnp.float32
BF16 = jnp.bfloat16
HIGHEST = lax.Precision.HIGHEST
LOG2E = math.log2(math.e)

HEAD_DIM = 64
SSM_HEADS = 16
SSM_GROUPS = 2
SSM_STATE = 128
SSM_CONV = 4
SSM_CHUNK = 128
SSM_D_INNER = SSM_HEADS * HEAD_DIM
SSM_CONV_DIM = SSM_D_INNER + 2 * SSM_GROUPS * SSM_STATE
SB_HEADS = 8
SB_WIDTH = SB_HEADS * HEAD_DIM
DIFF_HEADS = 4
DIFF_V_DIM = 2 * HEAD_DIM
DIFF_WIDTH = DIFF_HEADS * DIFF_V_DIM
N_MOD = 9
N_REL_BUCKETS = 32
REL_MAX_DIST = 128
EPS = 1e-6

LANES = 128
BF16_SUBLANES = 16
VMEM_LIMIT = 56 * 1024 * 1024

TOKEN_TILE = 512
FF_CHUNK = 256
ATT_BLOCK = 256

OFF_Z = 0
OFF_XBC = OFF_Z + SSM_D_INNER
OFF_SQ = OFF_XBC + SSM_CONV_DIM
OFF_SK = OFF_SQ + SB_WIDTH
OFF_SV = OFF_SK + SB_WIDTH
OFF_DQ = OFF_SV + SB_WIDTH
OFF_DK = OFF_DQ + DIFF_WIDTH
OFF_DV = OFF_DK + DIFF_WIDTH
OFF_DT = OFF_DV + DIFF_WIDTH
IN_PACKED = OFF_DT + LANES


def _cparams(semantics):
    return pltpu.CompilerParams(dimension_semantics=semantics, vmem_limit_bytes=VMEM_LIMIT)


SB_STREAMS = 4
DIFF_STREAMS = 4


def _resident(block_shape, index_map):
    return pl.BlockSpec(block_shape, index_map, pipeline_mode=pl.Buffered(1))


def _silu(x):
    hx = 0.5 * x
    return hx + hx * jnp.tanh(hx)


def _softplus(x):
    return jnp.maximum(x, 0.0) + jnp.log1p(jnp.exp(-jnp.abs(x)))


def _norm_modulate(x, g, shift, scale):
    ms = jnp.mean(x * x, axis=-1, keepdims=True)
    y = x * lax.rsqrt(ms + EPS) * g
    return y * (1.0 + scale) + shift


def _ada_kernel(c_ref, w_ref, b_ref, o_ref):
    cond = _silu(c_ref[...])
    o_ref[...] = jnp.dot(cond, w_ref[...], preferred_element_type=F32, precision=HIGHEST) + b_ref[...]


def _ada_modulation(c, ada_w, ada_b):
    depth, d, nmod = ada_w.shape
    b = c.shape[0]
    rows = 8 * pl.cdiv(b, 8)
    c_pad = jnp.zeros((rows, d), F32).at[:b].set(c)
    out = pl.pallas_call(
        _ada_kernel,
        grid=(depth, nmod // d),
        in_specs=[
            pl.BlockSpec((rows, d), lambda l, j: (0, 0)),
            pl.BlockSpec((None, d, d), lambda l, j: (l, 0, j)),
            pl.BlockSpec((None, 1, d), lambda l, j: (l, 0, j)),
        ],
        out_specs=pl.BlockSpec((None, rows, d), lambda l, j: (l, 0, j)),
        out_shape=jax.ShapeDtypeStruct((depth, rows, nmod), F32),
        compiler_params=_cparams(("arbitrary", "arbitrary")),
        name="ada_modulation",
    )(c_pad, ada_w, ada_b.reshape(depth, 1, nmod))
    return out[:, :b].reshape(depth, b, N_MOD, d)


def _ffn_kernel(x_ref, mod_ref, g_ref, w13_ref, w2_ref, *rest, mod_row, final_norm):
    if final_norm:
        fg_ref, o_ref, acc_ref = rest
    else:
        o_ref, acc_ref = rest
    x = x_ref[...]
    shift = mod_ref[mod_row:mod_row + 1, :]
    scale = mod_ref[mod_row + 1:mod_row + 2, :]
    gate = mod_ref[mod_row + 2:mod_row + 3, :]
    h = _norm_modulate(x, g_ref[...], shift, scale).astype(BF16)
    d_ff = w2_ref.shape[0]
    for j in range(d_ff // FF_CHUNK):
        cols = slice(j * FF_CHUNK, (j + 1) * FF_CHUNK)
        w1 = w13_ref[:, cols].astype(BF16)
        w3 = w13_ref[:, d_ff + j * FF_CHUNK:d_ff + (j + 1) * FF_CHUNK].astype(BF16)
        a = jnp.dot(h, w1, preferred_element_type=F32)
        u = jnp.dot(h, w3, preferred_element_type=F32)
        act = (_silu(a) * u).astype(BF16)
        part = jnp.dot(act, w2_ref[cols, :].astype(BF16), preferred_element_type=F32)
        if j == 0:
            acc_ref[...] = part
        else:
            acc_ref[...] += part
    y = x + (0.5 * gate) * acc_ref[...]
    if final_norm:
        ms = jnp.mean(y * y, axis=-1, keepdims=True)
        y = y * lax.rsqrt(ms + EPS) * fg_ref[...]
    o_ref[...] = y


def _ffn(x, mod, g, w13, w2, layer, mod_row, final_g=None):
    b, s, d = x.shape
    d_ff = w2.shape[1]
    assert d_ff % FF_CHUNK == 0 and d_ff % LANES == 0
    tm = min(TOKEN_TILE, s)
    final_norm = final_g is not None
    in_specs = [
        pl.BlockSpec((None, tm, d), lambda bi, i: (bi, i, 0)),
        pl.BlockSpec((None, N_MOD, d), lambda bi, i: (bi, 0, 0)),
        _resident((1, d), lambda bi, i: (0, 0)),
        _resident((None, d, 2 * d_ff), lambda bi, i: (layer, 0, 0)),
        _resident((None, d_ff, d), lambda bi, i: (layer, 0, 0)),
    ]
    args = [x, mod, g.reshape(1, d), w13, w2]
    if final_norm:
        in_specs.append(_resident((1, d), lambda bi, i: (0, 0)))
        args.append(final_g.reshape(1, d))
    return pl.pallas_call(
        functools.partial(_ffn_kernel, mod_row=mod_row, final_norm=final_norm),
        grid=(b, s // tm),
        in_specs=in_specs,
        out_specs=pl.BlockSpec((None, tm, d), lambda bi, i: (bi, i, 0)),
        out_shape=jax.ShapeDtypeStruct((b, s, d), F32),
        scratch_shapes=[pltpu.VMEM((tm, d), F32)],
        compiler_params=_cparams(("parallel", "parallel")),
        name="ffn_final" if final_norm else "ffn",
    )(*args)


def _inproj_kernel(x_ref, mod_ref, g_ref, w_ref, z_ref, xbc_ref, dt_ref,
                   sq_ref, sk_ref, sv_ref, dq_ref, dk_ref, dv_ref):
    x = x_ref[...]
    h = _norm_modulate(x, g_ref[...], mod_ref[3:4, :], mod_ref[4:5, :]).astype(BF16)

    def proj(off, width):
        return jnp.dot(h, w_ref[:, off:off + width], preferred_element_type=F32)

    qk_scale = HEAD_DIM ** -0.5 * LOG2E
    z_ref[...] = proj(OFF_Z, SSM_D_INNER).astype(BF16)
    xbc_ref[...] = proj(OFF_XBC, SSM_CONV_DIM).astype(BF16)
    dt_ref[...] = proj(OFF_DT, LANES)
    sq_ref[...] = (proj(OFF_SQ, SB_WIDTH) * qk_scale).astype(BF16)
    sk_ref[...] = proj(OFF_SK, SB_WIDTH).astype(BF16)
    sv_ref[...] = proj(OFF_SV, SB_WIDTH).astype(BF16)
    dq_ref[...] = (proj(OFF_DQ, DIFF_WIDTH) * qk_scale).astype(BF16)
    dk_ref[...] = proj(OFF_DK, DIFF_WIDTH).astype(BF16)
    dv_ref[...] = proj(OFF_DV, DIFF_WIDTH).astype(BF16)


def _pack_w_in(w_in):
    depth, d, _ = w_in.shape
    dt0 = SSM_D_INNER + SSM_CONV_DIM
    dt1 = dt0 + SSM_HEADS
    packed = jnp.concatenate(
        [w_in[..., :dt0], w_in[..., dt1:], w_in[..., dt0:dt1],
         jnp.zeros((depth, d, LANES - SSM_HEADS), w_in.dtype)], axis=-1)
    return packed.astype(BF16)


def _inproj(x, mod, g, w, layer):
    b, s, d = x.shape
    tm = min(TOKEN_TILE, s)
    widths = (SSM_D_INNER, SSM_CONV_DIM, LANES) + (SB_WIDTH,) * 3 + (DIFF_WIDTH,) * 3
    dtypes = (BF16, BF16, F32) + (BF16,) * 6
    return pl.pallas_call(
        _inproj_kernel,
        grid=(b, s // tm),
        in_specs=[
            pl.BlockSpec((None, tm, d), lambda bi, i: (bi, i, 0)),
            pl.BlockSpec((None, N_MOD, d), lambda bi, i: (bi, 0, 0)),
            _resident((1, d), lambda bi, i: (0, 0)),
            _resident((None, d, IN_PACKED), lambda bi, i: (layer, 0, 0)),
        ],
        out_specs=[pl.BlockSpec((None, tm, wd), lambda bi, i: (bi, i, 0)) for wd in widths],
        out_shape=[jax.ShapeDtypeStruct((b, s, wd), dt) for wd, dt in zip(widths, dtypes)],
        compiler_params=_cparams(("parallel", "parallel")),
        name="mixer_inproj",
    )(x, mod, g.reshape(1, d), w)


def _ssd_kernel(z_ref, xbc_ref, dtr_ref, cw_ref, cb_ref, dtb_ref, alog_ref, dsk_ref, ng_ref, e_ref,
                y_ref, tail_ref, state_ref):
    L = z_ref.shape[0]
    pad = tail_ref.shape[0]
    c = pl.program_id(1)

    @pl.when(c == 0)
    def _():
        tail_ref[...] = jnp.zeros(tail_ref.shape, BF16)
        state_ref[...] = jnp.zeros(state_ref.shape, F32)

    u = xbc_ref[...]
    u_ext = jnp.concatenate([tail_ref[...], u], axis=0)
    tail_ref[...] = u[L - pad:, :]
    trow = lax.broadcasted_iota(jnp.int32, (L, pad + L), 0)
    tcol = lax.broadcasted_iota(jnp.int32, (L, pad + L), 1)
    shifts = jnp.concatenate(
        [jnp.where(tcol == trow + (pad - (SSM_CONV - 1) + k), 1.0, 0.0) for k in range(SSM_CONV)], axis=0)
    shifted = jnp.dot(shifts.astype(BF16), u_ext, preferred_element_type=F32)
    conv = cb_ref[...]
    for k in range(SSM_CONV):
        conv = conv + cw_ref[k:k + 1, :] * shifted[k * L:(k + 1) * L]
    xa = _silu(conv)
    xs = xa[:, :SSM_D_INNER]
    bmat = xa[:, SSM_D_INNER:SSM_D_INNER + SSM_GROUPS * SSM_STATE]
    cmat = xa[:, SSM_D_INNER + SSM_GROUPS * SSM_STATE:]

    dt = _softplus(dtr_ref[...] + dtb_ref[...])
    a = -jnp.exp(alog_ref[...])
    da = dt * a
    row = lax.broadcasted_iota(jnp.int32, (L, L), 0)
    col = lax.broadcasted_iota(jnp.int32, (L, L), 1)
    causal = row >= col
    a_cum = jnp.dot(causal.astype(F32), da, preferred_element_type=F32, precision=HIGHEST)
    a_cum_t = a_cum.T
    exp_a = jnp.exp(a_cum)
    decay_to_end = jnp.exp(a_cum[L - 1:L, :] - a_cum)

    stacked = jnp.concatenate([dt, exp_a, decay_to_end], axis=0)
    hi = stacked.astype(BF16)
    lo = (stacked - hi.astype(F32)).astype(BF16)
    spread = jnp.dot(jnp.concatenate([hi, lo], axis=1), e_ref[...], preferred_element_type=F32)
    dt_e = spread[0:L]
    exp_a_e = spread[L:2 * L]
    dte_e = spread[2 * L:3 * L]

    xd = xs * dt_e
    xd_b = xd.astype(BF16)
    xdd_b = (xd * dte_e).astype(BF16)
    prev_b = state_ref[...].astype(BF16)
    lane = lax.broadcasted_iota(jnp.int32, (L, LANES), 1)
    heads_per_group = SSM_HEADS // SSM_GROUPS
    group_width = heads_per_group * HEAD_DIM

    y_diag_parts = []
    y_off_parts = []
    s_new_parts = []
    for g in range(SSM_GROUPS):
        bg = bmat[:, g * SSM_STATE:(g + 1) * SSM_STATE]
        cg_b = cmat[:, g * SSM_STATE:(g + 1) * SSM_STATE].astype(BF16)
        bg_b = bg.astype(BF16)
        bg_t_b = bg.T.astype(BF16)
        gs = slice(g * group_width, (g + 1) * group_width)
        cb = lax.dot_general(cg_b, bg_b, (((1,), (1,)), ((), ())), preferred_element_type=F32)
        y_off_parts.append(jnp.dot(cg_b, prev_b[:, gs], preferred_element_type=F32))
        s_new_parts.append(jnp.dot(bg_t_b, xdd_b[:, gs], preferred_element_type=F32))
        for hp in range(heads_per_group // 2):
            ms = []
            for h in (g * heads_per_group + 2 * hp, g * heads_per_group + 2 * hp + 1):
                seg = a_cum[:, h:h + 1] - a_cum_t[h:h + 1, :]
                decay = jnp.exp(jnp.where(causal, seg, -jnp.inf))
                ms.append((cb * decay).astype(BF16))
            pair = (g * heads_per_group) // 2 + hp
            yp = jnp.dot(jnp.concatenate(ms, axis=0), xd_b[:, pair * LANES:(pair + 1) * LANES],
                         preferred_element_type=F32)
            y_diag_parts.append(jnp.where(lane < HEAD_DIM, yp[:L], yp[L:]))
    y_diag = jnp.concatenate(y_diag_parts, axis=1)
    y_off = jnp.concatenate(y_off_parts, axis=1)
    s_new = jnp.concatenate(s_new_parts, axis=1)

    y = y_diag + y_off * exp_a_e + dsk_ref[...] * xs
    state_ref[...] = state_ref[...] * exp_a_e[L - 1:L, :] + s_new

    y = y * _silu(z_ref[...].astype(F32))
    outs = []
    for g in range(SSM_GROUPS):
        yg = y[:, g * group_width:(g + 1) * group_width]
        ms = jnp.mean(yg * yg, axis=-1, keepdims=True)
        outs.append(yg * lax.rsqrt(ms + EPS))
    y_ref[...] = (jnp.concatenate(outs, axis=1) * ng_ref[...]).astype(BF16)


def _head_spread_matrix():
    e = np.zeros((2 * LANES, SSM_D_INNER), np.float32)
    for h in range(SSM_HEADS):
        e[h, h * HEAD_DIM:(h + 1) * HEAD_DIM] = 1.0
        e[LANES + h, h * HEAD_DIM:(h + 1) * HEAD_DIM] = 1.0
    return jnp.asarray(e, BF16)


def _pad_lanes(v):
    return jnp.zeros((1, LANES), F32).at[0, :v.shape[0]].set(v)


def _ssd(z, xbc, dt_raw, conv_w, conv_b, dt_bias, a_log, d_skip, norm_g):
    b, s, _ = z.shape
    L = SSM_CHUNK
    row = lambda width: _resident((1, width), lambda bi, c: (0, 0))
    return pl.pallas_call(
        _ssd_kernel,
        grid=(b, s // L),
        in_specs=[
            pl.BlockSpec((None, L, SSM_D_INNER), lambda bi, c: (bi, c, 0)),
            pl.BlockSpec((None, L, SSM_CONV_DIM), lambda bi, c: (bi, c, 0)),
            pl.BlockSpec((None, L, LANES), lambda bi, c: (bi, c, 0)),
            _resident((SSM_CONV, SSM_CONV_DIM), lambda bi, c: (0, 0)),
            row(SSM_CONV_DIM), row(LANES), row(LANES), row(SSM_D_INNER), row(SSM_D_INNER),
            _resident((2 * LANES, SSM_D_INNER), lambda bi, c: (0, 0)),
        ],
        out_specs=pl.BlockSpec((None, L, SSM_D_INNER), lambda bi, c: (bi, c, 0)),
        out_shape=jax.ShapeDtypeStruct((b, s, SSM_D_INNER), BF16),
        scratch_shapes=[pltpu.VMEM((BF16_SUBLANES, SSM_CONV_DIM), BF16), pltpu.VMEM((SSM_STATE, SSM_D_INNER), F32)],
        compiler_params=_cparams(("parallel", "arbitrary")),
        name="ssd_mixer",
    )(z, xbc, dt_raw, conv_w.T, conv_b.reshape(1, -1), _pad_lanes(dt_bias), _pad_lanes(a_log),
      jnp.repeat(d_skip, HEAD_DIM).reshape(1, -1), norm_g.reshape(1, -1), _head_spread_matrix())


def _split_lane_halves(x2):
    xf = x2.astype(F32)
    lane = lax.broadcasted_iota(jnp.int32, xf.shape, 1)
    lo = jnp.where(lane < HEAD_DIM, xf, 0.0)
    hi = jnp.where(lane >= HEAD_DIM, xf, 0.0)
    return jnp.concatenate([lo, hi], axis=0).astype(BF16)


def _scores_t(kb, qs):
    return lax.dot_general(kb, qs, (((1,), (1,)), ((), ())), preferred_element_type=F32)


def _pv_t(vb, wt):
    return lax.dot_general(vb, wt, (((0,), (0,)), ((), ())), preferred_element_type=F32)


def _sb_kernel(q_ref, k_ref, v_ref, o_ref, z_ref, cs_ref, tot_ref, acc_ref, carry_ref):
    T = q_ref.shape[0]
    n_streams = q_ref.shape[1] // LANES
    i = pl.program_id(2)
    qs = [_split_lane_halves(q_ref[:, p * LANES:(p + 1) * LANES]) for p in range(n_streams)]

    key = lax.broadcasted_iota(jnp.int32, (T, 2 * T), 0)
    qry = lax.broadcasted_iota(jnp.int32, (T, 2 * T), 1)
    strict = key < jnp.where(qry >= T, qry - T, qry)
    ur = lax.broadcasted_iota(jnp.int32, (T, T), 0)
    uc = lax.broadcasted_iota(jnp.int32, (T, T), 1)
    neg_suffix = jnp.where(uc >= ur, -1.0, 0.0).astype(BF16)
    streams = range(n_streams)
    sign_bit = jnp.uint32(0x80000000)

    def scores(j_lo, nb):
        rows = pl.ds(pl.multiple_of(j_lo * T, T), nb * T)
        return [_scores_t(k_ref[rows, p * LANES:(p + 1) * LANES], qs[p]) for p in streams]

    def softplus2(z, diagonal):
        neg_abs = lax.bitcast_convert_type(lax.bitcast_convert_type(z, jnp.uint32) | sign_bit, F32)
        sp = jnp.maximum(z, 0.0) + jnp.log(1.0 + jnp.exp2(neg_abs)) * LOG2E
        if diagonal:
            sp = jnp.where(strict, sp, 0.0)
        return sp.astype(BF16)

    def suffix_sums(sp, nb):
        parts = [jnp.dot(neg_suffix, sp[u * T:(u + 1) * T], preferred_element_type=F32) for u in range(nb)]
        total = parts[nb - 1][0:1, :]
        for u in reversed(range(nb - 1)):
            parts[u] = parts[u] + total
            total = parts[u][0:1, :]
        return (parts[0] if nb == 1 else jnp.concatenate(parts, axis=0)), total

    def weights(z, cs, carry, diagonal):
        w = jnp.exp2(z + (cs + carry))
        if diagonal:
            w = jnp.where(strict, w, 0.0)
        return w.astype(BF16)

    def accumulate(j_lo, nb, ws, tots):
        rows = pl.ds(pl.multiple_of(j_lo * T, T), nb * T)
        for p in streams:
            acc_ref[p] += _pv_t(v_ref[rows, p * LANES:(p + 1) * LANES], ws[p])
            carry_ref[p] += tots[p]

    def direct_step(j_lo, nb, diagonal=False):
        zs = scores(j_lo, nb)
        cts = [suffix_sums(softplus2(z, diagonal), nb) for z in zs]
        ws = [weights(zs[p], cts[p][0], carry_ref[p], diagonal) for p in streams]
        accumulate(j_lo, nb, ws, [ct[1] for ct in cts])

    def finish_produce(slot, zs):
        for p in streams:
            z_ref[slot, p] = zs[p]
        for p in streams:
            cs, tot = suffix_sums(softplus2(zs[p], False), 2)
            cs_ref[slot, p] = cs
            tot_ref[slot, p] = tot

    for p in streams:
        acc_ref[p] = jnp.zeros((LANES, 2 * T), F32)
        carry_ref[p] = jnp.zeros((1, 2 * T), F32)
    direct_step(i, 1, diagonal=True)
    n_double = i // 2
    pair_lo = lambda t: i - 2 - 2 * t

    def consume(t, slot, produce_next):
        ws = [weights(z_ref[slot, p], cs_ref[slot, p], carry_ref[p], False) for p in streams]
        tots = [tot_ref[slot, p] for p in streams]
        if produce_next:
            zs_next = scores(pair_lo(t + 1), 2)
        accumulate(pair_lo(t), 2, ws, tots)
        if produce_next:
            finish_produce(1 - slot, zs_next)

    @pl.when(n_double > 0)
    def _():
        finish_produce(0, scores(pair_lo(0), 2))

    n_twice = jnp.maximum(n_double - 1, 0) // 2

    def body(u, _):
        consume(2 * u, 0, True)
        consume(2 * u + 1, 1, True)
        return 0

    lax.fori_loop(0, n_twice, body, 0)
    left = n_double - 2 * n_twice

    @pl.when(left == 2)
    def _():
        consume(2 * n_twice, 0, True)
        consume(2 * n_twice + 1, 1, False)

    @pl.when(left == 1)
    def _():
        consume(2 * n_twice, 0, False)

    @pl.when(i - 2 * n_double > 0)
    def _():
        direct_step(0, 1)

    for p in range(n_streams):
        acc = acc_ref[p]
        pair_t = jnp.concatenate([acc[:HEAD_DIM, :T], acc[HEAD_DIM:, T:]], axis=0)
        o_ref[:, p * LANES:(p + 1) * LANES] = pair_t.T.astype(BF16)


def _stick_breaking(q, k, v):
    b, s, width = q.shape
    T = min(ATT_BLOCK, s)
    gw = SB_STREAMS * LANES
    return pl.pallas_call(
        _sb_kernel,
        grid=(b, width // gw, s // T),
        in_specs=[
            pl.BlockSpec((None, T, gw), lambda bi, g, i: (bi, i, g)),
            pl.BlockSpec((None, s, gw), lambda bi, g, i: (bi, 0, g)),
            pl.BlockSpec((None, s, gw), lambda bi, g, i: (bi, 0, g)),
        ],
        out_specs=pl.BlockSpec((None, T, gw), lambda bi, g, i: (bi, i, g)),
        out_shape=jax.ShapeDtypeStruct((b, s, width), BF16),
        scratch_shapes=[
            pltpu.VMEM((2, SB_STREAMS, 2 * T, 2 * T), F32),
            pltpu.VMEM((2, SB_STREAMS, 2 * T, 2 * T), F32),
            pltpu.VMEM((2, SB_STREAMS, 1, 2 * T), F32),
            pltpu.VMEM((SB_STREAMS, LANES, 2 * T), F32),
            pltpu.VMEM((SB_STREAMS, 1, 2 * T), F32),
        ],
        compiler_params=_cparams(("parallel", "parallel", "arbitrary")),
        name="stick_breaking_attention",
    )(q, k, v)


def _bucket_starts():
    max_exact = N_REL_BUCKETS // 2
    dist = np.arange(0, 4 * REL_MAX_DIST, dtype=np.float64)
    ratio = np.log(np.maximum(dist, max_exact) / max_exact) / math.log(REL_MAX_DIST / max_exact)
    scaled = ratio * (N_REL_BUCKETS - max_exact)
    frac = np.abs(scaled - np.round(scaled))
    interior = (dist > max_exact) & (dist < REL_MAX_DIST)
    assert frac[interior].min() > 1e-3, "a bucket edge sits on an integer distance"
    large = np.minimum(max_exact + np.floor(scaled + 1e-9).astype(np.int64), N_REL_BUCKETS - 1)
    bucket = np.where(dist < max_exact, dist.astype(np.int64), large)
    assert np.all(np.diff(bucket) >= 0) and bucket[-1] == N_REL_BUCKETS - 1
    return [int(np.argmax(bucket >= bkt)) for bkt in range(N_REL_BUCKETS)]


def _bias_table_kernel(rel_ref, o_ref, *, starts):
    h = pl.program_id(0)
    T = o_ref.shape[-1]
    key = lax.broadcasted_iota(jnp.int32, (T, T), 0)
    qry = lax.broadcasted_iota(jnp.int32, (T, T), 1)
    far = rel_ref[N_REL_BUCKETS - 1, h]
    for o_blk in range(o_ref.shape[0]):
        dist = qry - key + o_blk * T
        bias = jnp.full((T, T), rel_ref[0, h], F32)
        for bkt in range(1, N_REL_BUCKETS):
            bias = jnp.where(dist >= starts[bkt], rel_ref[bkt, h], bias)
        o_ref[o_blk] = jnp.where(dist >= 0, (bias - far) * LOG2E, -jnp.inf)


def _bias_tables(rel_bias, T):
    starts = _bucket_starts()
    assert T + 1 >= starts[-1]
    return pl.pallas_call(
        functools.partial(_bias_table_kernel, starts=starts),
        grid=(DIFF_HEADS,),
        in_specs=[pl.BlockSpec(memory_space=pltpu.SMEM)],
        out_specs=pl.BlockSpec((None, 2, T, T), lambda h: (h, 0, 0, 0)),
        out_shape=jax.ShapeDtypeStruct((DIFF_HEADS, 2, T, T), F32),
        compiler_params=_cparams(("arbitrary",)),
        name="t5_bias_tables",
    )(rel_bias)


def _diff_kernel(q_ref, k_ref, v_ref, tb_ref, lq1_ref, lk1_ref, lq2_ref, lk2_ref, sg_ref, o_ref,
                 s_ref, m_ref, l_ref, acc_ref, *, lambda_init):
    T = q_ref.shape[0]
    n_streams = q_ref.shape[1] // LANES
    i = pl.program_id(2)
    qs = [_split_lane_halves(q_ref[:, h * LANES:(h + 1) * LANES]) for h in range(n_streams)]

    heads = range(n_streams)

    def scores(j_lo, nb):
        rows = pl.ds(pl.multiple_of(j_lo * T, T), nb * T)
        return [_scores_t(k_ref[rows, h * LANES:(h + 1) * LANES], qs[h]) for h in heads]

    def softmax_part(ss, table_rows):
        out = []
        for h in heads:
            read = ss[h] if callable(ss[h]) else (lambda v=ss[h]: v)
            if table_rows is not None:
                biases = [tb_ref[h, o] for o in table_rows]
                bias = biases[0] if len(biases) == 1 else jnp.concatenate(biases, axis=0)
                s = read() + jnp.concatenate([bias, bias], axis=1)
                read = lambda v=s: v
            m_old = m_ref[h]
            m_new = jnp.maximum(m_old, jnp.max(read(), axis=0, keepdims=True))
            alpha = jnp.exp2(m_old - m_new)
            p = jnp.exp2(read() - m_new)
            m_ref[h] = m_new
            l_ref[h] = alpha * l_ref[h] + jnp.sum(p, axis=0, keepdims=True)
            out.append((alpha, p.astype(BF16)))
        return out

    def accumulate(j_lo, nb, aps):
        rows = pl.ds(pl.multiple_of(j_lo * T, T), nb * T)
        for h in heads:
            alpha, p = aps[h]
            acc_ref[h] = alpha * acc_ref[h] + _pv_t(v_ref[rows, h * LANES:(h + 1) * LANES], p)

    for h in heads:
        m_ref[h] = jnp.full((1, 2 * T), -jnp.inf, F32)
        l_ref[h] = jnp.zeros((1, 2 * T), F32)
        acc_ref[h] = jnp.zeros((DIFF_V_DIM, 2 * T), F32)
    n_far = jnp.maximum(i - 1, 0)
    n_double = n_far // 2

    def produce(t, slot):
        for h, s in enumerate(scores(2 * t, 2)):
            s_ref[slot, h] = s

    def consume(t, slot, produce_next):
        aps = softmax_part([lambda h=h: s_ref[slot, h] for h in heads], None)
        if produce_next:
            produce(t + 1, 1 - slot)
        accumulate(2 * t, 2, aps)

    @pl.when(n_double > 0)
    def _():
        produce(0, 0)

    n_twice = jnp.maximum(n_double - 1, 0) // 2

    def body(u, _):
        consume(2 * u, 0, True)
        consume(2 * u + 1, 1, True)
        return 0

    lax.fori_loop(0, n_twice, body, 0)
    left = n_double - 2 * n_twice

    @pl.when(left == 2)
    def _():
        consume(2 * n_twice, 0, True)
        consume(2 * n_twice + 1, 1, False)

    @pl.when(left == 1)
    def _():
        consume(2 * n_twice, 0, False)

    @pl.when(n_far - 2 * n_double > 0)
    def _():
        accumulate(n_far - 1, 1, softmax_part(scores(n_far - 1, 1), None))

    @pl.when(i > 0)
    def _():
        accumulate(i - 1, 2, softmax_part(scores(i - 1, 2), [1, 0]))

    @pl.when(i == 0)
    def _():
        accumulate(0, 1, softmax_part(scores(0, 1), [0]))

    lam = (jnp.exp(jnp.sum(lq1_ref[...] * lk1_ref[...], axis=-1, keepdims=True))
           - jnp.exp(jnp.sum(lq2_ref[...] * lk2_ref[...], axis=-1, keepdims=True)) + lambda_init)
    for h in range(n_streams):
        l, acc = l_ref[h], acc_ref[h]
        o_t = acc[:, :T] / l[:, :T] - lam * (acc[:, T:] / l[:, T:])
        o = o_t.T
        ms = jnp.mean(o * o, axis=-1, keepdims=True)
        o = o * lax.rsqrt(ms + EPS) * sg_ref[...] * (1.0 - lambda_init)
        o_ref[:, h * LANES:(h + 1) * LANES] = o.astype(BF16)


def _diff_attention(q, k, v, tables, lq1, lk1, lq2, lk2, subln_g, lambda_init):
    b, s, width = q.shape
    T = tables.shape[-1]
    vec = lambda n: _resident((1, n), lambda bi, g, i: (0, 0))
    gw = DIFF_STREAMS * LANES
    return pl.pallas_call(
        functools.partial(_diff_kernel, lambda_init=lambda_init),
        grid=(b, width // gw, s // T),
        in_specs=[
            pl.BlockSpec((None, T, gw), lambda bi, g, i: (bi, i, g)),
            pl.BlockSpec((None, s, gw), lambda bi, g, i: (bi, 0, g)),
            pl.BlockSpec((None, s, gw), lambda bi, g, i: (bi, 0, g)),
            pl.BlockSpec((DIFF_STREAMS,) + tables.shape[1:], lambda bi, g, i: (g, 0, 0, 0)),
            vec(HEAD_DIM), vec(HEAD_DIM), vec(HEAD_DIM), vec(HEAD_DIM), vec(DIFF_V_DIM),
        ],
        out_specs=pl.BlockSpec((None, T, gw), lambda bi, g, i: (bi, i, g)),
        out_shape=jax.ShapeDtypeStruct((b, s, width), BF16),
        scratch_shapes=[
            pltpu.VMEM((2, DIFF_STREAMS, 2 * T, 2 * T), F32),
            pltpu.VMEM((DIFF_STREAMS, 1, 2 * T), F32),
            pltpu.VMEM((DIFF_STREAMS, 1, 2 * T), F32),
            pltpu.VMEM((DIFF_STREAMS, DIFF_V_DIM, 2 * T), F32),
        ],
        compiler_params=_cparams(("parallel", "parallel", "arbitrary")),
        name="differential_attention",
    )(q, k, v, tables, lq1.reshape(1, -1), lk1.reshape(1, -1), lq2.reshape(1, -1), lk2.reshape(1, -1),
      subln_g.reshape(1, -1))


def _outproj_kernel(x_ref, mod_ref, ys_ref, yb_ref, yd_ref, w_ref, o_ref):
    o1 = SSM_D_INNER
    o2 = o1 + SB_WIDTH
    y = jnp.dot(ys_ref[...], w_ref[0:o1, :].astype(BF16), preferred_element_type=F32)
    y = y + jnp.dot(yb_ref[...], w_ref[o1:o2, :].astype(BF16), preferred_element_type=F32)
    y = y + jnp.dot(yd_ref[...], w_ref[o2:, :].astype(BF16), preferred_element_type=F32)
    o_ref[...] = x_ref[...] + mod_ref[5:6, :] * y


def _outproj(x, mod, y_ssm, y_sb, y_diff, w_out, layer):
    b, s, d = x.shape
    tm = min(TOKEN_TILE, s)
    tok = lambda width: pl.BlockSpec((None, tm, width), lambda bi, i: (bi, i, 0))
    return pl.pallas_call(
        _outproj_kernel,
        grid=(b, s // tm),
        in_specs=[
            tok(d),
            pl.BlockSpec((None, N_MOD, d), lambda bi, i: (bi, 0, 0)),
            tok(SSM_D_INNER), tok(SB_WIDTH), tok(DIFF_WIDTH),
            _resident((None,) + w_out.shape[1:], lambda bi, i: (layer, 0, 0)),
        ],
        out_specs=tok(d),
        out_shape=jax.ShapeDtypeStruct((b, s, d), F32),
        compiler_params=_cparams(("parallel", "parallel")),
        name="mixer_outproj",
    )(x, mod, y_ssm, y_sb, y_diff, w_out)


def kernel(x, c, ada_w, ada_b, ffn1_norm, ffn1_w13, ffn1_w2, mix_norm, w_in, ssm_conv_w, ssm_conv_b, ssm_dt_bias, ssm_a_log, ssm_d, ssm_norm, diff_lambda_q1, diff_lambda_k1, diff_lambda_q2, diff_lambda_k2, diff_subln, rel_bias, w_out, ffn2_norm, ffn2_w13, ffn2_w2, final_norm):
    depth = ada_w.shape[0]
    s = x.shape[1]
    mods = _ada_modulation(c, ada_w, ada_b)
    tables = _bias_tables(rel_bias, min(ATT_BLOCK, s))
    w_in_packed = _pack_w_in(w_in)
    for l in range(depth):
        mod = mods[l]
        x = _ffn(x, mod, ffn1_norm[l], ffn1_w13, ffn1_w2, l, mod_row=0)
        lambda_init = 0.8 - 0.6 * math.exp(-0.3 * l)
        z, xbc, dt_raw, sq, sk, sv, dq, dk, dv = _inproj(x, mod, mix_norm[l], w_in_packed, l)
        y_ssm = _ssd(z, xbc, dt_raw, ssm_conv_w[l], ssm_conv_b[l], ssm_dt_bias[l], ssm_a_log[l], ssm_d[l],
                     ssm_norm[l])
        y_sb = _stick_breaking(sq, sk, sv)
        y_diff = _diff_attention(dq, dk, dv, tables, diff_lambda_q1[l], diff_lambda_k1[l], diff_lambda_q2[l],
                                 diff_lambda_k2[l], diff_subln[l], lambda_init)
        x = _outproj(x, mod, y_ssm, y_sb, y_diff, w_out, l)
        x = _ffn(x, mod, ffn2_norm[l], ffn2_w13, ffn2_w2, l, mod_row=6,
                 final_g=final_norm if l == depth - 1 else None)
    return x
```

```python
import functools
import math

import numpy as np
import jax
import jax.numpy as jnp
from jax import lax
from jax.experimental import pallas as pl
from jax.experimental.pallas import tpu as pltpu

F32 = jnp.float32
BF16 = jnp.bfloat16
HIGHEST = lax.Precision.HIGHEST
LOG2E = math.log2(math.e)

HEAD_DIM = 64
SSM_HEADS = 16
SSM_GROUPS = 2
SSM_STATE = 128
SSM_CONV = 4
SSM_CHUNK = 128
SSM_D_INNER = SSM_HEADS * HEAD_DIM
SSM_CONV_DIM = SSM_D_INNER + 2 * SSM_GROUPS * SSM_STATE
SB_HEADS = 8
SB_WIDTH = SB_HEADS * HEAD_DIM
DIFF_HEADS = 4
DIFF_V_DIM = 2 * HEAD_DIM
DIFF_WIDTH = DIFF_HEADS * DIFF_V_DIM
N_MOD = 9
N_REL_BUCKETS = 32
REL_MAX_DIST = 128
EPS = 1e-6

LANES = 128
BF16_SUBLANES = 16
VMEM_LIMIT = 56 * 1024 * 1024

TOKEN_TILE = 512
FF_CHUNK = 256
ATT_BLOCK = 256

OFF_Z = 0
OFF_XBC = OFF_Z + SSM_D_INNER
OFF_SQ = OFF_XBC + SSM_CONV_DIM
OFF_SK = OFF_SQ + SB_WIDTH
OFF_SV = OFF_SK + SB_WIDTH
OFF_DQ = OFF_SV + SB_WIDTH
OFF_DK = OFF_DQ + DIFF_WIDTH
OFF_DV = OFF_DK + DIFF_WIDTH
OFF_DT = OFF_DV + DIFF_WIDTH
IN_PACKED = OFF_DT + LANES


def _cparams(semantics):
    return pltpu.CompilerParams(dimension_semantics=semantics, vmem_limit_bytes=VMEM_LIMIT)


SSD_STREAMS = 2
SB_STREAMS = 4
DIFF_STREAMS = 4


def _resident(block_shape, index_map):
    return pl.BlockSpec(block_shape, index_map, pipeline_mode=pl.Buffered(1))


def _silu(x):
    hx = 0.5 * x
    return hx + hx * jnp.tanh(hx)


def _softplus(x):
    return jnp.maximum(x, 0.0) + jnp.log1p(jnp.exp(-jnp.abs(x)))


def _norm_modulate(x, g, shift, scale):
    ms = jnp.mean(x * x, axis=-1, keepdims=True)
    y = x * lax.rsqrt(ms + EPS) * g
    return y * (1.0 + scale) + shift


def _ada_kernel(c_ref, w_ref, b_ref, o_ref):
    cond = _silu(c_ref[...])
    o_ref[...] = jnp.dot(cond, w_ref[...], preferred_element_type=F32, precision=HIGHEST) + b_ref[...]


def _ada_modulation(c, ada_w, ada_b):
    depth, d, nmod = ada_w.shape
    b = c.shape[0]
    rows = 8 * pl.cdiv(b, 8)
    c_pad = jnp.zeros((rows, d), F32).at[:b].set(c)
    out = pl.pallas_call(
        _ada_kernel,
        grid=(depth, nmod // d),
        in_specs=[
            pl.BlockSpec((rows, d), lambda l, j: (0, 0)),
            pl.BlockSpec((None, d, d), lambda l, j: (l, 0, j)),
            pl.BlockSpec((None, 1, d), lambda l, j: (l, 0, j)),
        ],
        out_specs=pl.BlockSpec((None, rows, d), lambda l, j: (l, 0, j)),
        out_shape=jax.ShapeDtypeStruct((depth, rows, nmod), F32),
        compiler_params=_cparams(("arbitrary", "arbitrary")),
        name="ada_modulation",
    )(c_pad, ada_w, ada_b.reshape(depth, 1, nmod))
    return out[:, :b].reshape(depth, b, N_MOD, d)


def _ffn_kernel(x_ref, mod_ref, g_ref, w13_ref, w2_ref, *rest, mod_row, final_norm):
    if final_norm:
        fg_ref, o_ref, acc_ref = rest
    else:
        o_ref, acc_ref = rest
    x = x_ref[...]
    shift = mod_ref[mod_row:mod_row + 1, :]
    scale = mod_ref[mod_row + 1:mod_row + 2, :]
    gate = mod_ref[mod_row + 2:mod_row + 3, :]
    h = _norm_modulate(x, g_ref[...], shift, scale).astype(BF16)
    d_ff = w2_ref.shape[0]
    for j in range(d_ff // FF_CHUNK):
        cols = slice(j * FF_CHUNK, (j + 1) * FF_CHUNK)
        w1 = w13_ref[:, cols].astype(BF16)
        w3 = w13_ref[:, d_ff + j * FF_CHUNK:d_ff + (j + 1) * FF_CHUNK].astype(BF16)
        a = jnp.dot(h, w1, preferred_element_type=F32)
        u = jnp.dot(h, w3, preferred_element_type=F32)
        act = (_silu(a) * u).astype(BF16)
        part = jnp.dot(act, w2_ref[cols, :].astype(BF16), preferred_element_type=F32)
        if j == 0:
            acc_ref[...] = part
        else:
            acc_ref[...] += part
    y = x + (0.5 * gate) * acc_ref[...]
    if final_norm:
        ms = jnp.mean(y * y, axis=-1, keepdims=True)
        y = y * lax.rsqrt(ms + EPS) * fg_ref[...]
    o_ref[...] = y


def _ffn(x, mod, g, w13, w2, layer, mod_row, final_g=None):
    b, s, d = x.shape
    d_ff = w2.shape[1]
    assert d_ff % FF_CHUNK == 0 and d_ff % LANES == 0
    tm = min(TOKEN_TILE, s)
    final_norm = final_g is not None
    in_specs = [
        pl.BlockSpec((None, tm, d), lambda bi, i: (bi, i, 0)),
        pl.BlockSpec((None, N_MOD, d), lambda bi, i: (bi, 0, 0)),
        _resident((1, d), lambda bi, i: (0, 0)),
        _resident((None, d, 2 * d_ff), lambda bi, i: (layer, 0, 0)),
        _resident((None, d_ff, d), lambda bi, i: (layer, 0, 0)),
    ]
    args = [x, mod, g.reshape(1, d), w13, w2]
    if final_norm:
        in_specs.append(_resident((1, d), lambda bi, i: (0, 0)))
        args.append(final_g.reshape(1, d))
    return pl.pallas_call(
        functools.partial(_ffn_kernel, mod_row=mod_row, final_norm=final_norm),
        grid=(b, s // tm),
        in_specs=in_specs,
        out_specs=pl.BlockSpec((None, tm, d), lambda bi, i: (bi, i, 0)),
        out_shape=jax.ShapeDtypeStruct((b, s, d), F32),
        scratch_shapes=[pltpu.VMEM((tm, d), F32)],
        compiler_params=_cparams(("parallel", "parallel")),
        name="ffn_final" if final_norm else "ffn",
    )(*args)


def _inproj_kernel(x_ref, mod_ref, g_ref, w_ref, z_ref, xbc_ref, dt_ref,
                   sq_ref, sk_ref, sv_ref, dq_ref, dk_ref, dv_ref):
    x = x_ref[...]
    h = _norm_modulate(x, g_ref[...], mod_ref[3:4, :], mod_ref[4:5, :]).astype(BF16)

    def proj(off, width):
        return jnp.dot(h, w_ref[:, off:off + width], preferred_element_type=F32)

    qk_scale = HEAD_DIM ** -0.5 * LOG2E
    z_ref[...] = proj(OFF_Z, SSM_D_INNER).astype(BF16)
    xbc_ref[...] = proj(OFF_XBC, SSM_CONV_DIM).astype(BF16)
    dt_ref[...] = proj(OFF_DT, LANES)
    sq_ref[...] = (proj(OFF_SQ, SB_WIDTH) * qk_scale).astype(BF16)
    sk_ref[...] = proj(OFF_SK, SB_WIDTH).astype(BF16)
    sv_ref[...] = proj(OFF_SV, SB_WIDTH).astype(BF16)
    dq_ref[...] = (proj(OFF_DQ, DIFF_WIDTH) * qk_scale).astype(BF16)
    dk_ref[...] = proj(OFF_DK, DIFF_WIDTH).astype(BF16)
    dv_ref[...] = proj(OFF_DV, DIFF_WIDTH).astype(BF16)


def _pack_w_in(w_in):
    depth, d, _ = w_in.shape
    dt0 = SSM_D_INNER + SSM_CONV_DIM
    dt1 = dt0 + SSM_HEADS
    packed = jnp.concatenate(
        [w_in[..., :dt0], w_in[..., dt1:], w_in[..., dt0:dt1],
         jnp.zeros((depth, d, LANES - SSM_HEADS), w_in.dtype)], axis=-1)
    return packed.astype(BF16)


def _inproj(x, mod, g, w, layer):
    b, s, d = x.shape
    tm = min(TOKEN_TILE, s)
    widths = (SSM_D_INNER, SSM_CONV_DIM, LANES) + (SB_WIDTH,) * 3 + (DIFF_WIDTH,) * 3
    dtypes = (BF16, BF16, F32) + (BF16,) * 6
    return pl.pallas_call(
        _inproj_kernel,
        grid=(b, s // tm),
        in_specs=[
            pl.BlockSpec((None, tm, d), lambda bi, i: (bi, i, 0)),
            pl.BlockSpec((None, N_MOD, d), lambda bi, i: (bi, 0, 0)),
            _resident((1, d), lambda bi, i: (0, 0)),
            _resident((None, d, IN_PACKED), lambda bi, i: (layer, 0, 0)),
        ],
        out_specs=[pl.BlockSpec((None, tm, wd), lambda bi, i: (bi, i, 0)) for wd in widths],
        out_shape=[jax.ShapeDtypeStruct((b, s, wd), dt) for wd, dt in zip(widths, dtypes)],
        compiler_params=_cparams(("parallel", "parallel")),
        name="mixer_inproj",
    )(x, mod, g.reshape(1, d), w)


def _ssd_kernel(z_ref, xbc_ref, dtr_ref, cw_ref, cb_ref, dtb_ref, alog_ref, dsk_ref, ng_ref, e_ref,
                y_ref, tail_ref, state_ref):
    for bb in range(z_ref.shape[0]):
        _ssd_chunk(z_ref.at[bb], xbc_ref.at[bb], dtr_ref.at[bb], cw_ref, cb_ref, dtb_ref, alog_ref, dsk_ref,
                   ng_ref, e_ref, y_ref.at[bb], tail_ref.at[bb], state_ref.at[bb])


def _ssd_chunk(z_ref, xbc_ref, dtr_ref, cw_ref, cb_ref, dtb_ref, alog_ref, dsk_ref, ng_ref, e_ref,
               y_ref, tail_ref, state_ref):
    L = z_ref.shape[0]
    pad = tail_ref.shape[0]
    c = pl.program_id(1)

    @pl.when(c == 0)
    def _():
        tail_ref[...] = jnp.zeros(tail_ref.shape, BF16)
        state_ref[...] = jnp.zeros(state_ref.shape, F32)

    u = xbc_ref[...]
    u_ext = jnp.concatenate([tail_ref[...], u], axis=0)
    tail_ref[...] = u[L - pad:, :]
    trow = lax.broadcasted_iota(jnp.int32, (L, pad + L), 0)
    tcol = lax.broadcasted_iota(jnp.int32, (L, pad + L), 1)
    shifts = jnp.concatenate(
        [jnp.where(tcol == trow + (pad - (SSM_CONV - 1) + k), 1.0, 0.0) for k in range(SSM_CONV)], axis=0)
    shifted = jnp.dot(shifts.astype(BF16), u_ext, preferred_element_type=F32)
    conv = cb_ref[...]
    for k in range(SSM_CONV):
        conv = conv + cw_ref[k:k + 1, :] * shifted[k * L:(k + 1) * L]
    xa = _silu(conv)
    xs = xa[:, :SSM_D_INNER]
    bmat = xa[:, SSM_D_INNER:SSM_D_INNER + SSM_GROUPS * SSM_STATE]
    cmat = xa[:, SSM_D_INNER + SSM_GROUPS * SSM_STATE:]

    dt = _softplus(dtr_ref[...] + dtb_ref[...])
    a = -jnp.exp(alog_ref[...])
    da = dt * a
    row = lax.broadcasted_iota(jnp.int32, (L, L), 0)
    col = lax.broadcasted_iota(jnp.int32, (L, L), 1)
    causal = row >= col
    a_cum = jnp.dot(causal.astype(F32), da, preferred_element_type=F32, precision=HIGHEST)
    a_cum_t = a_cum.T
    exp_a = jnp.exp(a_cum)
    decay_to_end = jnp.exp(a_cum[L - 1:L, :] - a_cum)

    stacked = jnp.concatenate([dt, exp_a, decay_to_end], axis=0)
    hi = stacked.astype(BF16)
    lo = (stacked - hi.astype(F32)).astype(BF16)
    spread = jnp.dot(jnp.concatenate([hi, lo], axis=1), e_ref[...], preferred_element_type=F32)
    dt_e = spread[0:L]
    exp_a_e = spread[L:2 * L]
    dte_e = spread[2 * L:3 * L]

    xd = xs * dt_e
    xd_b = xd.astype(BF16)
    xdd_b = (xd * dte_e).astype(BF16)
    prev_b = state_ref[...].astype(BF16)
    lane = lax.broadcasted_iota(jnp.int32, (L, LANES), 1)
    heads_per_group = SSM_HEADS // SSM_GROUPS
    group_width = heads_per_group * HEAD_DIM

    y_diag_parts = []
    y_off_parts = []
    s_new_parts = []
    for g in range(SSM_GROUPS):
        bg = bmat[:, g * SSM_STATE:(g + 1) * SSM_STATE]
        cg_b = cmat[:, g * SSM_STATE:(g + 1) * SSM_STATE].astype(BF16)
        bg_b = bg.astype(BF16)
        bg_t_b = bg.T.astype(BF16)
        gs = slice(g * group_width, (g + 1) * group_width)
        cb = lax.dot_general(cg_b, bg_b, (((1,), (1,)), ((), ())), preferred_element_type=F32)
        y_off_parts.append(jnp.dot(cg_b, prev_b[:, gs], preferred_element_type=F32))
        s_new_parts.append(jnp.dot(bg_t_b, xdd_b[:, gs], preferred_element_type=F32))
        for hp in range(heads_per_group // 2):
            ms = []
            for h in (g * heads_per_group + 2 * hp, g * heads_per_group + 2 * hp + 1):
                seg = a_cum[:, h:h + 1] - a_cum_t[h:h + 1, :]
                decay = jnp.exp(jnp.where(causal, seg, -jnp.inf))
                ms.append((cb * decay).astype(BF16))
            pair = (g * heads_per_group) // 2 + hp
            yp = jnp.dot(jnp.concatenate(ms, axis=0), xd_b[:, pair * LANES:(pair + 1) * LANES],
                         preferred_element_type=F32)
            y_diag_parts.append(jnp.where(lane < HEAD_DIM, yp[:L], yp[L:]))
    y_diag = jnp.concatenate(y_diag_parts, axis=1)
    y_off = jnp.concatenate(y_off_parts, axis=1)
    s_new = jnp.concatenate(s_new_parts, axis=1)

    y = y_diag + y_off * exp_a_e + dsk_ref[...] * xs
    state_ref[...] = state_ref[...] * exp_a_e[L - 1:L, :] + s_new

    y = y * _silu(z_ref[...].astype(F32))
    outs = []
    for g in range(SSM_GROUPS):
        yg = y[:, g * group_width:(g + 1) * group_width]
        ms = jnp.mean(yg * yg, axis=-1, keepdims=True)
        outs.append(yg * lax.rsqrt(ms + EPS))
    y_ref[...] = (jnp.concatenate(outs, axis=1) * ng_ref[...]).astype(BF16)


def _head_spread_matrix():
    e = np.zeros((2 * LANES, SSM_D_INNER), np.float32)
    for h in range(SSM_HEADS):
        e[h, h * HEAD_DIM:(h + 1) * HEAD_DIM] = 1.0
        e[LANES + h, h * HEAD_DIM:(h + 1) * HEAD_DIM] = 1.0
    return jnp.asarray(e, BF16)


def _pad_lanes(v):
    return jnp.zeros((1, LANES), F32).at[0, :v.shape[0]].set(v)


def _ssd(z, xbc, dt_raw, conv_w, conv_b, dt_bias, a_log, d_skip, norm_g):
    b, s, _ = z.shape
    L = SSM_CHUNK
    nb = SSD_STREAMS if b % SSD_STREAMS == 0 else 1
    row = lambda width: _resident((1, width), lambda bi, c: (0, 0))
    return pl.pallas_call(
        _ssd_kernel,
        grid=(b // nb, s // L),
        in_specs=[
            pl.BlockSpec((nb, L, SSM_D_INNER), lambda bi, c: (bi, c, 0)),
            pl.BlockSpec((nb, L, SSM_CONV_DIM), lambda bi, c: (bi, c, 0)),
            pl.BlockSpec((nb, L, LANES), lambda bi, c: (bi, c, 0)),
            _resident((SSM_CONV, SSM_CONV_DIM), lambda bi, c: (0, 0)),
            row(SSM_CONV_DIM), row(LANES), row(LANES), row(SSM_D_INNER), row(SSM_D_INNER),
            _resident((2 * LANES, SSM_D_INNER), lambda bi, c: (0, 0)),
        ],
        out_specs=pl.BlockSpec((nb, L, SSM_D_INNER), lambda bi, c: (bi, c, 0)),
        out_shape=jax.ShapeDtypeStruct((b, s, SSM_D_INNER), BF16),
        scratch_shapes=[pltpu.VMEM((nb, BF16_SUBLANES, SSM_CONV_DIM), BF16),
                        pltpu.VMEM((nb, SSM_STATE, SSM_D_INNER), F32)],
        compiler_params=_cparams(("parallel", "arbitrary")),
        name="ssd_mixer",
    )(z, xbc, dt_raw, conv_w.T, conv_b.reshape(1, -1), _pad_lanes(dt_bias), _pad_lanes(a_log),
      jnp.repeat(d_skip, HEAD_DIM).reshape(1, -1), norm_g.reshape(1, -1), _head_spread_matrix())


def _split_lane_halves(x2):
    xf = x2.astype(F32)
    lane = lax.broadcasted_iota(jnp.int32, xf.shape, 1)
    lo = jnp.where(lane < HEAD_DIM, xf, 0.0)
    hi = jnp.where(lane >= HEAD_DIM, xf, 0.0)
    return jnp.concatenate([lo, hi], axis=0).astype(BF16)


def _scores_t(kb, qs):
    return lax.dot_general(kb, qs, (((1,), (1,)), ((), ())), preferred_element_type=F32)


def _pv_t(vb, wt):
    return lax.dot_general(vb, wt, (((0,), (0,)), ((), ())), preferred_element_type=F32)


def _sb_kernel(q_ref, k_ref, v_ref, o_ref, z_ref, cs_ref, tot_ref, acc_ref, carry_ref):
    T = q_ref.shape[0]
    n_streams = q_ref.shape[1] // LANES
    i = pl.program_id(2)
    qs = [_split_lane_halves(q_ref[:, p * LANES:(p + 1) * LANES]) for p in range(n_streams)]

    key = lax.broadcasted_iota(jnp.int32, (T, 2 * T), 0)
    qry = lax.broadcasted_iota(jnp.int32, (T, 2 * T), 1)
    strict = key < jnp.where(qry >= T, qry - T, qry)
    ur = lax.broadcasted_iota(jnp.int32, (T, T), 0)
    uc = lax.broadcasted_iota(jnp.int32, (T, T), 1)
    neg_suffix = jnp.where(uc >= ur, -1.0, 0.0).astype(BF16)
    streams = range(n_streams)
    sign_bit = jnp.uint32(0x80000000)

    def scores(j_lo, nb):
        rows = pl.ds(pl.multiple_of(j_lo * T, T), nb * T)
        return [_scores_t(k_ref[rows, p * LANES:(p + 1) * LANES], qs[p]) for p in streams]

    def softplus2(z, diagonal):
        neg_abs = lax.bitcast_convert_type(lax.bitcast_convert_type(z, jnp.uint32) | sign_bit, F32)
        sp = jnp.maximum(z, 0.0) + jnp.log(1.0 + jnp.exp2(neg_abs)) * LOG2E
        if diagonal:
            sp = jnp.where(strict, sp, 0.0)
        return sp.astype(BF16)

    def suffix_sums(sp, nb):
        parts = [jnp.dot(neg_suffix, sp[u * T:(u + 1) * T], preferred_element_type=F32) for u in range(nb)]
        total = parts[nb - 1][0:1, :]
        for u in reversed(range(nb - 1)):
            parts[u] = parts[u] + total
            total = parts[u][0:1, :]
        return (parts[0] if nb == 1 else jnp.concatenate(parts, axis=0)), total

    def weights(z, cs, carry, diagonal):
        w = jnp.exp2(z + (cs + carry))
        if diagonal:
            w = jnp.where(strict, w, 0.0)
        return w.astype(BF16)

    def accumulate(j_lo, nb, ws, tots):
        rows = pl.ds(pl.multiple_of(j_lo * T, T), nb * T)
        for p in streams:
            acc_ref[p] += _pv_t(v_ref[rows, p * LANES:(p + 1) * LANES], ws[p])
            carry_ref[p] += tots[p]

    def direct_step(j_lo, nb, diagonal=False):
        zs = scores(j_lo, nb)
        cts = [suffix_sums(softplus2(z, diagonal), nb) for z in zs]
        ws = [weights(zs[p], cts[p][0], carry_ref[p], diagonal) for p in streams]
        accumulate(j_lo, nb, ws, [ct[1] for ct in cts])

    def finish_produce(slot, zs):
        for p in streams:
            z_ref[slot, p] = zs[p]
        for p in streams:
            cs, tot = suffix_sums(softplus2(z_ref[slot, p], False), 2)
            cs_ref[slot, p] = cs
            tot_ref[slot, p] = tot

    for p in streams:
        acc_ref[p] = jnp.zeros((LANES, 2 * T), F32)
        carry_ref[p] = jnp.zeros((1, 2 * T), F32)
    direct_step(i, 1, diagonal=True)
    n_double = i // 2
    pair_lo = lambda t: i - 2 - 2 * t

    def consume(t, slot, produce_next):
        ws = [weights(z_ref[slot, p], cs_ref[slot, p], carry_ref[p], False) for p in streams]
        tots = [tot_ref[slot, p] for p in streams]
        if produce_next:
            zs_next = scores(pair_lo(t + 1), 2)
        accumulate(pair_lo(t), 2, ws, tots)
        if produce_next:
            finish_produce(1 - slot, zs_next)

    @pl.when(n_double > 0)
    def _():
        finish_produce(0, scores(pair_lo(0), 2))

    n_twice = jnp.maximum(n_double - 1, 0) // 2

    def body(u, _):
        consume(2 * u, 0, True)
        consume(2 * u + 1, 1, True)
        return 0

    lax.fori_loop(0, n_twice, body, 0)
    left = n_double - 2 * n_twice

    @pl.when(left == 2)
    def _():
        consume(2 * n_twice, 0, True)
        consume(2 * n_twice + 1, 1, False)

    @pl.when(left == 1)
    def _():
        consume(2 * n_twice, 0, False)

    @pl.when(i - 2 * n_double > 0)
    def _():
        direct_step(0, 1)

    for p in range(n_streams):
        acc = acc_ref[p]
        pair_t = jnp.concatenate([acc[:HEAD_DIM, :T], acc[HEAD_DIM:, T:]], axis=0)
        o_ref[:, p * LANES:(p + 1) * LANES] = pair_t.T.astype(BF16)


def _stick_breaking(q, k, v):
    b, s, width = q.shape
    T = min(ATT_BLOCK, s)
    gw = SB_STREAMS * LANES
    return pl.pallas_call(
        _sb_kernel,
        grid=(b, width // gw, s // T),
        in_specs=[
            pl.BlockSpec((None, T, gw), lambda bi, g, i: (bi, i, g)),
            pl.BlockSpec((None, s, gw), lambda bi, g, i: (bi, 0, g)),
            pl.BlockSpec((None, s, gw), lambda bi, g, i: (bi, 0, g)),
        ],
        out_specs=pl.BlockSpec((None, T, gw), lambda bi, g, i: (bi, i, g)),
        out_shape=jax.ShapeDtypeStruct((b, s, width), BF16),
        scratch_shapes=[
            pltpu.VMEM((2, SB_STREAMS, 2 * T, 2 * T), F32),
            pltpu.VMEM((2, SB_STREAMS, 2 * T, 2 * T), F32),
            pltpu.VMEM((2, SB_STREAMS, 1, 2 * T), F32),
            pltpu.VMEM((SB_STREAMS, LANES, 2 * T), F32),
            pltpu.VMEM((SB_STREAMS, 1, 2 * T), F32),
        ],
        compiler_params=_cparams(("parallel", "parallel", "arbitrary")),
        name="stick_breaking_attention",
    )(q, k, v)


def _bucket_starts():
    max_exact = N_REL_BUCKETS // 2
    dist = np.arange(0, 4 * REL_MAX_DIST, dtype=np.float64)
    ratio = np.log(np.maximum(dist, max_exact) / max_exact) / math.log(REL_MAX_DIST / max_exact)
    scaled = ratio * (N_REL_BUCKETS - max_exact)
    frac = np.abs(scaled - np.round(scaled))
    interior = (dist > max_exact) & (dist < REL_MAX_DIST)
    assert frac[interior].min() > 1e-3, "a bucket edge sits on an integer distance"
    large = np.minimum(max_exact + np.floor(scaled + 1e-9).astype(np.int64), N_REL_BUCKETS - 1)
    bucket = np.where(dist < max_exact, dist.astype(np.int64), large)
    assert np.all(np.diff(bucket) >= 0) and bucket[-1] == N_REL_BUCKETS - 1
    return [int(np.argmax(bucket >= bkt)) for bkt in range(N_REL_BUCKETS)]


def _bias_table_kernel(rel_ref, o_ref, *, starts):
    h = pl.program_id(0)
    T = o_ref.shape[-1]
    key = lax.broadcasted_iota(jnp.int32, (T, T), 0)
    qry = lax.broadcasted_iota(jnp.int32, (T, T), 1)
    far = rel_ref[N_REL_BUCKETS - 1, h]
    for o_blk in range(o_ref.shape[0]):
        dist = qry - key + o_blk * T
        bias = jnp.full((T, T), rel_ref[0, h], F32)
        for bkt in range(1, N_REL_BUCKETS):
            bias = jnp.where(dist >= starts[bkt], rel_ref[bkt, h], bias)
        o_ref[o_blk] = jnp.where(dist >= 0, (bias - far) * LOG2E, -jnp.inf)


def _bias_tables(rel_bias, T):
    starts = _bucket_starts()
    assert T + 1 >= starts[-1]
    return pl.pallas_call(
        functools.partial(_bias_table_kernel, starts=starts),
        grid=(DIFF_HEADS,),
        in_specs=[pl.BlockSpec(memory_space=pltpu.SMEM)],
        out_specs=pl.BlockSpec((None, 2, T, T), lambda h: (h, 0, 0, 0)),
        out_shape=jax.ShapeDtypeStruct((DIFF_HEADS, 2, T, T), F32),
        compiler_params=_cparams(("arbitrary",)),
        name="t5_bias_tables",
    )(rel_bias)


def _diff_kernel(q_ref, k_ref, v_ref, tb_ref, lq1_ref, lk1_ref, lq2_ref, lk2_ref, sg_ref, o_ref,
                 s_ref, m_ref, l_ref, acc_ref, *, lambda_init):
    T = q_ref.shape[0]
    n_streams = q_ref.shape[1] // LANES
    i = pl.program_id(2)
    qs = [_split_lane_halves(q_ref[:, h * LANES:(h + 1) * LANES]) for h in range(n_streams)]

    heads = range(n_streams)

    def scores(j_lo, nb):
        rows = pl.ds(pl.multiple_of(j_lo * T, T), nb * T)
        return [_scores_t(k_ref[rows, h * LANES:(h + 1) * LANES], qs[h]) for h in heads]

    def softmax_part(ss, table_rows):
        out = []
        for h in heads:
            read = ss[h] if callable(ss[h]) else (lambda v=ss[h]: v)
            if table_rows is not None:
                biases = [tb_ref[h, o] for o in table_rows]
                bias = biases[0] if len(biases) == 1 else jnp.concatenate(biases, axis=0)
                s = read() + jnp.concatenate([bias, bias], axis=1)
                read = lambda v=s: v
            m_old = m_ref[h]
            m_new = jnp.maximum(m_old, jnp.max(read(), axis=0, keepdims=True))
            alpha = jnp.exp2(m_old - m_new)
            p = jnp.exp2(read() - m_new)
            m_ref[h] = m_new
            l_ref[h] = alpha * l_ref[h] + jnp.sum(p, axis=0, keepdims=True)
            out.append((alpha, p.astype(BF16)))
        return out

    def accumulate(j_lo, nb, aps):
        rows = pl.ds(pl.multiple_of(j_lo * T, T), nb * T)
        for h in heads:
            alpha, p = aps[h]
            acc_ref[h] = alpha * acc_ref[h] + _pv_t(v_ref[rows, h * LANES:(h + 1) * LANES], p)

    for h in heads:
        m_ref[h] = jnp.full((1, 2 * T), -jnp.inf, F32)
        l_ref[h] = jnp.zeros((1, 2 * T), F32)
        acc_ref[h] = jnp.zeros((DIFF_V_DIM, 2 * T), F32)
    n_far = jnp.maximum(i - 1, 0)
    n_double = n_far // 2
    first = n_far - 2 * n_double
    pair_lo = lambda t: first + 2 * t

    def produce(t, slot):
        for h, s in enumerate(scores(pair_lo(t), 2)):
            s_ref[slot, h] = s

    def consume(t, slot, produce_next, table_rows=None):
        aps = softmax_part([lambda h=h: s_ref[slot, h] for h in heads], table_rows)
        if produce_next:
            produce(t + 1, 1 - slot)
        accumulate(pair_lo(t), 2, aps)

    @pl.when(first > 0)
    def _():
        accumulate(0, 1, softmax_part(scores(0, 1), None))

    @pl.when(i > 0)
    def _():
        produce(0, 0)

    n_twice = n_double // 2

    def body(u, _):
        consume(2 * u, 0, True)
        consume(2 * u + 1, 1, True)
        return 0

    lax.fori_loop(0, n_twice, body, 0)
    left = n_double - 2 * n_twice

    @pl.when(jnp.logical_and(i > 0, left == 1))
    def _():
        consume(2 * n_twice, 0, True)
        consume(2 * n_twice + 1, 1, False, [1, 0])

    @pl.when(jnp.logical_and(i > 0, left == 0))
    def _():
        consume(2 * n_twice, 0, False, [1, 0])

    @pl.when(i == 0)
    def _():
        accumulate(0, 1, softmax_part(scores(0, 1), [0]))

    lam = (jnp.exp(jnp.sum(lq1_ref[...] * lk1_ref[...], axis=-1, keepdims=True))
           - jnp.exp(jnp.sum(lq2_ref[...] * lk2_ref[...], axis=-1, keepdims=True)) + lambda_init)
    for h in range(n_streams):
        l, acc = l_ref[h], acc_ref[h]
        o_t = acc[:, :T] / l[:, :T] - lam * (acc[:, T:] / l[:, T:])
        o = o_t.T
        ms = jnp.mean(o * o, axis=-1, keepdims=True)
        o = o * lax.rsqrt(ms + EPS) * sg_ref[...] * (1.0 - lambda_init)
        o_ref[:, h * LANES:(h + 1) * LANES] = o.astype(BF16)


def _diff_attention(q, k, v, tables, lq1, lk1, lq2, lk2, subln_g, lambda_init):
    b, s, width = q.shape
    T = tables.shape[-1]
    vec = lambda n: _resident((1, n), lambda bi, g, i: (0, 0))
    gw = DIFF_STREAMS * LANES
    return pl.pallas_call(
        functools.partial(_diff_kernel, lambda_init=lambda_init),
        grid=(b, width // gw, s // T),
        in_specs=[
            pl.BlockSpec((None, T, gw), lambda bi, g, i: (bi, i, g)),
            pl.BlockSpec((None, s, gw), lambda bi, g, i: (bi, 0, g)),
            pl.BlockSpec((None, s, gw), lambda bi, g, i: (bi, 0, g)),
            pl.BlockSpec((DIFF_STREAMS,) + tables.shape[1:], lambda bi, g, i: (g, 0, 0, 0)),
            vec(HEAD_DIM), vec(HEAD_DIM), vec(HEAD_DIM), vec(HEAD_DIM), vec(DIFF_V_DIM),
        ],
        out_specs=pl.BlockSpec((None, T, gw), lambda bi, g, i: (bi, i, g)),
        out_shape=jax.ShapeDtypeStruct((b, s, width), BF16),
        scratch_shapes=[
            pltpu.VMEM((2, DIFF_STREAMS, 2 * T, 2 * T), F32),
            pltpu.VMEM((DIFF_STREAMS, 1, 2 * T), F32),
            pltpu.VMEM((DIFF_STREAMS, 1, 2 * T), F32),
            pltpu.VMEM((DIFF_STREAMS, DIFF_V_DIM, 2 * T), F32),
        ],
        compiler_params=_cparams(("parallel", "parallel", "arbitrary")),
        name="differential_attention",
    )(q, k, v, tables, lq1.reshape(1, -1), lk1.reshape(1, -1), lq2.reshape(1, -1), lk2.reshape(1, -1),
      subln_g.reshape(1, -1))


def _outproj_kernel(x_ref, mod_ref, ys_ref, yb_ref, yd_ref, w_ref, o_ref):
    o1 = SSM_D_INNER
    o2 = o1 + SB_WIDTH
    y = jnp.dot(ys_ref[...], w_ref[0:o1, :].astype(BF16), preferred_element_type=F32)
    y = y + jnp.dot(yb_ref[...], w_ref[o1:o2, :].astype(BF16), preferred_element_type=F32)
    y = y + jnp.dot(yd_ref[...], w_ref[o2:, :].astype(BF16), preferred_element_type=F32)
    o_ref[...] = x_ref[...] + mod_ref[5:6, :] * y


def _outproj(x, mod, y_ssm, y_sb, y_diff, w_out, layer):
    b, s, d = x.shape
    tm = min(TOKEN_TILE, s)
    tok = lambda width: pl.BlockSpec((None, tm, width), lambda bi, i: (bi, i, 0))
    return pl.pallas_call(
        _outproj_kernel,
        grid=(b, s // tm),
        in_specs=[
            tok(d),
            pl.BlockSpec((None, N_MOD, d), lambda bi, i: (bi, 0, 0)),
            tok(SSM_D_INNER), tok(SB_WIDTH), tok(DIFF_WIDTH),
            _resident((None,) + w_out.shape[1:], lambda bi, i: (layer, 0, 0)),
        ],
        out_specs=tok(d),
        out_shape=jax.ShapeDtypeStruct((b, s, d), F32),
        compiler_params=_cparams(("parallel", "parallel")),
        name="mixer_outproj",
    )(x, mod, y_ssm, y_sb, y_diff, w_out)


def kernel(x, c, ada_w, ada_b, ffn1_norm, ffn1_w13, ffn1_w2, mix_norm, w_in, ssm_conv_w, ssm_conv_b, ssm_dt_bias, ssm_a_log, ssm_d, ssm_norm, diff_lambda_q1, diff_lambda_k1, diff_lambda_q2, diff_lambda_k2, diff_subln, rel_bias, w_out, ffn2_norm, ffn2_w13, ffn2_w2, final_norm):
    depth = ada_w.shape[0]
    s = x.shape[1]
    mods = _ada_modulation(c, ada_w, ada_b)
    tables = _bias_tables(rel_bias, min(ATT_BLOCK, s))
    w_in_packed = _pack_w_in(w_in)
    for l in range(depth):
        mod = mods[l]
        x = _ffn(x, mod, ffn1_norm[l], ffn1_w13, ffn1_w2, l, mod_row=0)
        lambda_init = 0.8 - 0.6 * math.exp(-0.3 * l)
        z, xbc, dt_raw, sq, sk, sv, dq, dk, dv = _inproj(x, mod, mix_norm[l], w_in_packed, l)
        y_ssm = _ssd(z, xbc, dt_raw, ssm_conv_w[l], ssm_conv_b[l], ssm_dt_bias[l], ssm_a_log[l], ssm_d[l],
                     ssm_norm[l])
        y_sb = _stick_breaking(sq, sk, sv)
        y_diff = _diff_attention(dq, dk, dv, tables, diff_lambda_q1[l], diff_lambda_k1[l], diff_lambda_q2[l],
                                 diff_lambda_k2[l], diff_subln[l], lambda_init)
        x = _outproj(x, mod, y_ssm, y_sb, y_diff, w_out, l)
        x = _ffn(x, mod, ffn2_norm[l], ffn2_w13, ffn2_w2, l, mod_row=6,
                 final_g=final_norm if l == depth - 1 else None)
    return x
```

```python
import functools
import math

import numpy as np
import jax
import jax.numpy as jnp
from jax import lax
from jax.experimental import pallas as pl
from jax.experimental.pallas import tpu as pltpu

F32 = jnp.float32
BF16 = jnp.bfloat16
HIGHEST = lax.Precision.HIGHEST
LOG2E = math.log2(math.e)

HEAD_DIM = 64
SSM_HEADS = 16
SSM_GROUPS = 2
SSM_STATE = 128
SSM_CONV = 4
SSM_CHUNK = 128
SSM_D_INNER = SSM_HEADS * HEAD_DIM
SSM_CONV_DIM = SSM_D_INNER + 2 * SSM_GROUPS * SSM_STATE
SB_HEADS = 8
SB_WIDTH = SB_HEADS * HEAD_DIM
DIFF_HEADS = 4
DIFF_V_DIM = 2 * HEAD_DIM
DIFF_WIDTH = DIFF_HEADS * DIFF_V_DIM
N_MOD = 9
N_REL_BUCKETS = 32
REL_MAX_DIST = 128
EPS = 1e-6

LANES = 128
BF16_SUBLANES = 16
VMEM_LIMIT = 56 * 1024 * 1024

TOKEN_TILE = 512
FF_CHUNK = 256
ATT_BLOCK = 256

OFF_Z = 0
OFF_XBC = OFF_Z + SSM_D_INNER
OFF_SQ = OFF_XBC + SSM_CONV_DIM
OFF_SK = OFF_SQ + SB_WIDTH
OFF_SV = OFF_SK + SB_WIDTH
OFF_DQ = OFF_SV + SB_WIDTH
OFF_DK = OFF_DQ + DIFF_WIDTH
OFF_DV = OFF_DK + DIFF_WIDTH
OFF_DT = OFF_DV + DIFF_WIDTH
IN_PACKED = OFF_DT + LANES


def _cparams(semantics):
    return pltpu.CompilerParams(dimension_semantics=semantics, vmem_limit_bytes=VMEM_LIMIT)


SSD_STREAMS = 2
SB_STREAMS = 4
DIFF_STREAMS = 4


def _resident(block_shape, index_map):
    return pl.BlockSpec(block_shape, index_map, pipeline_mode=pl.Buffered(1))


def _silu(x):
    hx = 0.5 * x
    return hx + hx * jnp.tanh(hx)


def _softplus(x):
    return jnp.maximum(x, 0.0) + jnp.log1p(jnp.exp(-jnp.abs(x)))


def _norm_modulate(x, g, shift, scale):
    ms = jnp.mean(x * x, axis=-1, keepdims=True)
    y = x * lax.rsqrt(ms + EPS) * g
    return y * (1.0 + scale) + shift


def _ada_kernel(c_ref, w_ref, b_ref, o_ref):
    cond = _silu(c_ref[...])
    o_ref[...] = jnp.dot(cond, w_ref[...], preferred_element_type=F32, precision=HIGHEST) + b_ref[...]


def _ada_modulation(c, ada_w, ada_b):
    depth, d, nmod = ada_w.shape
    b = c.shape[0]
    rows = 8 * pl.cdiv(b, 8)
    c_pad = jnp.zeros((rows, d), F32).at[:b].set(c)
    out = pl.pallas_call(
        _ada_kernel,
        grid=(depth, nmod // d),
        in_specs=[
            pl.BlockSpec((rows, d), lambda l, j: (0, 0)),
            pl.BlockSpec((None, d, d), lambda l, j: (l, 0, j)),
            pl.BlockSpec((None, 1, d), lambda l, j: (l, 0, j)),
        ],
        out_specs=pl.BlockSpec((None, rows, d), lambda l, j: (l, 0, j)),
        out_shape=jax.ShapeDtypeStruct((depth, rows, nmod), F32),
        compiler_params=_cparams(("arbitrary", "arbitrary")),
        name="ada_modulation",
    )(c_pad, ada_w, ada_b.reshape(depth, 1, nmod))
    return out[:, :b].reshape(depth, b, N_MOD, d)


def _ffn_kernel(x_ref, mod_ref, g_ref, w13_ref, w2_ref, *rest, mod_row, final_norm):
    if final_norm:
        fg_ref, o_ref, acc_ref = rest
    else:
        o_ref, acc_ref = rest
    x = x_ref[...]
    shift = mod_ref[mod_row:mod_row + 1, :]
    scale = mod_ref[mod_row + 1:mod_row + 2, :]
    gate = mod_ref[mod_row + 2:mod_row + 3, :]
    h = _norm_modulate(x, g_ref[...], shift, scale).astype(BF16)
    d_ff = w2_ref.shape[0]
    for j in range(d_ff // FF_CHUNK):
        cols = slice(j * FF_CHUNK, (j + 1) * FF_CHUNK)
        w1 = w13_ref[:, cols].astype(BF16)
        w3 = w13_ref[:, d_ff + j * FF_CHUNK:d_ff + (j + 1) * FF_CHUNK].astype(BF16)
        a = jnp.dot(h, w1, preferred_element_type=F32)
        u = jnp.dot(h, w3, preferred_element_type=F32)
        act = (_silu(a) * u).astype(BF16)
        part = jnp.dot(act, w2_ref[cols, :].astype(BF16), preferred_element_type=F32)
        if j == 0:
            acc_ref[...] = part
        else:
            acc_ref[...] += part
    y = x + (0.5 * gate) * acc_ref[...]
    if final_norm:
        ms = jnp.mean(y * y, axis=-1, keepdims=True)
        y = y * lax.rsqrt(ms + EPS) * fg_ref[...]
    o_ref[...] = y


def _ffn(x, mod, g, w13, w2, layer, mod_row, final_g=None):
    b, s, d = x.shape
    d_ff = w2.shape[1]
    assert d_ff % FF_CHUNK == 0 and d_ff % LANES == 0
    tm = min(TOKEN_TILE, s)
    final_norm = final_g is not None
    in_specs = [
        pl.BlockSpec((None, tm, d), lambda bi, i: (bi, i, 0)),
        pl.BlockSpec((None, N_MOD, d), lambda bi, i: (bi, 0, 0)),
        _resident((1, d), lambda bi, i: (0, 0)),
        _resident((None, d, 2 * d_ff), lambda bi, i: (layer, 0, 0)),
        _resident((None, d_ff, d), lambda bi, i: (layer, 0, 0)),
    ]
    args = [x, mod, g.reshape(1, d), w13, w2]
    if final_norm:
        in_specs.append(_resident((1, d), lambda bi, i: (0, 0)))
        args.append(final_g.reshape(1, d))
    return pl.pallas_call(
        functools.partial(_ffn_kernel, mod_row=mod_row, final_norm=final_norm),
        grid=(b, s // tm),
        in_specs=in_specs,
        out_specs=pl.BlockSpec((None, tm, d), lambda bi, i: (bi, i, 0)),
        out_shape=jax.ShapeDtypeStruct((b, s, d), F32),
        scratch_shapes=[pltpu.VMEM((tm, d), F32)],
        compiler_params=_cparams(("parallel", "parallel")),
        name="ffn_final" if final_norm else "ffn",
    )(*args)


def _inproj_kernel(x_ref, mod_ref, g_ref, w_ref, z_ref, xbc_ref, dt_ref,
                   sq_ref, sk_ref, sv_ref, dq_ref, dk_ref, dv_ref):
    x = x_ref[...]
    h = _norm_modulate(x, g_ref[...], mod_ref[3:4, :], mod_ref[4:5, :]).astype(BF16)

    def proj(off, width):
        return jnp.dot(h, w_ref[:, off:off + width], preferred_element_type=F32)

    qk_scale = HEAD_DIM ** -0.5 * LOG2E
    z_ref[...] = proj(OFF_Z, SSM_D_INNER).astype(BF16)
    xbc_ref[...] = proj(OFF_XBC, SSM_CONV_DIM).astype(BF16)
    dt_ref[...] = proj(OFF_DT, LANES)
    sq_ref[...] = (proj(OFF_SQ, SB_WIDTH) * qk_scale).astype(BF16)
    sk_ref[...] = proj(OFF_SK, SB_WIDTH).astype(BF16)
    sv_ref[...] = proj(OFF_SV, SB_WIDTH).astype(BF16)
    dq_ref[...] = (proj(OFF_DQ, DIFF_WIDTH) * qk_scale).astype(BF16)
    dk_ref[...] = proj(OFF_DK, DIFF_WIDTH).astype(BF16)
    dv_ref[...] = proj(OFF_DV, DIFF_WIDTH).astype(BF16)


def _pack_w_in(w_in):
    depth, d, _ = w_in.shape
    dt0 = SSM_D_INNER + SSM_CONV_DIM
    dt1 = dt0 + SSM_HEADS
    packed = jnp.concatenate(
        [w_in[..., :dt0], w_in[..., dt1:], w_in[..., dt0:dt1],
         jnp.zeros((depth, d, LANES - SSM_HEADS), w_in.dtype)], axis=-1)
    return packed.astype(BF16)


def _inproj(x, mod, g, w, layer):
    b, s, d = x.shape
    tm = min(TOKEN_TILE, s)
    widths = (SSM_D_INNER, SSM_CONV_DIM, LANES) + (SB_WIDTH,) * 3 + (DIFF_WIDTH,) * 3
    dtypes = (BF16, BF16, F32) + (BF16,) * 6
    return pl.pallas_call(
        _inproj_kernel,
        grid=(b, s // tm),
        in_specs=[
            pl.BlockSpec((None, tm, d), lambda bi, i: (bi, i, 0)),
            pl.BlockSpec((None, N_MOD, d), lambda bi, i: (bi, 0, 0)),
            _resident((1, d), lambda bi, i: (0, 0)),
            _resident((None, d, IN_PACKED), lambda bi, i: (layer, 0, 0)),
        ],
        out_specs=[pl.BlockSpec((None, tm, wd), lambda bi, i: (bi, i, 0)) for wd in widths],
        out_shape=[jax.ShapeDtypeStruct((b, s, wd), dt) for wd, dt in zip(widths, dtypes)],
        compiler_params=_cparams(("parallel", "parallel")),
        name="mixer_inproj",
    )(x, mod, g.reshape(1, d), w)


def _ssd_kernel(z_ref, xbc_ref, dtr_ref, cw_ref, cb_ref, dtb_ref, alog_ref, dsk_ref, ng_ref, e_ref,
                y_ref, tail_ref, state_ref):
    for bb in range(z_ref.shape[0]):
        _ssd_chunk(z_ref.at[bb], xbc_ref.at[bb], dtr_ref.at[bb], cw_ref, cb_ref, dtb_ref, alog_ref, dsk_ref,
                   ng_ref, e_ref, y_ref.at[bb], tail_ref.at[bb], state_ref.at[bb])


def _ssd_chunk(z_ref, xbc_ref, dtr_ref, cw_ref, cb_ref, dtb_ref, alog_ref, dsk_ref, ng_ref, e_ref,
               y_ref, tail_ref, state_ref):
    L = z_ref.shape[0]
    pad = tail_ref.shape[0]
    c = pl.program_id(1)

    @pl.when(c == 0)
    def _():
        tail_ref[...] = jnp.zeros(tail_ref.shape, BF16)
        state_ref[...] = jnp.zeros(state_ref.shape, F32)

    u = xbc_ref[...]
    u_ext = jnp.concatenate([tail_ref[...], u], axis=0)
    tail_ref[...] = u[L - pad:, :]
    trow = lax.broadcasted_iota(jnp.int32, (L, pad + L), 0)
    tcol = lax.broadcasted_iota(jnp.int32, (L, pad + L), 1)
    shifts = jnp.concatenate(
        [jnp.where(tcol == trow + (pad - (SSM_CONV - 1) + k), 1.0, 0.0) for k in range(SSM_CONV)], axis=0)
    shifted = jnp.dot(shifts.astype(BF16), u_ext, preferred_element_type=F32)
    conv = cb_ref[...]
    for k in range(SSM_CONV):
        conv = conv + cw_ref[k:k + 1, :] * shifted[k * L:(k + 1) * L]
    xa = _silu(conv)
    xs = xa[:, :SSM_D_INNER]
    bmat = xa[:, SSM_D_INNER:SSM_D_INNER + SSM_GROUPS * SSM_STATE]
    cmat = xa[:, SSM_D_INNER + SSM_GROUPS * SSM_STATE:]

    dt = _softplus(dtr_ref[...] + dtb_ref[...])
    a = -jnp.exp(alog_ref[...])
    da = dt * a
    row = lax.broadcasted_iota(jnp.int32, (L, L), 0)
    col = lax.broadcasted_iota(jnp.int32, (L, L), 1)
    causal = row >= col
    a_cum = jnp.dot(causal.astype(F32), da, preferred_element_type=F32, precision=HIGHEST)
    a_cum_t = a_cum.T
    exp_a = jnp.exp(a_cum)
    decay_to_end = jnp.exp(a_cum[L - 1:L, :] - a_cum)

    stacked = jnp.concatenate([dt, exp_a, decay_to_end], axis=0)
    hi = stacked.astype(BF16)
    lo = (stacked - hi.astype(F32)).astype(BF16)
    spread = jnp.dot(jnp.concatenate([hi, lo], axis=1), e_ref[...], preferred_element_type=F32)
    dt_e = spread[0:L]
    exp_a_e = spread[L:2 * L]
    dte_e = spread[2 * L:3 * L]

    xd = xs * dt_e
    xd_b = xd.astype(BF16)
    xdd_b = (xd * dte_e).astype(BF16)
    prev_b = state_ref[...].astype(BF16)
    lane = lax.broadcasted_iota(jnp.int32, (L, LANES), 1)
    heads_per_group = SSM_HEADS // SSM_GROUPS
    group_width = heads_per_group * HEAD_DIM

    y_diag_parts = []
    y_off_parts = []
    s_new_parts = []
    for g in range(SSM_GROUPS):
        bg = bmat[:, g * SSM_STATE:(g + 1) * SSM_STATE]
        cg_b = cmat[:, g * SSM_STATE:(g + 1) * SSM_STATE].astype(BF16)
        bg_b = bg.astype(BF16)
        bg_t_b = bg.T.astype(BF16)
        gs = slice(g * group_width, (g + 1) * group_width)
        cb = lax.dot_general(cg_b, bg_b, (((1,), (1,)), ((), ())), preferred_element_type=F32)
        y_off_parts.append(jnp.dot(cg_b, prev_b[:, gs], preferred_element_type=F32))
        s_new_parts.append(jnp.dot(bg_t_b, xdd_b[:, gs], preferred_element_type=F32))
        for hp in range(heads_per_group // 2):
            ms = []
            for h in (g * heads_per_group + 2 * hp, g * heads_per_group + 2 * hp + 1):
                seg = a_cum[:, h:h + 1] - a_cum_t[h:h + 1, :]
                decay = jnp.exp(jnp.where(causal, seg, -jnp.inf))
                ms.append((cb * decay).astype(BF16))
            pair = (g * heads_per_group) // 2 + hp
            yp = jnp.dot(jnp.concatenate(ms, axis=0), xd_b[:, pair * LANES:(pair + 1) * LANES],
                         preferred_element_type=F32)
            y_diag_parts.append(jnp.where(lane < HEAD_DIM, yp[:L], yp[L:]))
    y_diag = jnp.concatenate(y_diag_parts, axis=1)
    y_off = jnp.concatenate(y_off_parts, axis=1)
    s_new = jnp.concatenate(s_new_parts, axis=1)

    y = y_diag + y_off * exp_a_e + dsk_ref[...] * xs
    state_ref[...] = state_ref[...] * exp_a_e[L - 1:L, :] + s_new

    y = y * _silu(z_ref[...].astype(F32))
    outs = []
    for g in range(SSM_GROUPS):
        yg = y[:, g * group_width:(g + 1) * group_width]
        ms = jnp.mean(yg * yg, axis=-1, keepdims=True)
        outs.append(yg * lax.rsqrt(ms + EPS))
    y_ref[...] = (jnp.concatenate(outs, axis=1) * ng_ref[...]).astype(BF16)


def _head_spread_matrix():
    e = np.zeros((2 * LANES, SSM_D_INNER), np.float32)
    for h in range(SSM_HEADS):
        e[h, h * HEAD_DIM:(h + 1) * HEAD_DIM] = 1.0
        e[LANES + h, h * HEAD_DIM:(h + 1) * HEAD_DIM] = 1.0
    return jnp.asarray(e, BF16)


def _pad_lanes(v):
    return jnp.zeros((1, LANES), F32).at[0, :v.shape[0]].set(v)


def _ssd(z, xbc, dt_raw, conv_w, conv_b, dt_bias, a_log, d_skip, norm_g):
    b, s, _ = z.shape
    L = SSM_CHUNK
    nb = SSD_STREAMS if b % SSD_STREAMS == 0 else 1
    row = lambda width: _resident((1, width), lambda bi, c: (0, 0))
    return pl.pallas_call(
        _ssd_kernel,
        grid=(b // nb, s // L),
        in_specs=[
            pl.BlockSpec((nb, L, SSM_D_INNER), lambda bi, c: (bi, c, 0)),
            pl.BlockSpec((nb, L, SSM_CONV_DIM), lambda bi, c: (bi, c, 0)),
            pl.BlockSpec((nb, L, LANES), lambda bi, c: (bi, c, 0)),
            _resident((SSM_CONV, SSM_CONV_DIM), lambda bi, c: (0, 0)),
            row(SSM_CONV_DIM), row(LANES), row(LANES), row(SSM_D_INNER), row(SSM_D_INNER),
            _resident((2 * LANES, SSM_D_INNER), lambda bi, c: (0, 0)),
        ],
        out_specs=pl.BlockSpec((nb, L, SSM_D_INNER), lambda bi, c: (bi, c, 0)),
        out_shape=jax.ShapeDtypeStruct((b, s, SSM_D_INNER), BF16),
        scratch_shapes=[pltpu.VMEM((nb, BF16_SUBLANES, SSM_CONV_DIM), BF16),
                        pltpu.VMEM((nb, SSM_STATE, SSM_D_INNER), F32)],
        compiler_params=_cparams(("parallel", "arbitrary")),
        name="ssd_mixer",
    )(z, xbc, dt_raw, conv_w.T, conv_b.reshape(1, -1), _pad_lanes(dt_bias), _pad_lanes(a_log),
      jnp.repeat(d_skip, HEAD_DIM).reshape(1, -1), norm_g.reshape(1, -1), _head_spread_matrix())


def _split_lane_halves(x2):
    xf = x2.astype(F32)
    lane = lax.broadcasted_iota(jnp.int32, xf.shape, 1)
    lo = jnp.where(lane < HEAD_DIM, xf, 0.0)
    hi = jnp.where(lane >= HEAD_DIM, xf, 0.0)
    return jnp.concatenate([lo, hi], axis=0).astype(BF16)


def _scores_t(kb, qs):
    return lax.dot_general(kb, qs, (((1,), (1,)), ((), ())), preferred_element_type=F32)


def _pv_t(vb, wt):
    return lax.dot_general(vb, wt, (((0,), (0,)), ((), ())), preferred_element_type=F32)


def _sb_kernel(q_ref, k_ref, v_ref, o_ref, z_ref, cs_ref, tot_ref, acc_ref, carry_ref):
    T = q_ref.shape[0]
    n_streams = q_ref.shape[1] // LANES
    i = pl.program_id(2)
    qs = [_split_lane_halves(q_ref[:, p * LANES:(p + 1) * LANES]) for p in range(n_streams)]

    key = lax.broadcasted_iota(jnp.int32, (T, 2 * T), 0)
    qry = lax.broadcasted_iota(jnp.int32, (T, 2 * T), 1)
    strict = key < jnp.where(qry >= T, qry - T, qry)
    ur = lax.broadcasted_iota(jnp.int32, (T, T), 0)
    uc = lax.broadcasted_iota(jnp.int32, (T, T), 1)
    neg_suffix = jnp.where(uc >= ur, -1.0, 0.0).astype(BF16)
    streams = range(n_streams)
    sign_bit = jnp.uint32(0x80000000)

    def scores(j_lo, nb):
        rows = pl.ds(pl.multiple_of(j_lo * T, T), nb * T)
        return [_scores_t(k_ref[rows, p * LANES:(p + 1) * LANES], qs[p]) for p in streams]

    def softplus2(z, diagonal):
        neg_abs = lax.bitcast_convert_type(lax.bitcast_convert_type(z, jnp.uint32) | sign_bit, F32)
        sp = jnp.maximum(z, 0.0) + jnp.log(1.0 + jnp.exp2(neg_abs)) * LOG2E
        if diagonal:
            sp = jnp.where(strict, sp, 0.0)
        return sp.astype(BF16)

    def suffix_sums(sp, nb):
        parts = [jnp.dot(neg_suffix, sp[u * T:(u + 1) * T], preferred_element_type=F32) for u in range(nb)]
        total = parts[nb - 1][0:1, :]
        for u in reversed(range(nb - 1)):
            parts[u] = parts[u] + total
            total = parts[u][0:1, :]
        return (parts[0] if nb == 1 else jnp.concatenate(parts, axis=0)), total

    def weights(z, cs, carry, diagonal):
        w = jnp.exp2(z + (cs + carry))
        if diagonal:
            w = jnp.where(strict, w, 0.0)
        return w.astype(BF16)

    def accumulate(j_lo, nb, ws, tots):
        rows = pl.ds(pl.multiple_of(j_lo * T, T), nb * T)
        for p in streams:
            acc_ref[p] += _pv_t(v_ref[rows, p * LANES:(p + 1) * LANES], ws[p])
            carry_ref[p] += tots[p]

    def direct_step(j_lo, nb, diagonal=False):
        zs = scores(j_lo, nb)
        cts = [suffix_sums(softplus2(z, diagonal), nb) for z in zs]
        ws = [weights(zs[p], cts[p][0], carry_ref[p], diagonal) for p in streams]
        accumulate(j_lo, nb, ws, [ct[1] for ct in cts])

    def finish_produce(slot, zs):
        for p in streams:
            z_ref[slot, p] = zs[p]
        for p in streams:
            cs, tot = suffix_sums(softplus2(z_ref[slot, p], False), 2)
            cs_ref[slot, p] = cs
            tot_ref[slot, p] = tot

    for p in streams:
        acc_ref[p] = jnp.zeros((LANES, 2 * T), F32)
        carry_ref[p] = jnp.zeros((1, 2 * T), F32)
    direct_step(i, 1, diagonal=True)
    n_double = i // 2
    pair_lo = lambda t: i - 2 - 2 * t

    def consume(t, slot, produce_next):
        ws = [weights(z_ref[slot, p], cs_ref[slot, p], carry_ref[p], False) for p in streams]
        tots = [tot_ref[slot, p] for p in streams]
        if produce_next:
            finish_produce(1 - slot, scores(pair_lo(t + 1), 2))
        accumulate(pair_lo(t), 2, ws, tots)

    @pl.when(n_double > 0)
    def _():
        finish_produce(0, scores(pair_lo(0), 2))

    n_twice = jnp.maximum(n_double - 1, 0) // 2

    def body(u, _):
        consume(2 * u, 0, True)
        consume(2 * u + 1, 1, True)
        return 0

    lax.fori_loop(0, n_twice, body, 0)
    left = n_double - 2 * n_twice

    @pl.when(left == 2)
    def _():
        consume(2 * n_twice, 0, True)
        consume(2 * n_twice + 1, 1, False)

    @pl.when(left == 1)
    def _():
        consume(2 * n_twice, 0, False)

    @pl.when(i - 2 * n_double > 0)
    def _():
        direct_step(0, 1)

    for p in range(n_streams):
        acc = acc_ref[p]
        pair_t = jnp.concatenate([acc[:HEAD_DIM, :T], acc[HEAD_DIM:, T:]], axis=0)
        o_ref[:, p * LANES:(p + 1) * LANES] = pair_t.T.astype(BF16)


def _stick_breaking(q, k, v):
    b, s, width = q.shape
    T = min(ATT_BLOCK, s)
    gw = SB_STREAMS * LANES
    return pl.pallas_call(
        _sb_kernel,
        grid=(b, width // gw, s // T),
        in_specs=[
            pl.BlockSpec((None, T, gw), lambda bi, g, i: (bi, i, g)),
            pl.BlockSpec((None, s, gw), lambda bi, g, i: (bi, 0, g)),
            pl.BlockSpec((None, s, gw), lambda bi, g, i: (bi, 0, g)),
        ],
        out_specs=pl.BlockSpec((None, T, gw), lambda bi, g, i: (bi, i, g)),
        out_shape=jax.ShapeDtypeStruct((b, s, width), BF16),
        scratch_shapes=[
            pltpu.VMEM((2, SB_STREAMS, 2 * T, 2 * T), F32),
            pltpu.VMEM((2, SB_STREAMS, 2 * T, 2 * T), F32),
            pltpu.VMEM((2, SB_STREAMS, 1, 2 * T), F32),
            pltpu.VMEM((SB_STREAMS, LANES, 2 * T), F32),
            pltpu.VMEM((SB_STREAMS, 1, 2 * T), F32),
        ],
        compiler_params=_cparams(("parallel", "parallel", "arbitrary")),
        name="stick_breaking_attention",
    )(q, k, v)


def _bucket_starts():
    max_exact = N_REL_BUCKETS // 2
    dist = np.arange(0, 4 * REL_MAX_DIST, dtype=np.float64)
    ratio = np.log(np.maximum(dist, max_exact) / max_exact) / math.log(REL_MAX_DIST / max_exact)
    scaled = ratio * (N_REL_BUCKETS - max_exact)
    frac = np.abs(scaled - np.round(scaled))
    interior = (dist > max_exact) & (dist < REL_MAX_DIST)
    assert frac[interior].min() > 1e-3, "a bucket edge sits on an integer distance"
    large = np.minimum(max_exact + np.floor(scaled + 1e-9).astype(np.int64), N_REL_BUCKETS - 1)
    bucket = np.where(dist < max_exact, dist.astype(np.int64), large)
    assert np.all(np.diff(bucket) >= 0) and bucket[-1] == N_REL_BUCKETS - 1
    return [int(np.argmax(bucket >= bkt)) for bkt in range(N_REL_BUCKETS)]


def _bias_table_kernel(rel_ref, o_ref, *, starts):
    h = pl.program_id(0)
    T = o_ref.shape[-1]
    key = lax.broadcasted_iota(jnp.int32, (T, T), 0)
    qry = lax.broadcasted_iota(jnp.int32, (T, T), 1)
    far = rel_ref[N_REL_BUCKETS - 1, h]
    for o_blk in range(o_ref.shape[0]):
        dist = qry - key + o_blk * T
        bias = jnp.full((T, T), rel_ref[0, h], F32)
        for bkt in range(1, N_REL_BUCKETS):
            bias = jnp.where(dist >= starts[bkt], rel_ref[bkt, h], bias)
        o_ref[o_blk] = jnp.where(dist >= 0, (bias - far) * LOG2E, -jnp.inf)


def _bias_tables(rel_bias, T):
    starts = _bucket_starts()
    assert T + 1 >= starts[-1]
    return pl.pallas_call(
        functools.partial(_bias_table_kernel, starts=starts),
        grid=(DIFF_HEADS,),
        in_specs=[pl.BlockSpec(memory_space=pltpu.SMEM)],
        out_specs=pl.BlockSpec((None, 2, T, T), lambda h: (h, 0, 0, 0)),
        out_shape=jax.ShapeDtypeStruct((DIFF_HEADS, 2, T, T), F32),
        compiler_params=_cparams(("arbitrary",)),
        name="t5_bias_tables",
    )(rel_bias)


def _diff_kernel(q_ref, k_ref, v_ref, tb_ref, lq1_ref, lk1_ref, lq2_ref, lk2_ref, sg_ref, o_ref,
                 s_ref, m_ref, l_ref, acc_ref, *, lambda_init):
    T = q_ref.shape[0]
    n_streams = q_ref.shape[1] // LANES
    i = pl.program_id(2)
    qs = [_split_lane_halves(q_ref[:, h * LANES:(h + 1) * LANES]) for h in range(n_streams)]

    heads = range(n_streams)

    def scores(j_lo, nb):
        rows = pl.ds(pl.multiple_of(j_lo * T, T), nb * T)
        return [_scores_t(k_ref[rows, h * LANES:(h + 1) * LANES], qs[h]) for h in heads]

    def softmax_part(ss, table_rows):
        out = []
        for h in heads:
            read = ss[h] if callable(ss[h]) else (lambda v=ss[h]: v)
            if table_rows is not None:
                biases = [tb_ref[h, o] for o in table_rows]
                bias = biases[0] if len(biases) == 1 else jnp.concatenate(biases, axis=0)
                s = read() + jnp.concatenate([bias, bias], axis=1)
                read = lambda v=s: v
            m_old = m_ref[h]
            m_new = jnp.maximum(m_old, jnp.max(read(), axis=0, keepdims=True))
            alpha = jnp.exp2(m_old - m_new)
            p = jnp.exp2(read() - m_new)
            m_ref[h] = m_new
            l_ref[h] = alpha * l_ref[h] + jnp.sum(p, axis=0, keepdims=True)
            out.append((alpha, p.astype(BF16)))
        return out

    def accumulate(j_lo, nb, aps):
        rows = pl.ds(pl.multiple_of(j_lo * T, T), nb * T)
        for h in heads:
            alpha, p = aps[h]
            acc_ref[h] = alpha * acc_ref[h] + _pv_t(v_ref[rows, h * LANES:(h + 1) * LANES], p)

    for h in heads:
        m_ref[h] = jnp.full((1, 2 * T), -jnp.inf, F32)
        l_ref[h] = jnp.zeros((1, 2 * T), F32)
        acc_ref[h] = jnp.zeros((DIFF_V_DIM, 2 * T), F32)
    n_far = jnp.maximum(i - 1, 0)
    n_double = n_far // 2
    first = n_far - 2 * n_double
    pair_lo = lambda t: first + 2 * t

    def produce(t, slot):
        for h, s in enumerate(scores(pair_lo(t), 2)):
            s_ref[slot, h] = s

    def consume(t, slot, produce_next, table_rows=None):
        aps = softmax_part([lambda h=h: s_ref[slot, h] for h in heads], table_rows)
        if produce_next:
            produce(t + 1, 1 - slot)
        accumulate(pair_lo(t), 2, aps)

    @pl.when(first > 0)
    def _():
        accumulate(0, 1, softmax_part(scores(0, 1), None))

    @pl.when(i > 0)
    def _():
        produce(0, 0)

    n_twice = n_double // 2

    def body(u, _):
        consume(2 * u, 0, True)
        consume(2 * u + 1, 1, True)
        return 0

    lax.fori_loop(0, n_twice, body, 0)
    left = n_double - 2 * n_twice

    @pl.when(jnp.logical_and(i > 0, left == 1))
    def _():
        consume(2 * n_twice, 0, True)
        consume(2 * n_twice + 1, 1, False, [1, 0])

    @pl.when(jnp.logical_and(i > 0, left == 0))
    def _():
        consume(2 * n_twice, 0, False, [1, 0])

    @pl.when(i == 0)
    def _():
        accumulate(0, 1, softmax_part(scores(0, 1), [0]))

    lam = (jnp.exp(jnp.sum(lq1_ref[...] * lk1_ref[...], axis=-1, keepdims=True))
           - jnp.exp(jnp.sum(lq2_ref[...] * lk2_ref[...], axis=-1, keepdims=True)) + lambda_init)
    for h in range(n_streams):
        l, acc = l_ref[h], acc_ref[h]
        o_t = acc[:, :T] / l[:, :T] - lam * (acc[:, T:] / l[:, T:])
        o = o_t.T
        ms = jnp.mean(o * o, axis=-1, keepdims=True)
        o = o * lax.rsqrt(ms + EPS) * sg_ref[...] * (1.0 - lambda_init)
        o_ref[:, h * LANES:(h + 1) * LANES] = o.astype(BF16)


def _diff_attention(q, k, v, tables, lq1, lk1, lq2, lk2, subln_g, lambda_init):
    b, s, width = q.shape
    T = tables.shape[-1]
    vec = lambda n: _resident((1, n), lambda bi, g, i: (0, 0))
    gw = DIFF_STREAMS * LANES
    return pl.pallas_call(
        functools.partial(_diff_kernel, lambda_init=lambda_init),
        grid=(b, width // gw, s // T),
        in_specs=[
            pl.BlockSpec((None, T, gw), lambda bi, g, i: (bi, i, g)),
            pl.BlockSpec((None, s, gw), lambda bi, g, i: (bi, 0, g)),
            pl.BlockSpec((None, s, gw), lambda bi, g, i: (bi, 0, g)),
            pl.BlockSpec((DIFF_STREAMS,) + tables.shape[1:], lambda bi, g, i: (g, 0, 0, 0)),
            vec(HEAD_DIM), vec(HEAD_DIM), vec(HEAD_DIM), vec(HEAD_DIM), vec(DIFF_V_DIM),
        ],
        out_specs=pl.BlockSpec((None, T, gw), lambda bi, g, i: (bi, i, g)),
        out_shape=jax.ShapeDtypeStruct((b, s, width), BF16),
        scratch_shapes=[
            pltpu.VMEM((2, DIFF_STREAMS, 2 * T, 2 * T), F32),
            pltpu.VMEM((DIFF_STREAMS, 1, 2 * T), F32),
            pltpu.VMEM((DIFF_STREAMS, 1, 2 * T), F32),
            pltpu.VMEM((DIFF_STREAMS, DIFF_V_DIM, 2 * T), F32),
        ],
        compiler_params=_cparams(("parallel", "parallel", "arbitrary")),
        name="differential_attention",
    )(q, k, v, tables, lq1.reshape(1, -1), lk1.reshape(1, -1), lq2.reshape(1, -1), lk2.reshape(1, -1),
      subln_g.reshape(1, -1))


def _outproj_kernel(x_ref, mod_ref, ys_ref, yb_ref, yd_ref, w_ref, o_ref):
    o1 = SSM_D_INNER
    o2 = o1 + SB_WIDTH
    y = jnp.dot(ys_ref[...], w_ref[0:o1, :].astype(BF16), preferred_element_type=F32)
    y = y + jnp.dot(yb_ref[...], w_ref[o1:o2, :].astype(BF16), preferred_element_type=F32)
    y = y + jnp.dot(yd_ref[...], w_ref[o2:, :].astype(BF16), preferred_element_type=F32)
    o_ref[...] = x_ref[...] + mod_ref[5:6, :] * y


def _outproj(x, mod, y_ssm, y_sb, y_diff, w_out, layer):
    b, s, d = x.shape
    tm = min(TOKEN_TILE, s)
    tok = lambda width: pl.BlockSpec((None, tm, width), lambda bi, i: (bi, i, 0))
    return pl.pallas_call(
        _outproj_kernel,
        grid=(b, s // tm),
        in_specs=[
            tok(d),
            pl.BlockSpec((None, N_MOD, d), lambda bi, i: (bi, 0, 0)),
            tok(SSM_D_INNER), tok(SB_WIDTH), tok(DIFF_WIDTH),
            _resident((None,) + w_out.shape[1:], lambda bi, i: (layer, 0, 0)),
        ],
        out_specs=tok(d),
        out_shape=jax.ShapeDtypeStruct((b, s, d), F32),
        compiler_params=_cparams(("parallel", "parallel")),
        name="mixer_outproj",
    )(x, mod, y_ssm, y_sb, y_diff, w_out)


def kernel(x, c, ada_w, ada_b, ffn1_norm, ffn1_w13, ffn1_w2, mix_norm, w_in, ssm_conv_w, ssm_conv_b, ssm_dt_bias, ssm_a_log, ssm_d, ssm_norm, diff_lambda_q1, diff_lambda_k1, diff_lambda_q2, diff_lambda_k2, diff_subln, rel_bias, w_out, ffn2_norm, ffn2_w13, ffn2_w2, final_norm):
    depth = ada_w.shape[0]
    s = x.shape[1]
    mods = _ada_modulation(c, ada_w, ada_b)
    tables = _bias_tables(rel_bias, min(ATT_BLOCK, s))
    w_in_packed = _pack_w_in(w_in)
    for l in range(depth):
        mod = mods[l]
        x = _ffn(x, mod, ffn1_norm[l], ffn1_w13, ffn1_w2, l, mod_row=0)
        lambda_init = 0.8 - 0.6 * math.exp(-0.3 * l)
        z, xbc, dt_raw, sq, sk, sv, dq, dk, dv = _inproj(x, mod, mix_norm[l], w_in_packed, l)
        y_ssm = _ssd(z, xbc, dt_raw, ssm_conv_w[l], ssm_conv_b[l], ssm_dt_bias[l], ssm_a_log[l], ssm_d[l],
                     ssm_norm[l])
        y_sb = _stick_breaking(sq, sk, sv)
        y_diff = _diff_attention(dq, dk, dv, tables, diff_lambda_q1[l], diff_lambda_k1[l], diff_lambda_q2[l],
                                 diff_lambda_k2[l], diff_subln[l], lambda_init)
        x = _outproj(x, mod, y_ssm, y_sb, y_diff, w_out, l)
        x = _ffn(x, mod, ffn2_norm[l], ffn2_w13, ffn2_w2, l, mod_row=6,
                 final_g=final_norm if l == depth - 1 else None)
    return x
```

```python
import functools
import math

import numpy as np
import jax
import jax.numpy as jnp
from jax import lax
from jax.experimental import pallas as pl
from jax.experimental.pallas import tpu as pltpu

F32 = jnp.float32
BF16 = jnp.bfloat16
HIGHEST = lax.Precision.HIGHEST
LOG2E = math.log2(math.e)

HEAD_DIM = 64
SSM_HEADS = 16
SSM_GROUPS = 2
SSM_STATE = 128
SSM_CONV = 4
SSM_CHUNK = 128
SSM_D_INNER = SSM_HEADS * HEAD_DIM
SSM_CONV_DIM = SSM_D_INNER + 2 * SSM_GROUPS * SSM_STATE
SB_HEADS = 8
SB_WIDTH = SB_HEADS * HEAD_DIM
DIFF_HEADS = 4
DIFF_V_DIM = 2 * HEAD_DIM
DIFF_WIDTH = DIFF_HEADS * DIFF_V_DIM
N_MOD = 9
N_REL_BUCKETS = 32
REL_MAX_DIST = 128
EPS = 1e-6

LANES = 128
BF16_SUBLANES = 16
VMEM_LIMIT = 56 * 1024 * 1024

TOKEN_TILE = 512
FF_CHUNK = 256
ATT_BLOCK = 256

OFF_Z = 0
OFF_XBC = OFF_Z + SSM_D_INNER
OFF_SQ = OFF_XBC + SSM_CONV_DIM
OFF_SK = OFF_SQ + SB_WIDTH
OFF_SV = OFF_SK + SB_WIDTH
OFF_DQ = OFF_SV + SB_WIDTH
OFF_DK = OFF_DQ + DIFF_WIDTH
OFF_DV = OFF_DK + DIFF_WIDTH
OFF_DT = OFF_DV + DIFF_WIDTH
IN_PACKED = OFF_DT + LANES


def _cparams(semantics):
    return pltpu.CompilerParams(dimension_semantics=semantics, vmem_limit_bytes=VMEM_LIMIT)


SSD_STREAMS = 2
SB_STREAMS = 4
DIFF_STREAMS = 4


def _resident(block_shape, index_map):
    return pl.BlockSpec(block_shape, index_map, pipeline_mode=pl.Buffered(1))


def _silu(x):
    hx = 0.5 * x
    return hx + hx * jnp.tanh(hx)


def _softplus(x):
    return jnp.maximum(x, 0.0) + jnp.log1p(jnp.exp(-jnp.abs(x)))


def _norm_modulate(x, g, shift, scale):
    ms = jnp.mean(x * x, axis=-1, keepdims=True)
    y = x * lax.rsqrt(ms + EPS) * g
    return y * (1.0 + scale) + shift


def _ada_kernel(c_ref, w_ref, b_ref, o_ref):
    cond = _silu(c_ref[...])
    o_ref[...] = jnp.dot(cond, w_ref[...], preferred_element_type=F32, precision=HIGHEST) + b_ref[...]


def _ada_modulation(c, ada_w, ada_b):
    depth, d, nmod = ada_w.shape
    b = c.shape[0]
    rows = 8 * pl.cdiv(b, 8)
    c_pad = jnp.zeros((rows, d), F32).at[:b].set(c)
    out = pl.pallas_call(
        _ada_kernel,
        grid=(depth, nmod // d),
        in_specs=[
            pl.BlockSpec((rows, d), lambda l, j: (0, 0)),
            pl.BlockSpec((None, d, d), lambda l, j: (l, 0, j)),
            pl.BlockSpec((None, 1, d), lambda l, j: (l, 0, j)),
        ],
        out_specs=pl.BlockSpec((None, rows, d), lambda l, j: (l, 0, j)),
        out_shape=jax.ShapeDtypeStruct((depth, rows, nmod), F32),
        compiler_params=_cparams(("arbitrary", "arbitrary")),
        name="ada_modulation",
    )(c_pad, ada_w, ada_b.reshape(depth, 1, nmod))
    return out[:, :b].reshape(depth, b, N_MOD, d)


def _ffn_kernel(x_ref, mod_ref, g_ref, w13_ref, w2_ref, *rest, mod_row, final_norm):
    if final_norm:
        fg_ref, o_ref, acc_ref = rest
    else:
        o_ref, acc_ref = rest
    x = x_ref[...]
    shift = mod_ref[mod_row:mod_row + 1, :]
    scale = mod_ref[mod_row + 1:mod_row + 2, :]
    gate = mod_ref[mod_row + 2:mod_row + 3, :]
    h = _norm_modulate(x, g_ref[...], shift, scale).astype(BF16)
    d_ff = w2_ref.shape[0]
    for j in range(d_ff // FF_CHUNK):
        cols = slice(j * FF_CHUNK, (j + 1) * FF_CHUNK)
        w1 = w13_ref[:, cols].astype(BF16)
        w3 = w13_ref[:, d_ff + j * FF_CHUNK:d_ff + (j + 1) * FF_CHUNK].astype(BF16)
        a = jnp.dot(h, w1, preferred_element_type=F32)
        u = jnp.dot(h, w3, preferred_element_type=F32)
        act = (_silu(a) * u).astype(BF16)
        part = jnp.dot(act, w2_ref[cols, :].astype(BF16), preferred_element_type=F32)
        if j == 0:
            acc_ref[...] = part
        else:
            acc_ref[...] += part
    y = x + (0.5 * gate) * acc_ref[...]
    if final_norm:
        ms = jnp.mean(y * y, axis=-1, keepdims=True)
        y = y * lax.rsqrt(ms + EPS) * fg_ref[...]
    o_ref[...] = y


def _ffn(x, mod, g, w13, w2, layer, mod_row, final_g=None):
    b, s, d = x.shape
    d_ff = w2.shape[1]
    assert d_ff % FF_CHUNK == 0 and d_ff % LANES == 0
    tm = min(TOKEN_TILE, s)
    final_norm = final_g is not None
    in_specs = [
        pl.BlockSpec((None, tm, d), lambda bi, i: (bi, i, 0)),
        pl.BlockSpec((None, N_MOD, d), lambda bi, i: (bi, 0, 0)),
        _resident((1, d), lambda bi, i: (0, 0)),
        _resident((None, d, 2 * d_ff), lambda bi, i: (layer, 0, 0)),
        _resident((None, d_ff, d), lambda bi, i: (layer, 0, 0)),
    ]
    args = [x, mod, g.reshape(1, d), w13, w2]
    if final_norm:
        in_specs.append(_resident((1, d), lambda bi, i: (0, 0)))
        args.append(final_g.reshape(1, d))
    return pl.pallas_call(
        functools.partial(_ffn_kernel, mod_row=mod_row, final_norm=final_norm),
        grid=(b, s // tm),
        in_specs=in_specs,
        out_specs=pl.BlockSpec((None, tm, d), lambda bi, i: (bi, i, 0)),
        out_shape=jax.ShapeDtypeStruct((b, s, d), F32),
        scratch_shapes=[pltpu.VMEM((tm, d), F32)],
        compiler_params=_cparams(("parallel", "parallel")),
        name="ffn_final" if final_norm else "ffn",
    )(*args)


def _inproj_kernel(x_ref, mod_ref, g_ref, w_ref, z_ref, xbc_ref, dt_ref,
                   sq_ref, sk_ref, sv_ref, dq_ref, dk_ref, dv_ref):
    x = x_ref[...]
    h = _norm_modulate(x, g_ref[...], mod_ref[3:4, :], mod_ref[4:5, :]).astype(BF16)

    def proj(off, width):
        return jnp.dot(h, w_ref[:, off:off + width], preferred_element_type=F32)

    qk_scale = HEAD_DIM ** -0.5 * LOG2E
    z_ref[...] = proj(OFF_Z, SSM_D_INNER).astype(BF16)
    xbc_ref[...] = proj(OFF_XBC, SSM_CONV_DIM).astype(BF16)
    dt_ref[...] = proj(OFF_DT, LANES)
    sq_ref[...] = (proj(OFF_SQ, SB_WIDTH) * qk_scale).astype(BF16)
    sk_ref[...] = proj(OFF_SK, SB_WIDTH).astype(BF16)
    sv_ref[...] = proj(OFF_SV, SB_WIDTH).astype(BF16)
    dq_ref[...] = (proj(OFF_DQ, DIFF_WIDTH) * qk_scale).astype(BF16)
    dk_ref[...] = proj(OFF_DK, DIFF_WIDTH).astype(BF16)
    dv_ref[...] = proj(OFF_DV, DIFF_WIDTH).astype(BF16)


def _pack_kernel(w_ref, o_ref):
    dt0 = SSM_D_INNER + SSM_CONV_DIM
    dt1 = dt0 + SSM_HEADS
    w = w_ref[...]
    rows = w.shape[0]
    o_ref[:, :dt0] = w[:, :dt0].astype(BF16)
    o_ref[:, dt0:OFF_DT] = w[:, dt1:].astype(BF16)
    o_ref[:, OFF_DT:] = jnp.concatenate(
        [w[:, dt0:dt1], jnp.zeros((rows, LANES - SSM_HEADS), F32)], axis=1).astype(BF16)


def _pack_w_in(w_in):
    depth, d, n_in = w_in.shape
    rows = 128
    return pl.pallas_call(
        _pack_kernel,
        grid=(depth, d // rows),
        in_specs=[pl.BlockSpec((None, rows, n_in), lambda l, r: (l, r, 0))],
        out_specs=pl.BlockSpec((None, rows, IN_PACKED), lambda l, r: (l, r, 0)),
        out_shape=jax.ShapeDtypeStruct((depth, d, IN_PACKED), BF16),
        compiler_params=_cparams(("arbitrary", "arbitrary")),
        name="pack_w_in",
    )(w_in)


def _inproj(x, mod, g, w, layer):
    b, s, d = x.shape
    tm = min(TOKEN_TILE, s)
    widths = (SSM_D_INNER, SSM_CONV_DIM, LANES) + (SB_WIDTH,) * 3 + (DIFF_WIDTH,) * 3
    dtypes = (BF16, BF16, F32) + (BF16,) * 6
    return pl.pallas_call(
        _inproj_kernel,
        grid=(b, s // tm),
        in_specs=[
            pl.BlockSpec((None, tm, d), lambda bi, i: (bi, i, 0)),
            pl.BlockSpec((None, N_MOD, d), lambda bi, i: (bi, 0, 0)),
            _resident((1, d), lambda bi, i: (0, 0)),
            _resident((None, d, IN_PACKED), lambda bi, i: (layer, 0, 0)),
        ],
        out_specs=[pl.BlockSpec((None, tm, wd), lambda bi, i: (bi, i, 0)) for wd in widths],
        out_shape=[jax.ShapeDtypeStruct((b, s, wd), dt) for wd, dt in zip(widths, dtypes)],
        compiler_params=_cparams(("parallel", "parallel")),
        name="mixer_inproj",
    )(x, mod, g.reshape(1, d), w)


def _ssd_kernel(z_ref, xbc_ref, dtr_ref, cw_ref, cb_ref, dtb_ref, alog_ref, dsk_ref, ng_ref, e_ref,
                y_ref, tail_ref, state_ref):
    rows = range(z_ref.shape[0])
    convs = [_ssd_conv(xbc_ref.at[bb], cw_ref, cb_ref, tail_ref.at[bb], state_ref.at[bb]) for bb in rows]
    for bb in rows:
        _ssd_scan(convs[bb], z_ref.at[bb], dtr_ref.at[bb], dtb_ref, alog_ref, dsk_ref, ng_ref, e_ref,
                  y_ref.at[bb], state_ref.at[bb])


def _ssd_conv(xbc_ref, cw_ref, cb_ref, tail_ref, state_ref):
    L = xbc_ref.shape[0]
    pad = tail_ref.shape[0]
    c = pl.program_id(1)

    @pl.when(c == 0)
    def _():
        tail_ref[...] = jnp.zeros(tail_ref.shape, BF16)
        state_ref[...] = jnp.zeros(state_ref.shape, F32)

    u = xbc_ref[...]
    u_ext = jnp.concatenate([tail_ref[...], u], axis=0)
    tail_ref[...] = u[L - pad:, :]
    trow = lax.broadcasted_iota(jnp.int32, (L, pad + L), 0)
    tcol = lax.broadcasted_iota(jnp.int32, (L, pad + L), 1)
    shifts = jnp.concatenate(
        [jnp.where(tcol == trow + (pad - (SSM_CONV - 1) + k), 1.0, 0.0) for k in range(SSM_CONV)], axis=0)
    shifted = jnp.dot(shifts.astype(BF16), u_ext, preferred_element_type=F32)
    conv = cb_ref[...]
    for k in range(SSM_CONV):
        conv = conv + cw_ref[k:k + 1, :] * shifted[k * L:(k + 1) * L]
    return _silu(conv)


def _ssd_scan(xa, z_ref, dtr_ref, dtb_ref, alog_ref, dsk_ref, ng_ref, e_ref, y_ref, state_ref):
    L = z_ref.shape[0]
    xs = xa[:, :SSM_D_INNER]
    bmat = xa[:, SSM_D_INNER:SSM_D_INNER + SSM_GROUPS * SSM_STATE]
    cmat = xa[:, SSM_D_INNER + SSM_GROUPS * SSM_STATE:]

    dt = _softplus(dtr_ref[...] + dtb_ref[...])
    a = -jnp.exp(alog_ref[...])
    da = dt * a
    row = lax.broadcasted_iota(jnp.int32, (L, L), 0)
    col = lax.broadcasted_iota(jnp.int32, (L, L), 1)
    causal = row >= col
    a_cum = jnp.dot(causal.astype(F32), da, preferred_element_type=F32, precision=HIGHEST)
    a_cum_t = a_cum.T
    exp_a = jnp.exp(a_cum)
    decay_to_end = jnp.exp(a_cum[L - 1:L, :] - a_cum)

    stacked = jnp.concatenate([dt, exp_a, decay_to_end], axis=0)
    hi = stacked.astype(BF16)
    lo = (stacked - hi.astype(F32)).astype(BF16)
    spread = jnp.dot(jnp.concatenate([hi, lo], axis=1), e_ref[...], preferred_element_type=F32)
    dt_e = spread[0:L]
    exp_a_e = spread[L:2 * L]
    dte_e = spread[2 * L:3 * L]

    xd = xs * dt_e
    xd_b = xd.astype(BF16)
    xdd_b = (xd * dte_e).astype(BF16)
    prev_b = state_ref[...].astype(BF16)
    lane = lax.broadcasted_iota(jnp.int32, (L, LANES), 1)
    heads_per_group = SSM_HEADS // SSM_GROUPS
    group_width = heads_per_group * HEAD_DIM

    y_diag_parts = []
    y_off_parts = []
    s_new_parts = []
    for g in range(SSM_GROUPS):
        bg = bmat[:, g * SSM_STATE:(g + 1) * SSM_STATE]
        cg_b = cmat[:, g * SSM_STATE:(g + 1) * SSM_STATE].astype(BF16)
        bg_b = bg.astype(BF16)
        bg_t_b = bg.T.astype(BF16)
        gs = slice(g * group_width, (g + 1) * group_width)
        cb = lax.dot_general(cg_b, bg_b, (((1,), (1,)), ((), ())), preferred_element_type=F32)
        y_off_parts.append(jnp.dot(cg_b, prev_b[:, gs], preferred_element_type=F32))
        s_new_parts.append(jnp.dot(bg_t_b, xdd_b[:, gs], preferred_element_type=F32))
        for hp in range(heads_per_group // 2):
            ms = []
            for h in (g * heads_per_group + 2 * hp, g * heads_per_group + 2 * hp + 1):
                seg = a_cum[:, h:h + 1] - a_cum_t[h:h + 1, :]
                decay = jnp.exp(jnp.where(causal, seg, -jnp.inf))
                ms.append((cb * decay).astype(BF16))
            pair = (g * heads_per_group) // 2 + hp
            yp = jnp.dot(jnp.concatenate(ms, axis=0), xd_b[:, pair * LANES:(pair + 1) * LANES],
                         preferred_element_type=F32)
            y_diag_parts.append(jnp.where(lane < HEAD_DIM, yp[:L], yp[L:]))
    y_diag = jnp.concatenate(y_diag_parts, axis=1)
    y_off = jnp.concatenate(y_off_parts, axis=1)
    s_new = jnp.concatenate(s_new_parts, axis=1)

    y = y_diag + y_off * exp_a_e + dsk_ref[...] * xs
    state_ref[...] = state_ref[...] * exp_a_e[L - 1:L, :] + s_new

    y = y * _silu(z_ref[...].astype(F32))
    outs = []
    for g in range(SSM_GROUPS):
        yg = y[:, g * group_width:(g + 1) * group_width]
        ms = jnp.mean(yg * yg, axis=-1, keepdims=True)
        outs.append(yg * lax.rsqrt(ms + EPS))
    y_ref[...] = (jnp.concatenate(outs, axis=1) * ng_ref[...]).astype(BF16)


def _head_spread_matrix():
    e = np.zeros((2 * LANES, SSM_D_INNER), np.float32)
    for h in range(SSM_HEADS):
        e[h, h * HEAD_DIM:(h + 1) * HEAD_DIM] = 1.0
        e[LANES + h, h * HEAD_DIM:(h + 1) * HEAD_DIM] = 1.0
    return jnp.asarray(e, BF16)


def _pad_lanes(v):
    return jnp.zeros((1, LANES), F32).at[0, :v.shape[0]].set(v)


def _ssd(z, xbc, dt_raw, conv_w, conv_b, dt_bias, a_log, d_skip, norm_g):
    b, s, _ = z.shape
    L = SSM_CHUNK
    nb = SSD_STREAMS if b % SSD_STREAMS == 0 else 1
    row = lambda width: _resident((1, width), lambda bi, c: (0, 0))
    return pl.pallas_call(
        _ssd_kernel,
        grid=(b // nb, s // L),
        in_specs=[
            pl.BlockSpec((nb, L, SSM_D_INNER), lambda bi, c: (bi, c, 0)),
            pl.BlockSpec((nb, L, SSM_CONV_DIM), lambda bi, c: (bi, c, 0)),
            pl.BlockSpec((nb, L, LANES), lambda bi, c: (bi, c, 0)),
            _resident((SSM_CONV, SSM_CONV_DIM), lambda bi, c: (0, 0)),
            row(SSM_CONV_DIM), row(LANES), row(LANES), row(SSM_D_INNER), row(SSM_D_INNER),
            _resident((2 * LANES, SSM_D_INNER), lambda bi, c: (0, 0)),
        ],
        out_specs=pl.BlockSpec((nb, L, SSM_D_INNER), lambda bi, c: (bi, c, 0)),
        out_shape=jax.ShapeDtypeStruct((b, s, SSM_D_INNER), BF16),
        scratch_shapes=[pltpu.VMEM((nb, BF16_SUBLANES, SSM_CONV_DIM), BF16),
                        pltpu.VMEM((nb, SSM_STATE, SSM_D_INNER), F32)],
        compiler_params=_cparams(("parallel", "arbitrary")),
        name="ssd_mixer",
    )(z, xbc, dt_raw, conv_w.T, conv_b.reshape(1, -1), _pad_lanes(dt_bias), _pad_lanes(a_log),
      jnp.repeat(d_skip, HEAD_DIM).reshape(1, -1), norm_g.reshape(1, -1), _head_spread_matrix())


def _split_lane_halves(x2):
    xf = x2.astype(F32)
    lane = lax.broadcasted_iota(jnp.int32, xf.shape, 1)
    lo = jnp.where(lane < HEAD_DIM, xf, 0.0)
    hi = jnp.where(lane >= HEAD_DIM, xf, 0.0)
    return jnp.concatenate([lo, hi], axis=0).astype(BF16)


def _scores_t(kb, qs):
    return lax.dot_general(kb, qs, (((1,), (1,)), ((), ())), preferred_element_type=F32)


def _pv_t(vb, wt):
    return lax.dot_general(vb, wt, (((0,), (0,)), ((), ())), preferred_element_type=F32)


def _sb_kernel(q_ref, k_ref, v_ref, o_ref, z_ref, cs_ref, tot_ref, acc_ref, carry_ref):
    T = q_ref.shape[0]
    n_streams = q_ref.shape[1] // LANES
    i = pl.program_id(2)
    qs = [_split_lane_halves(q_ref[:, p * LANES:(p + 1) * LANES]) for p in range(n_streams)]

    key = lax.broadcasted_iota(jnp.int32, (T, 2 * T), 0)
    qry = lax.broadcasted_iota(jnp.int32, (T, 2 * T), 1)
    strict = key < jnp.where(qry >= T, qry - T, qry)
    ur = lax.broadcasted_iota(jnp.int32, (T, T), 0)
    uc = lax.broadcasted_iota(jnp.int32, (T, T), 1)
    neg_suffix = jnp.where(uc >= ur, -1.0, 0.0).astype(BF16)
    streams = range(n_streams)
    sign_bit = jnp.uint32(0x80000000)

    def scores(j_lo, nb):
        rows = pl.ds(pl.multiple_of(j_lo * T, T), nb * T)
        return [_scores_t(k_ref[rows, p * LANES:(p + 1) * LANES], qs[p]) for p in streams]

    def softplus2(z, diagonal):
        neg_abs = lax.bitcast_convert_type(lax.bitcast_convert_type(z, jnp.uint32) | sign_bit, F32)
        sp = jnp.maximum(z, 0.0) + jnp.log(1.0 + jnp.exp2(neg_abs)) * LOG2E
        if diagonal:
            sp = jnp.where(strict, sp, 0.0)
        return sp.astype(BF16)

    def suffix_sums(sp, nb):
        parts = [jnp.dot(neg_suffix, sp[u * T:(u + 1) * T], preferred_element_type=F32) for u in range(nb)]
        total = parts[nb - 1][0:1, :]
        for u in reversed(range(nb - 1)):
            parts[u] = parts[u] + total
            total = parts[u][0:1, :]
        return (parts[0] if nb == 1 else jnp.concatenate(parts, axis=0)), total

    def weights(z, cs, carry, diagonal):
        w = jnp.exp2(z + (cs + carry))
        if diagonal:
            w = jnp.where(strict, w, 0.0)
        return w.astype(BF16)

    def accumulate(j_lo, nb, ws, tots):
        rows = pl.ds(pl.multiple_of(j_lo * T, T), nb * T)
        for p in streams:
            acc_ref[p] += _pv_t(v_ref[rows, p * LANES:(p + 1) * LANES], ws[p])
            carry_ref[p] += tots[p]

    def direct_step(j_lo, nb, diagonal=False):
        zs = scores(j_lo, nb)
        cts = [suffix_sums(softplus2(z, diagonal), nb) for z in zs]
        ws = [weights(zs[p], cts[p][0], carry_ref[p], diagonal) for p in streams]
        accumulate(j_lo, nb, ws, [ct[1] for ct in cts])

    def finish_produce(slot, zs):
        for p in streams:
            z_ref[slot, p] = zs[p]
        for p in streams:
            cs, tot = suffix_sums(softplus2(z_ref[slot, p], False), 2)
            cs_ref[slot, p] = cs
            tot_ref[slot, p] = tot

    for p in streams:
        acc_ref[p] = jnp.zeros((LANES, 2 * T), F32)
        carry_ref[p] = jnp.zeros((1, 2 * T), F32)
    direct_step(i, 1, diagonal=True)
    n_double = i // 2
    pair_lo = lambda t: i - 2 - 2 * t

    def consume(t, slot, produce_next):
        ws = [weights(z_ref[slot, p], cs_ref[slot, p], carry_ref[p], False) for p in streams]
        tots = [tot_ref[slot, p] for p in streams]
        if produce_next:
            finish_produce(1 - slot, scores(pair_lo(t + 1), 2))
        accumulate(pair_lo(t), 2, ws, tots)

    @pl.when(n_double > 0)
    def _():
        finish_produce(0, scores(pair_lo(0), 2))

    n_twice = jnp.maximum(n_double - 1, 0) // 2

    def body(u, _):
        consume(2 * u, 0, True)
        consume(2 * u + 1, 1, True)
        return 0

    lax.fori_loop(0, n_twice, body, 0)
    left = n_double - 2 * n_twice

    @pl.when(left == 2)
    def _():
        consume(2 * n_twice, 0, True)
        consume(2 * n_twice + 1, 1, False)

    @pl.when(left == 1)
    def _():
        consume(2 * n_twice, 0, False)

    @pl.when(i - 2 * n_double > 0)
    def _():
        direct_step(0, 1)

    for p in range(n_streams):
        acc = acc_ref[p]
        pair_t = jnp.concatenate([acc[:HEAD_DIM, :T], acc[HEAD_DIM:, T:]], axis=0)
        o_ref[:, p * LANES:(p + 1) * LANES] = pair_t.T.astype(BF16)


def _stick_breaking(q, k, v):
    b, s, width = q.shape
    T = min(ATT_BLOCK, s)
    gw = SB_STREAMS * LANES
    return pl.pallas_call(
        _sb_kernel,
        grid=(b, width // gw, s // T),
        in_specs=[
            pl.BlockSpec((None, T, gw), lambda bi, g, i: (bi, i, g)),
            pl.BlockSpec((None, s, gw), lambda bi, g, i: (bi, 0, g)),
            pl.BlockSpec((None, s, gw), lambda bi, g, i: (bi, 0, g)),
        ],
        out_specs=pl.BlockSpec((None, T, gw), lambda bi, g, i: (bi, i, g)),
        out_shape=jax.ShapeDtypeStruct((b, s, width), BF16),
        scratch_shapes=[
            pltpu.VMEM((2, SB_STREAMS, 2 * T, 2 * T), F32),
            pltpu.VMEM((2, SB_STREAMS, 2 * T, 2 * T), F32),
            pltpu.VMEM((2, SB_STREAMS, 1, 2 * T), F32),
            pltpu.VMEM((SB_STREAMS, LANES, 2 * T), F32),
            pltpu.VMEM((SB_STREAMS, 1, 2 * T), F32),
        ],
        compiler_params=_cparams(("parallel", "parallel", "arbitrary")),
        name="stick_breaking_attention",
    )(q, k, v)


def _bucket_starts():
    max_exact = N_REL_BUCKETS // 2
    dist = np.arange(0, 4 * REL_MAX_DIST, dtype=np.float64)
    ratio = np.log(np.maximum(dist, max_exact) / max_exact) / math.log(REL_MAX_DIST / max_exact)
    scaled = ratio * (N_REL_BUCKETS - max_exact)
    frac = np.abs(scaled - np.round(scaled))
    interior = (dist > max_exact) & (dist < REL_MAX_DIST)
    assert frac[interior].min() > 1e-3, "a bucket edge sits on an integer distance"
    large = np.minimum(max_exact + np.floor(scaled + 1e-9).astype(np.int64), N_REL_BUCKETS - 1)
    bucket = np.where(dist < max_exact, dist.astype(np.int64), large)
    assert np.all(np.diff(bucket) >= 0) and bucket[-1] == N_REL_BUCKETS - 1
    return [int(np.argmax(bucket >= bkt)) for bkt in range(N_REL_BUCKETS)]


def _bias_table_kernel(rel_ref, o_ref, *, starts):
    h = pl.program_id(0)
    T = o_ref.shape[-1]
    key = lax.broadcasted_iota(jnp.int32, (T, T), 0)
    qry = lax.broadcasted_iota(jnp.int32, (T, T), 1)
    far = rel_ref[N_REL_BUCKETS - 1, h]
    for o_blk in range(o_ref.shape[0]):
        dist = qry - key + o_blk * T
        bias = jnp.full((T, T), rel_ref[0, h], F32)
        for bkt in range(1, N_REL_BUCKETS):
            bias = jnp.where(dist >= starts[bkt], rel_ref[bkt, h], bias)
        o_ref[o_blk] = jnp.where(dist >= 0, (bias - far) * LOG2E, -jnp.inf)


def _bias_tables(rel_bias, T):
    starts = _bucket_starts()
    assert T + 1 >= starts[-1]
    return pl.pallas_call(
        functools.partial(_bias_table_kernel, starts=starts),
        grid=(DIFF_HEADS,),
        in_specs=[pl.BlockSpec(memory_space=pltpu.SMEM)],
        out_specs=pl.BlockSpec((None, 2, T, T), lambda h: (h, 0, 0, 0)),
        out_shape=jax.ShapeDtypeStruct((DIFF_HEADS, 2, T, T), F32),
        compiler_params=_cparams(("arbitrary",)),
        name="t5_bias_tables",
    )(rel_bias)


def _diff_kernel(q_ref, k_ref, v_ref, tb_ref, lq1_ref, lk1_ref, lq2_ref, lk2_ref, sg_ref, o_ref,
                 s_ref, m_ref, l_ref, acc_ref, *, lambda_init):
    T = q_ref.shape[0]
    n_streams = q_ref.shape[1] // LANES
    i = pl.program_id(2)
    qs = [_split_lane_halves(q_ref[:, h * LANES:(h + 1) * LANES]) for h in range(n_streams)]

    heads = range(n_streams)

    def scores(j_lo, nb):
        rows = pl.ds(pl.multiple_of(j_lo * T, T), nb * T)
        return [_scores_t(k_ref[rows, h * LANES:(h + 1) * LANES], qs[h]) for h in heads]

    def softmax_part(ss, table_rows):
        out = []
        for h in heads:
            read = ss[h] if callable(ss[h]) else (lambda v=ss[h]: v)
            if table_rows is not None:
                biases = [tb_ref[h, o] for o in table_rows]
                bias = biases[0] if len(biases) == 1 else jnp.concatenate(biases, axis=0)
                s = read() + jnp.concatenate([bias, bias], axis=1)
                read = lambda v=s: v
            m_old = m_ref[h]
            m_new = jnp.maximum(m_old, jnp.max(read(), axis=0, keepdims=True))
            alpha = jnp.exp2(m_old - m_new)
            p = jnp.exp2(read() - m_new)
            m_ref[h] = m_new
            l_ref[h] = alpha * l_ref[h] + jnp.sum(p, axis=0, keepdims=True)
            out.append((alpha, p.astype(BF16)))
        return out

    def accumulate(j_lo, nb, aps):
        rows = pl.ds(pl.multiple_of(j_lo * T, T), nb * T)
        for h in heads:
            alpha, p = aps[h]
            acc_ref[h] = alpha * acc_ref[h] + _pv_t(v_ref[rows, h * LANES:(h + 1) * LANES], p)

    for h in heads:
        m_ref[h] = jnp.full((1, 2 * T), -jnp.inf, F32)
        l_ref[h] = jnp.zeros((1, 2 * T), F32)
        acc_ref[h] = jnp.zeros((DIFF_V_DIM, 2 * T), F32)
    n_far = jnp.maximum(i - 1, 0)
    n_double = n_far // 2
    first = n_far - 2 * n_double
    pair_lo = lambda t: first + 2 * t

    def produce(t, slot):
        for h, s in enumerate(scores(pair_lo(t), 2)):
            s_ref[slot, h] = s

    def consume(t, slot, produce_next, table_rows=None):
        aps = softmax_part([lambda h=h: s_ref[slot, h] for h in heads], table_rows)
        if produce_next:
            produce(t + 1, 1 - slot)
        accumulate(pair_lo(t), 2, aps)

    @pl.when(first > 0)
    def _():
        accumulate(0, 1, softmax_part(scores(0, 1), None))

    @pl.when(i > 0)
    def _():
        produce(0, 0)

    n_twice = n_double // 2

    def body(u, _):
        consume(2 * u, 0, True)
        consume(2 * u + 1, 1, True)
        return 0

    lax.fori_loop(0, n_twice, body, 0)
    left = n_double - 2 * n_twice

    @pl.when(jnp.logical_and(i > 0, left == 1))
    def _():
        consume(2 * n_twice, 0, True)
        consume(2 * n_twice + 1, 1, False, [1, 0])

    @pl.when(jnp.logical_and(i > 0, left == 0))
    def _():
        consume(2 * n_twice, 0, False, [1, 0])

    @pl.when(i == 0)
    def _():
        accumulate(0, 1, softmax_part(scores(0, 1), [0]))

    lam = (jnp.exp(jnp.sum(lq1_ref[...] * lk1_ref[...], axis=-1, keepdims=True))
           - jnp.exp(jnp.sum(lq2_ref[...] * lk2_ref[...], axis=-1, keepdims=True)) + lambda_init)
    for h in range(n_streams):
        l, acc = l_ref[h], acc_ref[h]
        o_t = acc[:, :T] / l[:, :T] - lam * (acc[:, T:] / l[:, T:])
        o = o_t.T
        ms = jnp.mean(o * o, axis=-1, keepdims=True)
        o = o * lax.rsqrt(ms + EPS) * sg_ref[...] * (1.0 - lambda_init)
        o_ref[:, h * LANES:(h + 1) * LANES] = o.astype(BF16)


def _diff_attention(q, k, v, tables, lq1, lk1, lq2, lk2, subln_g, lambda_init):
    b, s, width = q.shape
    T = tables.shape[-1]
    vec = lambda n: _resident((1, n), lambda bi, g, i: (0, 0))
    gw = DIFF_STREAMS * LANES
    return pl.pallas_call(
        functools.partial(_diff_kernel, lambda_init=lambda_init),
        grid=(b, width // gw, s // T),
        in_specs=[
            pl.BlockSpec((None, T, gw), lambda bi, g, i: (bi, i, g)),
            pl.BlockSpec((None, s, gw), lambda bi, g, i: (bi, 0, g)),
            pl.BlockSpec((None, s, gw), lambda bi, g, i: (bi, 0, g)),
            pl.BlockSpec((DIFF_STREAMS,) + tables.shape[1:], lambda bi, g, i: (g, 0, 0, 0)),
            vec(HEAD_DIM), vec(HEAD_DIM), vec(HEAD_DIM), vec(HEAD_DIM), vec(DIFF_V_DIM),
        ],
        out_specs=pl.BlockSpec((None, T, gw), lambda bi, g, i: (bi, i, g)),
        out_shape=jax.ShapeDtypeStruct((b, s, width), BF16),
        scratch_shapes=[
            pltpu.VMEM((2, DIFF_STREAMS, 2 * T, 2 * T), F32),
            pltpu.VMEM((DIFF_STREAMS, 1, 2 * T), F32),
            pltpu.VMEM((DIFF_STREAMS, 1, 2 * T), F32),
            pltpu.VMEM((DIFF_STREAMS, DIFF_V_DIM, 2 * T), F32),
        ],
        compiler_params=_cparams(("parallel", "parallel", "arbitrary")),
        name="differential_attention",
    )(q, k, v, tables, lq1.reshape(1, -1), lk1.reshape(1, -1), lq2.reshape(1, -1), lk2.reshape(1, -1),
      subln_g.reshape(1, -1))


def _outproj_kernel(x_ref, mod_ref, ys_ref, yb_ref, yd_ref, w_ref, o_ref):
    o1 = SSM_D_INNER
    o2 = o1 + SB_WIDTH
    y = jnp.dot(ys_ref[...], w_ref[0:o1, :].astype(BF16), preferred_element_type=F32)
    y = y + jnp.dot(yb_ref[...], w_ref[o1:o2, :].astype(BF16), preferred_element_type=F32)
    y = y + jnp.dot(yd_ref[...], w_ref[o2:, :].astype(BF16), preferred_element_type=F32)
    o_ref[...] = x_ref[...] + mod_ref[5:6, :] * y


def _outproj(x, mod, y_ssm, y_sb, y_diff, w_out, layer):
    b, s, d = x.shape
    tm = min(TOKEN_TILE, s)
    tok = lambda width: pl.BlockSpec((None, tm, width), lambda bi, i: (bi, i, 0))
    return pl.pallas_call(
        _outproj_kernel,
        grid=(b, s // tm),
        in_specs=[
            tok(d),
            pl.BlockSpec((None, N_MOD, d), lambda bi, i: (bi, 0, 0)),
            tok(SSM_D_INNER), tok(SB_WIDTH), tok(DIFF_WIDTH),
            _resident((None,) + w_out.shape[1:], lambda bi, i: (layer, 0, 0)),
        ],
        out_specs=tok(d),
        out_shape=jax.ShapeDtypeStruct((b, s, d), F32),
        compiler_params=_cparams(("parallel", "parallel")),
        name="mixer_outproj",
    )(x, mod, y_ssm, y_sb, y_diff, w_out)


def kernel(x, c, ada_w, ada_b, ffn1_norm, ffn1_w13, ffn1_w2, mix_norm, w_in, ssm_conv_w, ssm_conv_b, ssm_dt_bias, ssm_a_log, ssm_d, ssm_norm, diff_lambda_q1, diff_lambda_k1, diff_lambda_q2, diff_lambda_k2, diff_subln, rel_bias, w_out, ffn2_norm, ffn2_w13, ffn2_w2, final_norm):
    depth = ada_w.shape[0]
    s = x.shape[1]
    mods = _ada_modulation(c, ada_w, ada_b)
    tables = _bias_tables(rel_bias, min(ATT_BLOCK, s))
    w_in_packed = _pack_w_in(w_in)
    for l in range(depth):
        mod = mods[l]
        x = _ffn(x, mod, ffn1_norm[l], ffn1_w13, ffn1_w2, l, mod_row=0)
        lambda_init = 0.8 - 0.6 * math.exp(-0.3 * l)
        z, xbc, dt_raw, sq, sk, sv, dq, dk, dv = _inproj(x, mod, mix_norm[l], w_in_packed, l)
        y_ssm = _ssd(z, xbc, dt_raw, ssm_conv_w[l], ssm_conv_b[l], ssm_dt_bias[l], ssm_a_log[l], ssm_d[l],
                     ssm_norm[l])
        y_sb = _stick_breaking(sq, sk, sv)
        y_diff = _diff_attention(dq, dk, dv, tables, diff_lambda_q1[l], diff_lambda_k1[l], diff_lambda_q2[l],
                                 diff_lambda_k2[l], diff_subln[l], lambda_init)
        x = _outproj(x, mod, y_ssm, y_sb, y_diff, w_out, l)
        x = _ffn(x, mod, ffn2_norm[l], ffn2_w13, ffn2_w2, l, mod_row=6,
                 final_g=final_norm if l == depth - 1 else None)
    return x
```

```python
import functools
import math

import numpy as np
import jax
import jax.numpy as jnp
from jax import lax
from jax.experimental import pallas as pl
from jax.experimental.pallas import tpu as pltpu

F32 = jnp.float32
BF16 = jnp.bfloat16
HIGHEST = lax.Precision.HIGHEST
LOG2E = math.log2(math.e)

HEAD_DIM = 64
SSM_HEADS = 16
SSM_GROUPS = 2
SSM_STATE = 128
SSM_CONV = 4
SSM_CHUNK = 128
SSM_D_INNER = SSM_HEADS * HEAD_DIM
SSM_CONV_DIM = SSM_D_INNER + 2 * SSM_GROUPS * SSM_STATE
SB_HEADS = 8
SB_WIDTH = SB_HEADS * HEAD_DIM
DIFF_HEADS = 4
DIFF_V_DIM = 2 * HEAD_DIM
DIFF_WIDTH = DIFF_HEADS * DIFF_V_DIM
N_MOD = 9
N_REL_BUCKETS = 32
REL_MAX_DIST = 128
EPS = 1e-6

LANES = 128
BF16_SUBLANES = 16
VMEM_LIMIT = 56 * 1024 * 1024

TOKEN_TILE = 512
FF_CHUNK = 256
ATT_BLOCK = 256

OFF_Z = 0
OFF_XBC = OFF_Z + SSM_D_INNER
OFF_SQ = OFF_XBC + SSM_CONV_DIM
OFF_SK = OFF_SQ + SB_WIDTH
OFF_SV = OFF_SK + SB_WIDTH
OFF_DQ = OFF_SV + SB_WIDTH
OFF_DK = OFF_DQ + DIFF_WIDTH
OFF_DV = OFF_DK + DIFF_WIDTH
OFF_DT = OFF_DV + DIFF_WIDTH
IN_PACKED = OFF_DT + LANES


def _cparams(semantics):
    return pltpu.CompilerParams(dimension_semantics=semantics, vmem_limit_bytes=VMEM_LIMIT)


SSD_STREAMS = 2
SB_STREAMS = 4
DIFF_STREAMS = 4


def _resident(block_shape, index_map):
    return pl.BlockSpec(block_shape, index_map, pipeline_mode=pl.Buffered(1))


def _silu(x):
    hx = 0.5 * x
    return hx + hx * jnp.tanh(hx)


def _softplus(x):
    return jnp.maximum(x, 0.0) + jnp.log1p(jnp.exp(-jnp.abs(x)))


def _norm_modulate(x, g, shift, scale):
    ms = jnp.mean(x * x, axis=-1, keepdims=True)
    y = x * lax.rsqrt(ms + EPS) * g
    return y * (1.0 + scale) + shift


def _ada_kernel(c_ref, w_ref, b_ref, o_ref):
    cond = _silu(c_ref[...])
    o_ref[...] = jnp.dot(cond, w_ref[...], preferred_element_type=F32, precision=HIGHEST) + b_ref[...]


def _ada_modulation(c, ada_w, ada_b):
    depth, d, nmod = ada_w.shape
    b = c.shape[0]
    rows = 8 * pl.cdiv(b, 8)
    c_pad = jnp.zeros((rows, d), F32).at[:b].set(c)
    out = pl.pallas_call(
        _ada_kernel,
        grid=(depth, nmod // d),
        in_specs=[
            pl.BlockSpec((rows, d), lambda l, j: (0, 0)),
            pl.BlockSpec((None, d, d), lambda l, j: (l, 0, j)),
            pl.BlockSpec((None, 1, d), lambda l, j: (l, 0, j)),
        ],
        out_specs=pl.BlockSpec((None, rows, d), lambda l, j: (l, 0, j)),
        out_shape=jax.ShapeDtypeStruct((depth, rows, nmod), F32),
        compiler_params=_cparams(("arbitrary", "arbitrary")),
        name="ada_modulation",
    )(c_pad, ada_w, ada_b.reshape(depth, 1, nmod))
    return out[:, :b].reshape(depth, b, N_MOD, d)


def _ffn_kernel(x_ref, mod_ref, g_ref, w13_ref, w2_ref, *rest, mod_row, final_norm):
    if final_norm:
        fg_ref, o_ref, acc_ref = rest
    else:
        o_ref, acc_ref = rest
    x = x_ref[...]
    shift = mod_ref[mod_row:mod_row + 1, :]
    scale = mod_ref[mod_row + 1:mod_row + 2, :]
    gate = mod_ref[mod_row + 2:mod_row + 3, :]
    h = _norm_modulate(x, g_ref[...], shift, scale).astype(BF16)
    d_ff = w2_ref.shape[0]
    for j in range(d_ff // FF_CHUNK):
        cols = slice(j * FF_CHUNK, (j + 1) * FF_CHUNK)
        w1 = w13_ref[:, cols].astype(BF16)
        w3 = w13_ref[:, d_ff + j * FF_CHUNK:d_ff + (j + 1) * FF_CHUNK].astype(BF16)
        a = jnp.dot(h, w1, preferred_element_type=F32)
        u = jnp.dot(h, w3, preferred_element_type=F32)
        act = (_silu(a) * u).astype(BF16)
        part = jnp.dot(act, w2_ref[cols, :].astype(BF16), preferred_element_type=F32)
        if j == 0:
            acc_ref[...] = part
        else:
            acc_ref[...] += part
    y = x + (0.5 * gate) * acc_ref[...]
    if final_norm:
        ms = jnp.mean(y * y, axis=-1, keepdims=True)
        y = y * lax.rsqrt(ms + EPS) * fg_ref[...]
    o_ref[...] = y


def _ffn(x, mod, g, w13, w2, layer, mod_row, final_g=None):
    b, s, d = x.shape
    d_ff = w2.shape[1]
    assert d_ff % FF_CHUNK == 0 and d_ff % LANES == 0
    tm = min(TOKEN_TILE, s)
    final_norm = final_g is not None
    in_specs = [
        pl.BlockSpec((None, tm, d), lambda bi, i: (bi, i, 0)),
        pl.BlockSpec((None, N_MOD, d), lambda bi, i: (bi, 0, 0)),
        _resident((1, d), lambda bi, i: (0, 0)),
        _resident((None, d, 2 * d_ff), lambda bi, i: (layer, 0, 0)),
        _resident((None, d_ff, d), lambda bi, i: (layer, 0, 0)),
    ]
    args = [x, mod, g.reshape(1, d), w13, w2]
    if final_norm:
        in_specs.append(_resident((1, d), lambda bi, i: (0, 0)))
        args.append(final_g.reshape(1, d))
    return pl.pallas_call(
        functools.partial(_ffn_kernel, mod_row=mod_row, final_norm=final_norm),
        grid=(b, s // tm),
        in_specs=in_specs,
        out_specs=pl.BlockSpec((None, tm, d), lambda bi, i: (bi, i, 0)),
        out_shape=jax.ShapeDtypeStruct((b, s, d), F32),
        scratch_shapes=[pltpu.VMEM((tm, d), F32)],
        compiler_params=_cparams(("parallel", "parallel")),
        name="ffn_final" if final_norm else "ffn",
    )(*args)


def _inproj_kernel(x_ref, mod_ref, g_ref, w_ref, z_ref, xbc_ref, dt_ref,
                   sq_ref, sk_ref, sv_ref, dq_ref, dk_ref, dv_ref):
    x = x_ref[...]
    h = _norm_modulate(x, g_ref[...], mod_ref[3:4, :], mod_ref[4:5, :]).astype(BF16)

    def proj(off, width):
        return jnp.dot(h, w_ref[:, off:off + width], preferred_element_type=F32)

    qk_scale = HEAD_DIM ** -0.5 * LOG2E
    z_ref[...] = proj(OFF_Z, SSM_D_INNER).astype(BF16)
    xbc_ref[...] = proj(OFF_XBC, SSM_CONV_DIM).astype(BF16)
    dt_ref[...] = proj(OFF_DT, LANES)
    sq_ref[...] = (proj(OFF_SQ, SB_WIDTH) * qk_scale).astype(BF16)
    sk_ref[...] = proj(OFF_SK, SB_WIDTH).astype(BF16)
    sv_ref[...] = proj(OFF_SV, SB_WIDTH).astype(BF16)
    dq_ref[...] = (proj(OFF_DQ, DIFF_WIDTH) * qk_scale).astype(BF16)
    dk_ref[...] = proj(OFF_DK, DIFF_WIDTH).astype(BF16)
    dv_ref[...] = proj(OFF_DV, DIFF_WIDTH).astype(BF16)


def _pack_kernel(w_ref, o_ref):
    dt0 = SSM_D_INNER + SSM_CONV_DIM
    dt1 = dt0 + SSM_HEADS
    w = w_ref[...]
    rows = w.shape[0]
    o_ref[:, :dt0] = w[:, :dt0].astype(BF16)
    o_ref[:, dt0:OFF_DT] = w[:, dt1:].astype(BF16)
    o_ref[:, OFF_DT:] = jnp.concatenate(
        [w[:, dt0:dt1], jnp.zeros((rows, LANES - SSM_HEADS), F32)], axis=1).astype(BF16)


def _pack_w_in(w_in):
    depth, d, n_in = w_in.shape
    rows = 128
    return pl.pallas_call(
        _pack_kernel,
        grid=(depth, d // rows),
        in_specs=[pl.BlockSpec((None, rows, n_in), lambda l, r: (l, r, 0))],
        out_specs=pl.BlockSpec((None, rows, IN_PACKED), lambda l, r: (l, r, 0)),
        out_shape=jax.ShapeDtypeStruct((depth, d, IN_PACKED), BF16),
        compiler_params=_cparams(("arbitrary", "arbitrary")),
        name="pack_w_in",
    )(w_in)


def _inproj(x, mod, g, w, layer):
    b, s, d = x.shape
    tm = min(TOKEN_TILE, s)
    widths = (SSM_D_INNER, SSM_CONV_DIM, LANES) + (SB_WIDTH,) * 3 + (DIFF_WIDTH,) * 3
    dtypes = (BF16, BF16, F32) + (BF16,) * 6
    return pl.pallas_call(
        _inproj_kernel,
        grid=(b, s // tm),
        in_specs=[
            pl.BlockSpec((None, tm, d), lambda bi, i: (bi, i, 0)),
            pl.BlockSpec((None, N_MOD, d), lambda bi, i: (bi, 0, 0)),
            _resident((1, d), lambda bi, i: (0, 0)),
            _resident((None, d, IN_PACKED), lambda bi, i: (layer, 0, 0)),
        ],
        out_specs=[pl.BlockSpec((None, tm, wd), lambda bi, i: (bi, i, 0)) for wd in widths],
        out_shape=[jax.ShapeDtypeStruct((b, s, wd), dt) for wd, dt in zip(widths, dtypes)],
        compiler_params=_cparams(("parallel", "parallel")),
        name="mixer_inproj",
    )(x, mod, g.reshape(1, d), w)


def _ssd_kernel(z_ref, xbc_ref, dtr_ref, cw_ref, cb_ref, dtb_ref, alog_ref, dsk_ref, ng_ref, e_ref,
                y_ref, tail_ref, state_ref):
    rows = range(z_ref.shape[0])
    convs = [_ssd_conv(xbc_ref.at[bb], cw_ref, cb_ref, tail_ref.at[bb], state_ref.at[bb]) for bb in rows]
    for bb in rows:
        _ssd_scan(convs[bb], z_ref.at[bb], dtr_ref.at[bb], dtb_ref, alog_ref, dsk_ref, ng_ref, e_ref,
                  y_ref.at[bb], state_ref.at[bb])


def _ssd_conv(xbc_ref, cw_ref, cb_ref, tail_ref, state_ref):
    L = xbc_ref.shape[0]
    pad = tail_ref.shape[0]
    c = pl.program_id(1)

    @pl.when(c == 0)
    def _():
        tail_ref[...] = jnp.zeros(tail_ref.shape, BF16)
        state_ref[...] = jnp.zeros(state_ref.shape, F32)

    u = xbc_ref[...]
    u_ext = jnp.concatenate([tail_ref[...], u], axis=0)
    tail_ref[...] = u[L - pad:, :]
    trow = lax.broadcasted_iota(jnp.int32, (L, pad + L), 0)
    tcol = lax.broadcasted_iota(jnp.int32, (L, pad + L), 1)
    shifts = jnp.concatenate(
        [jnp.where(tcol == trow + (pad - (SSM_CONV - 1) + k), 1.0, 0.0) for k in range(SSM_CONV)], axis=0)
    shifted = jnp.dot(shifts.astype(BF16), u_ext, preferred_element_type=F32)
    conv = cb_ref[...]
    for k in range(SSM_CONV):
        conv = conv + cw_ref[k:k + 1, :] * shifted[k * L:(k + 1) * L]
    return _silu(conv)


def _ssd_scan(xa, z_ref, dtr_ref, dtb_ref, alog_ref, dsk_ref, ng_ref, e_ref, y_ref, state_ref):
    L = z_ref.shape[0]
    xs = xa[:, :SSM_D_INNER]
    bmat = xa[:, SSM_D_INNER:SSM_D_INNER + SSM_GROUPS * SSM_STATE]
    cmat = xa[:, SSM_D_INNER + SSM_GROUPS * SSM_STATE:]

    dt = _softplus(dtr_ref[...] + dtb_ref[...])
    a = -jnp.exp(alog_ref[...])
    da = dt * a
    row = lax.broadcasted_iota(jnp.int32, (L, L), 0)
    col = lax.broadcasted_iota(jnp.int32, (L, L), 1)
    causal = row >= col
    a_cum = jnp.dot(causal.astype(F32), da, preferred_element_type=F32, precision=HIGHEST)
    a_cum_t = a_cum.T
    exp_a = jnp.exp(a_cum)
    decay_to_end = jnp.exp(a_cum[L - 1:L, :] - a_cum)

    stacked = jnp.concatenate([dt, exp_a, decay_to_end], axis=0)
    hi = stacked.astype(BF16)
    lo = (stacked - hi.astype(F32)).astype(BF16)
    spread = jnp.dot(jnp.concatenate([hi, lo], axis=1), e_ref[...], preferred_element_type=F32)
    dt_e = spread[0:L]
    exp_a_e = spread[L:2 * L]
    dte_e = spread[2 * L:3 * L]

    xd = xs * dt_e
    xd_b = xd.astype(BF16)
    xdd_b = (xd * dte_e).astype(BF16)
    prev_b = state_ref[...].astype(BF16)
    lane = lax.broadcasted_iota(jnp.int32, (L, LANES), 1)
    heads_per_group = SSM_HEADS // SSM_GROUPS
    group_width = heads_per_group * HEAD_DIM

    y_diag_parts = []
    y_off_parts = []
    s_new_parts = []
    for g in range(SSM_GROUPS):
        bg = bmat[:, g * SSM_STATE:(g + 1) * SSM_STATE]
        cg_b = cmat[:, g * SSM_STATE:(g + 1) * SSM_STATE].astype(BF16)
        bg_b = bg.astype(BF16)
        bg_t_b = bg.T.astype(BF16)
        gs = slice(g * group_width, (g + 1) * group_width)
        cb = lax.dot_general(cg_b, bg_b, (((1,), (1,)), ((), ())), preferred_element_type=F32)
        y_off_parts.append(jnp.dot(cg_b, prev_b[:, gs], preferred_element_type=F32))
        s_new_parts.append(jnp.dot(bg_t_b, xdd_b[:, gs], preferred_element_type=F32))
        for hp in range(heads_per_group // 2):
            ms = []
            for h in (g * heads_per_group + 2 * hp, g * heads_per_group + 2 * hp + 1):
                seg = a_cum[:, h:h + 1] - a_cum_t[h:h + 1, :]
                decay = jnp.exp(jnp.where(causal, seg, -jnp.inf))
                ms.append((cb * decay).astype(BF16))
            pair = (g * heads_per_group) // 2 + hp
            yp = jnp.dot(jnp.concatenate(ms, axis=0), xd_b[:, pair * LANES:(pair + 1) * LANES],
                         preferred_element_type=F32)
            y_diag_parts.append(jnp.where(lane < HEAD_DIM, yp[:L], yp[L:]))
    y_diag = jnp.concatenate(y_diag_parts, axis=1)
    y_off = jnp.concatenate(y_off_parts, axis=1)
    s_new = jnp.concatenate(s_new_parts, axis=1)

    y = y_diag + y_off * exp_a_e + dsk_ref[...] * xs
    state_ref[...] = state_ref[...] * exp_a_e[L - 1:L, :] + s_new

    y = y * _silu(z_ref[...].astype(F32))
    outs = []
    for g in range(SSM_GROUPS):
        yg = y[:, g * group_width:(g + 1) * group_width]
        ms = jnp.mean(yg * yg, axis=-1, keepdims=True)
        outs.append(yg * lax.rsqrt(ms + EPS))
    y_ref[...] = (jnp.concatenate(outs, axis=1) * ng_ref[...]).astype(BF16)


def _head_spread_matrix():
    e = np.zeros((2 * LANES, SSM_D_INNER), np.float32)
    for h in range(SSM_HEADS):
        e[h, h * HEAD_DIM:(h + 1) * HEAD_DIM] = 1.0
        e[LANES + h, h * HEAD_DIM:(h + 1) * HEAD_DIM] = 1.0
    return jnp.asarray(e, BF16)


def _pad_lanes(v):
    return jnp.zeros((1, LANES), F32).at[0, :v.shape[0]].set(v)


def _ssd(z, xbc, dt_raw, conv_w, conv_b, dt_bias, a_log, d_skip, norm_g):
    b, s, _ = z.shape
    L = SSM_CHUNK
    nb = SSD_STREAMS if b % SSD_STREAMS == 0 else 1
    row = lambda width: _resident((1, width), lambda bi, c: (0, 0))
    return pl.pallas_call(
        _ssd_kernel,
        grid=(b // nb, s // L),
        in_specs=[
            pl.BlockSpec((nb, L, SSM_D_INNER), lambda bi, c: (bi, c, 0)),
            pl.BlockSpec((nb, L, SSM_CONV_DIM), lambda bi, c: (bi, c, 0)),
            pl.BlockSpec((nb, L, LANES), lambda bi, c: (bi, c, 0)),
            _resident((SSM_CONV, SSM_CONV_DIM), lambda bi, c: (0, 0)),
            row(SSM_CONV_DIM), row(LANES), row(LANES), row(SSM_D_INNER), row(SSM_D_INNER),
            _resident((2 * LANES, SSM_D_INNER), lambda bi, c: (0, 0)),
        ],
        out_specs=pl.BlockSpec((nb, L, SSM_D_INNER), lambda bi, c: (bi, c, 0)),
        out_shape=jax.ShapeDtypeStruct((b, s, SSM_D_INNER), BF16),
        scratch_shapes=[pltpu.VMEM((nb, BF16_SUBLANES, SSM_CONV_DIM), BF16),
                        pltpu.VMEM((nb, SSM_STATE, SSM_D_INNER), F32)],
        compiler_params=_cparams(("parallel", "arbitrary")),
        name="ssd_mixer",
    )(z, xbc, dt_raw, conv_w.T, conv_b.reshape(1, -1), _pad_lanes(dt_bias), _pad_lanes(a_log),
      jnp.repeat(d_skip, HEAD_DIM).reshape(1, -1), norm_g.reshape(1, -1), _head_spread_matrix())


def _split_lane_halves(x2):
    xf = x2.astype(F32)
    lane = lax.broadcasted_iota(jnp.int32, xf.shape, 1)
    lo = jnp.where(lane < HEAD_DIM, xf, 0.0)
    hi = jnp.where(lane >= HEAD_DIM, xf, 0.0)
    return jnp.concatenate([lo, hi], axis=0).astype(BF16)


def _scores_t(kb, qs):
    return lax.dot_general(kb, qs, (((1,), (1,)), ((), ())), preferred_element_type=F32)


def _pv_t(vb, wt):
    return lax.dot_general(vb, wt, (((0,), (0,)), ((), ())), preferred_element_type=F32)


def _sb_kernel(q_ref, k_ref, v_ref, o_ref, z_ref, cs_ref, tot_ref, acc_ref, carry_ref):
    T = q_ref.shape[0]
    n_streams = q_ref.shape[1] // LANES
    i = pl.program_id(2)
    qs = [_split_lane_halves(q_ref[:, p * LANES:(p + 1) * LANES]) for p in range(n_streams)]

    key = lax.broadcasted_iota(jnp.int32, (T, 2 * T), 0)
    qry = lax.broadcasted_iota(jnp.int32, (T, 2 * T), 1)
    strict = key < jnp.where(qry >= T, qry - T, qry)
    ur = lax.broadcasted_iota(jnp.int32, (T, T), 0)
    uc = lax.broadcasted_iota(jnp.int32, (T, T), 1)
    neg_suffix = jnp.where(uc >= ur, -1.0, 0.0).astype(BF16)
    streams = range(n_streams)
    sign_bit = jnp.uint32(0x80000000)

    def scores(j_lo, nb):
        rows = pl.ds(pl.multiple_of(j_lo * T, T), nb * T)
        return [_scores_t(k_ref[rows, p * LANES:(p + 1) * LANES], qs[p]) for p in streams]

    def softplus2(z, diagonal):
        neg_abs = lax.bitcast_convert_type(lax.bitcast_convert_type(z, jnp.uint32) | sign_bit, F32)
        sp = jnp.maximum(z, 0.0) + jnp.log(1.0 + jnp.exp2(neg_abs)) * LOG2E
        if diagonal:
            sp = jnp.where(strict, sp, 0.0)
        return sp.astype(BF16)

    def suffix_sums(sp, nb):
        parts = [jnp.dot(neg_suffix, sp[u * T:(u + 1) * T], preferred_element_type=F32) for u in range(nb)]
        rows = [None] * nb
        total = parts[nb - 1][0:1, :]
        for u in reversed(range(nb - 1)):
            rows[u] = total
            total = total + parts[u][0:1, :]
        rows[nb - 1] = total
        return parts, (rows[0] if nb == 1 else jnp.concatenate(rows, axis=0))

    def weights(z, parts, after_rows, carry, diagonal):
        nb = len(parts)
        ws = []
        for u in range(nb):
            later = carry if u == nb - 1 else carry + after_rows[u:u + 1, :]
            w = jnp.exp2(z[u * T:(u + 1) * T] + (parts[u] + later))
            if diagonal:
                w = jnp.where(strict, w, 0.0)
            ws.append(w.astype(BF16))
        return ws[0] if nb == 1 else jnp.concatenate(ws, axis=0)

    def accumulate(j_lo, nb, ws, tots):
        rows = pl.ds(pl.multiple_of(j_lo * T, T), nb * T)
        for p in streams:
            acc_ref[p] += _pv_t(v_ref[rows, p * LANES:(p + 1) * LANES], ws[p])
            carry_ref[p] += tots[p]

    def direct_step(j_lo, nb, diagonal=False):
        zs = scores(j_lo, nb)
        cts = [suffix_sums(softplus2(z, diagonal), nb) for z in zs]
        ws = [weights(zs[p], cts[p][0], cts[p][1], carry_ref[p], diagonal) for p in streams]
        accumulate(j_lo, nb, ws, [ct[1][nb - 1:nb, :] for ct in cts])

    def finish_produce(slot, zs):
        for p in streams:
            z_ref[slot, p] = zs[p]
        for p in streams:
            parts, rows = suffix_sums(softplus2(z_ref[slot, p], False), 2)
            cs_ref[slot, p] = jnp.concatenate(parts, axis=0)
            tot_ref[slot, p] = rows

    for p in streams:
        acc_ref[p] = jnp.zeros((LANES, 2 * T), F32)
        carry_ref[p] = jnp.zeros((1, 2 * T), F32)
    direct_step(i, 1, diagonal=True)
    n_double = i // 2
    pair_lo = lambda t: i - 2 - 2 * t

    def consume(t, slot, produce_next):
        ws = [weights(z_ref[slot, p], [cs_ref[slot, p, 0:T, :], cs_ref[slot, p, T:2 * T, :]],
                      tot_ref[slot, p], carry_ref[p], False) for p in streams]
        tots = [tot_ref[slot, p, 1:2, :] for p in streams]
        if produce_next:
            finish_produce(1 - slot, scores(pair_lo(t + 1), 2))
        accumulate(pair_lo(t), 2, ws, tots)

    @pl.when(n_double > 0)
    def _():
        finish_produce(0, scores(pair_lo(0), 2))

    n_twice = jnp.maximum(n_double - 1, 0) // 2

    def body(u, _):
        consume(2 * u, 0, True)
        consume(2 * u + 1, 1, True)
        return 0

    lax.fori_loop(0, n_twice, body, 0)
    left = n_double - 2 * n_twice

    @pl.when(left == 2)
    def _():
        consume(2 * n_twice, 0, True)
        consume(2 * n_twice + 1, 1, False)

    @pl.when(left == 1)
    def _():
        consume(2 * n_twice, 0, False)

    @pl.when(i - 2 * n_double > 0)
    def _():
        direct_step(0, 1)

    for p in range(n_streams):
        acc = acc_ref[p]
        pair_t = jnp.concatenate([acc[:HEAD_DIM, :T], acc[HEAD_DIM:, T:]], axis=0)
        o_ref[:, p * LANES:(p + 1) * LANES] = pair_t.T.astype(BF16)


def _stick_breaking(q, k, v):
    b, s, width = q.shape
    T = min(ATT_BLOCK, s)
    gw = SB_STREAMS * LANES
    return pl.pallas_call(
        _sb_kernel,
        grid=(b, width // gw, s // T),
        in_specs=[
            pl.BlockSpec((None, T, gw), lambda bi, g, i: (bi, i, g)),
            pl.BlockSpec((None, s, gw), lambda bi, g, i: (bi, 0, g)),
            pl.BlockSpec((None, s, gw), lambda bi, g, i: (bi, 0, g)),
        ],
        out_specs=pl.BlockSpec((None, T, gw), lambda bi, g, i: (bi, i, g)),
        out_shape=jax.ShapeDtypeStruct((b, s, width), BF16),
        scratch_shapes=[
            pltpu.VMEM((2, SB_STREAMS, 2 * T, 2 * T), F32),
            pltpu.VMEM((2, SB_STREAMS, 2 * T, 2 * T), F32),
            pltpu.VMEM((2, SB_STREAMS, 2, 2 * T), F32),
            pltpu.VMEM((SB_STREAMS, LANES, 2 * T), F32),
            pltpu.VMEM((SB_STREAMS, 1, 2 * T), F32),
        ],
        compiler_params=_cparams(("parallel", "parallel", "arbitrary")),
        name="stick_breaking_attention",
    )(q, k, v)


def _bucket_starts():
    max_exact = N_REL_BUCKETS // 2
    dist = np.arange(0, 4 * REL_MAX_DIST, dtype=np.float64)
    ratio = np.log(np.maximum(dist, max_exact) / max_exact) / math.log(REL_MAX_DIST / max_exact)
    scaled = ratio * (N_REL_BUCKETS - max_exact)
    frac = np.abs(scaled - np.round(scaled))
    interior = (dist > max_exact) & (dist < REL_MAX_DIST)
    assert frac[interior].min() > 1e-3, "a bucket edge sits on an integer distance"
    large = np.minimum(max_exact + np.floor(scaled + 1e-9).astype(np.int64), N_REL_BUCKETS - 1)
    bucket = np.where(dist < max_exact, dist.astype(np.int64), large)
    assert np.all(np.diff(bucket) >= 0) and bucket[-1] == N_REL_BUCKETS - 1
    return [int(np.argmax(bucket >= bkt)) for bkt in range(N_REL_BUCKETS)]


def _bias_table_kernel(rel_ref, o_ref, *, starts):
    h = pl.program_id(0)
    T = o_ref.shape[-1]
    key = lax.broadcasted_iota(jnp.int32, (T, T), 0)
    qry = lax.broadcasted_iota(jnp.int32, (T, T), 1)
    far = rel_ref[N_REL_BUCKETS - 1, h]
    for o_blk in range(o_ref.shape[0]):
        dist = qry - key + o_blk * T
        bias = jnp.full((T, T), rel_ref[0, h], F32)
        for bkt in range(1, N_REL_BUCKETS):
            bias = jnp.where(dist >= starts[bkt], rel_ref[bkt, h], bias)
        o_ref[o_blk] = jnp.where(dist >= 0, (bias - far) * LOG2E, -jnp.inf)


def _bias_tables(rel_bias, T):
    starts = _bucket_starts()
    assert T + 1 >= starts[-1]
    return pl.pallas_call(
        functools.partial(_bias_table_kernel, starts=starts),
        grid=(DIFF_HEADS,),
        in_specs=[pl.BlockSpec(memory_space=pltpu.SMEM)],
        out_specs=pl.BlockSpec((None, 2, T, T), lambda h: (h, 0, 0, 0)),
        out_shape=jax.ShapeDtypeStruct((DIFF_HEADS, 2, T, T), F32),
        compiler_params=_cparams(("arbitrary",)),
        name="t5_bias_tables",
    )(rel_bias)


def _diff_kernel(q_ref, k_ref, v_ref, tb_ref, lq1_ref, lk1_ref, lq2_ref, lk2_ref, sg_ref, o_ref,
                 s_ref, m_ref, l_ref, acc_ref, *, lambda_init):
    T = q_ref.shape[0]
    n_streams = q_ref.shape[1] // LANES
    i = pl.program_id(2)
    qs = [_split_lane_halves(q_ref[:, h * LANES:(h + 1) * LANES]) for h in range(n_streams)]

    heads = range(n_streams)

    def scores(j_lo, nb):
        rows = pl.ds(pl.multiple_of(j_lo * T, T), nb * T)
        return [_scores_t(k_ref[rows, h * LANES:(h + 1) * LANES], qs[h]) for h in heads]

    def softmax_part(ss, table_rows):
        out = []
        for h in heads:
            read = ss[h] if callable(ss[h]) else (lambda v=ss[h]: v)
            if table_rows is not None:
                biases = [tb_ref[h, o] for o in table_rows]
                bias = biases[0] if len(biases) == 1 else jnp.concatenate(biases, axis=0)
                s = read() + jnp.concatenate([bias, bias], axis=1)
                read = lambda v=s: v
            m_old = m_ref[h]
            m_new = jnp.maximum(m_old, jnp.max(read(), axis=0, keepdims=True))
            alpha = jnp.exp2(m_old - m_new)
            p = jnp.exp2(read() - m_new)
            m_ref[h] = m_new
            l_ref[h] = alpha * l_ref[h] + jnp.sum(p, axis=0, keepdims=True)
            out.append((alpha, p.astype(BF16)))
        return out

    def accumulate(j_lo, nb, aps):
        rows = pl.ds(pl.multiple_of(j_lo * T, T), nb * T)
        for h in heads:
            alpha, p = aps[h]
            acc_ref[h] = alpha * acc_ref[h] + _pv_t(v_ref[rows, h * LANES:(h + 1) * LANES], p)

    for h in heads:
        m_ref[h] = jnp.full((1, 2 * T), -jnp.inf, F32)
        l_ref[h] = jnp.zeros((1, 2 * T), F32)
        acc_ref[h] = jnp.zeros((DIFF_V_DIM, 2 * T), F32)
    n_far = jnp.maximum(i - 1, 0)
    n_double = n_far // 2
    first = n_far - 2 * n_double
    pair_lo = lambda t: first + 2 * t

    def produce(t, slot):
        for h, s in enumerate(scores(pair_lo(t), 2)):
            s_ref[slot, h] = s

    def consume(t, slot, produce_next, table_rows=None):
        aps = softmax_part([lambda h=h: s_ref[slot, h] for h in heads], table_rows)
        if produce_next:
            produce(t + 1, 1 - slot)
        accumulate(pair_lo(t), 2, aps)

    @pl.when(first > 0)
    def _():
        accumulate(0, 1, softmax_part(scores(0, 1), None))

    @pl.when(i > 0)
    def _():
        produce(0, 0)

    n_twice = n_double // 2

    def body(u, _):
        consume(2 * u, 0, True)
        consume(2 * u + 1, 1, True)
        return 0

    lax.fori_loop(0, n_twice, body, 0)
    left = n_double - 2 * n_twice

    @pl.when(jnp.logical_and(i > 0, left == 1))
    def _():
        consume(2 * n_twice, 0, True)
        consume(2 * n_twice + 1, 1, False, [1, 0])

    @pl.when(jnp.logical_and(i > 0, left == 0))
    def _():
        consume(2 * n_twice, 0, False, [1, 0])

    @pl.when(i == 0)
    def _():
        accumulate(0, 1, softmax_part(scores(0, 1), [0]))

    lam = (jnp.exp(jnp.sum(lq1_ref[...] * lk1_ref[...], axis=-1, keepdims=True))
           - jnp.exp(jnp.sum(lq2_ref[...] * lk2_ref[...], axis=-1, keepdims=True)) + lambda_init)
    for h in range(n_streams):
        l, acc = l_ref[h], acc_ref[h]
        o_t = acc[:, :T] / l[:, :T] - lam * (acc[:, T:] / l[:, T:])
        o = o_t.T
        ms = jnp.mean(o * o, axis=-1, keepdims=True)
        o = o * lax.rsqrt(ms + EPS) * sg_ref[...] * (1.0 - lambda_init)
        o_ref[:, h * LANES:(h + 1) * LANES] = o.astype(BF16)


def _diff_attention(q, k, v, tables, lq1, lk1, lq2, lk2, subln_g, lambda_init):
    b, s, width = q.shape
    T = tables.shape[-1]
    vec = lambda n: _resident((1, n), lambda bi, g, i: (0, 0))
    gw = DIFF_STREAMS * LANES
    return pl.pallas_call(
        functools.partial(_diff_kernel, lambda_init=lambda_init),
        grid=(b, width // gw, s // T),
        in_specs=[
            pl.BlockSpec((None, T, gw), lambda bi, g, i: (bi, i, g)),
            pl.BlockSpec((None, s, gw), lambda bi, g, i: (bi, 0, g)),
            pl.BlockSpec((None, s, gw), lambda bi, g, i: (bi, 0, g)),
            pl.BlockSpec((DIFF_STREAMS,) + tables.shape[1:], lambda bi, g, i: (g, 0, 0, 0)),
            vec(HEAD_DIM), vec(HEAD_DIM), vec(HEAD_DIM), vec(HEAD_DIM), vec(DIFF_V_DIM),
        ],
        out_specs=pl.BlockSpec((None, T, gw), lambda bi, g, i: (bi, i, g)),
        out_shape=jax.ShapeDtypeStruct((b, s, width), BF16),
        scratch_shapes=[
            pltpu.VMEM((2, DIFF_STREAMS, 2 * T, 2 * T), F32),
            pltpu.VMEM((DIFF_STREAMS, 1, 2 * T), F32),
            pltpu.VMEM((DIFF_STREAMS, 1, 2 * T), F32),
            pltpu.VMEM((DIFF_STREAMS, DIFF_V_DIM, 2 * T), F32),
        ],
        compiler_params=_cparams(("parallel", "parallel", "arbitrary")),
        name="differential_attention",
    )(q, k, v, tables, lq1.reshape(1, -1), lk1.reshape(1, -1), lq2.reshape(1, -1), lk2.reshape(1, -1),
      subln_g.reshape(1, -1))


def _outproj_kernel(x_ref, mod_ref, ys_ref, yb_ref, yd_ref, w_ref, o_ref):
    o1 = SSM_D_INNER
    o2 = o1 + SB_WIDTH
    y = jnp.dot(ys_ref[...], w_ref[0:o1, :].astype(BF16), preferred_element_type=F32)
    y = y + jnp.dot(yb_ref[...], w_ref[o1:o2, :].astype(BF16), preferred_element_type=F32)
    y = y + jnp.dot(yd_ref[...], w_ref[o2:, :].astype(BF16), preferred_element_type=F32)
    o_ref[...] = x_ref[...] + mod_ref[5:6, :] * y


def _outproj(x, mod, y_ssm, y_sb, y_diff, w_out, layer):
    b, s, d = x.shape
    tm = min(TOKEN_TILE, s)
    tok = lambda width: pl.BlockSpec((None, tm, width), lambda bi, i: (bi, i, 0))
    return pl.pallas_call(
        _outproj_kernel,
        grid=(b, s // tm),
        in_specs=[
            tok(d),
            pl.BlockSpec((None, N_MOD, d), lambda bi, i: (bi, 0, 0)),
            tok(SSM_D_INNER), tok(SB_WIDTH), tok(DIFF_WIDTH),
            _resident((None,) + w_out.shape[1:], lambda bi, i: (layer, 0, 0)),
        ],
        out_specs=tok(d),
        out_shape=jax.ShapeDtypeStruct((b, s, d), F32),
        compiler_params=_cparams(("parallel", "parallel")),
        name="mixer_outproj",
    )(x, mod, y_ssm, y_sb, y_diff, w_out)


def kernel(x, c, ada_w, ada_b, ffn1_norm, ffn1_w13, ffn1_w2, mix_norm, w_in, ssm_conv_w, ssm_conv_b, ssm_dt_bias, ssm_a_log, ssm_d, ssm_norm, diff_lambda_q1, diff_lambda_k1, diff_lambda_q2, diff_lambda_k2, diff_subln, rel_bias, w_out, ffn2_norm, ffn2_w13, ffn2_w2, final_norm):
    depth = ada_w.shape[0]
    s = x.shape[1]
    mods = _ada_modulation(c, ada_w, ada_b)
    tables = _bias_tables(rel_bias, min(ATT_BLOCK, s))
    w_in_packed = _pack_w_in(w_in)
    for l in range(depth):
        mod = mods[l]
        x = _ffn(x, mod, ffn1_norm[l], ffn1_w13, ffn1_w2, l, mod_row=0)
        lambda_init = 0.8 - 0.6 * math.exp(-0.3 * l)
        z, xbc, dt_raw, sq, sk, sv, dq, dk, dv = _inproj(x, mod, mix_norm[l], w_in_packed, l)
        y_ssm = _ssd(z, xbc, dt_raw, ssm_conv_w[l], ssm_conv_b[l], ssm_dt_bias[l], ssm_a_log[l], ssm_d[l],
                     ssm_norm[l])
        y_sb = _stick_breaking(sq, sk, sv)
        y_diff = _diff_attention(dq, dk, dv, tables, diff_lambda_q1[l], diff_lambda_k1[l], diff_lambda_q2[l],
                                 diff_lambda_k2[l], diff_subln[l], lambda_init)
        x = _outproj(x, mod, y_ssm, y_sb, y_diff, w_out, l)
        x = _ffn(x, mod, ffn2_norm[l], ffn2_w13, ffn2_w2, l, mod_row=6,
                 final_g=final_norm if l == depth - 1 else None)
    return x
```

```python
import functools
import math

import numpy as np
import jax
import jax.numpy as jnp
from jax import lax
from jax.experimental import pallas as pl
from jax.experimental.pallas import tpu as pltpu

F32 = jnp.float32
BF16 = jnp.bfloat16
HIGHEST = lax.Precision.HIGHEST
LOG2E = math.log2(math.e)

HEAD_DIM = 64
SSM_HEADS = 16
SSM_GROUPS = 2
SSM_STATE = 128
SSM_CONV = 4
SSM_CHUNK = 128
SSM_D_INNER = SSM_HEADS * HEAD_DIM
SSM_CONV_DIM = SSM_D_INNER + 2 * SSM_GROUPS * SSM_STATE
SB_HEADS = 8
SB_WIDTH = SB_HEADS * HEAD_DIM
DIFF_HEADS = 4
DIFF_V_DIM = 2 * HEAD_DIM
DIFF_WIDTH = DIFF_HEADS * DIFF_V_DIM
N_MOD = 9
N_REL_BUCKETS = 32
REL_MAX_DIST = 128
EPS = 1e-6

LANES = 128
BF16_SUBLANES = 16
VMEM_LIMIT = 56 * 1024 * 1024

TOKEN_TILE = 512
FF_CHUNK = 256
ATT_BLOCK = 256

OFF_Z = 0
OFF_XBC = OFF_Z + SSM_D_INNER
OFF_SQ = OFF_XBC + SSM_CONV_DIM
OFF_SK = OFF_SQ + SB_WIDTH
OFF_SV = OFF_SK + SB_WIDTH
OFF_DQ = OFF_SV + SB_WIDTH
OFF_DK = OFF_DQ + DIFF_WIDTH
OFF_DV = OFF_DK + DIFF_WIDTH
OFF_DT = OFF_DV + DIFF_WIDTH
IN_PACKED = OFF_DT + LANES


def _cparams(semantics):
    return pltpu.CompilerParams(dimension_semantics=semantics, vmem_limit_bytes=VMEM_LIMIT)


SSD_STREAMS = 2
SB_STREAMS = 4
DIFF_STREAMS = 4


def _resident(block_shape, index_map):
    return pl.BlockSpec(block_shape, index_map, pipeline_mode=pl.Buffered(1))


def _silu(x):
    hx = 0.5 * x
    return hx + hx * jnp.tanh(hx)


def _softplus(x):
    return jnp.maximum(x, 0.0) + jnp.log1p(jnp.exp(-jnp.abs(x)))


def _norm_modulate(x, g, shift, scale):
    ms = jnp.mean(x * x, axis=-1, keepdims=True)
    y = x * lax.rsqrt(ms + EPS) * g
    return y * (1.0 + scale) + shift


def _ada_kernel(c_ref, w_ref, b_ref, o_ref):
    cond = _silu(c_ref[...])
    o_ref[...] = jnp.dot(cond, w_ref[...], preferred_element_type=F32, precision=HIGHEST) + b_ref[...]


def _ada_modulation(c, ada_w, ada_b):
    depth, d, nmod = ada_w.shape
    b = c.shape[0]
    rows = 8 * pl.cdiv(b, 8)
    c_pad = jnp.zeros((rows, d), F32).at[:b].set(c)
    out = pl.pallas_call(
        _ada_kernel,
        grid=(depth, nmod // d),
        in_specs=[
            pl.BlockSpec((rows, d), lambda l, j: (0, 0)),
            pl.BlockSpec((None, d, d), lambda l, j: (l, 0, j)),
            pl.BlockSpec((None, 1, d), lambda l, j: (l, 0, j)),
        ],
        out_specs=pl.BlockSpec((None, rows, d), lambda l, j: (l, 0, j)),
        out_shape=jax.ShapeDtypeStruct((depth, rows, nmod), F32),
        compiler_params=_cparams(("arbitrary", "arbitrary")),
        name="ada_modulation",
    )(c_pad, ada_w, ada_b.reshape(depth, 1, nmod))
    return out[:, :b].reshape(depth, b, N_MOD, d)


def _ffn_kernel(x_ref, mod_ref, g_ref, w13_ref, w2_ref, *rest, mod_row, final_norm):
    if final_norm:
        fg_ref, o_ref, acc_ref = rest
    else:
        o_ref, acc_ref = rest
    x = x_ref[...]
    shift = mod_ref[mod_row:mod_row + 1, :]
    scale = mod_ref[mod_row + 1:mod_row + 2, :]
    gate = mod_ref[mod_row + 2:mod_row + 3, :]
    h = _norm_modulate(x, g_ref[...], shift, scale).astype(BF16)
    d_ff = w2_ref.shape[0]
    for j in range(d_ff // FF_CHUNK):
        cols = slice(j * FF_CHUNK, (j + 1) * FF_CHUNK)
        w1 = w13_ref[:, cols].astype(BF16)
        w3 = w13_ref[:, d_ff + j * FF_CHUNK:d_ff + (j + 1) * FF_CHUNK].astype(BF16)
        a = jnp.dot(h, w1, preferred_element_type=F32)
        u = jnp.dot(h, w3, preferred_element_type=F32)
        act = (_silu(a) * u).astype(BF16)
        part = jnp.dot(act, w2_ref[cols, :].astype(BF16), preferred_element_type=F32)
        if j == 0:
            acc_ref[...] = part
        else:
            acc_ref[...] += part
    y = x + (0.5 * gate) * acc_ref[...]
    if final_norm:
        ms = jnp.mean(y * y, axis=-1, keepdims=True)
        y = y * lax.rsqrt(ms + EPS) * fg_ref[...]
    o_ref[...] = y


def _ffn(x, mod, g, w13, w2, layer, mod_row, final_g=None):
    b, s, d = x.shape
    d_ff = w2.shape[1]
    assert d_ff % FF_CHUNK == 0 and d_ff % LANES == 0
    tm = min(TOKEN_TILE, s)
    final_norm = final_g is not None
    in_specs = [
        pl.BlockSpec((None, tm, d), lambda bi, i: (bi, i, 0)),
        pl.BlockSpec((None, N_MOD, d), lambda bi, i: (bi, 0, 0)),
        _resident((1, d), lambda bi, i: (0, 0)),
        _resident((None, d, 2 * d_ff), lambda bi, i: (layer, 0, 0)),
        _resident((None, d_ff, d), lambda bi, i: (layer, 0, 0)),
    ]
    args = [x, mod, g.reshape(1, d), w13, w2]
    if final_norm:
        in_specs.append(_resident((1, d), lambda bi, i: (0, 0)))
        args.append(final_g.reshape(1, d))
    return pl.pallas_call(
        functools.partial(_ffn_kernel, mod_row=mod_row, final_norm=final_norm),
        grid=(b, s // tm),
        in_specs=in_specs,
        out_specs=pl.BlockSpec((None, tm, d), lambda bi, i: (bi, i, 0)),
        out_shape=jax.ShapeDtypeStruct((b, s, d), F32),
        scratch_shapes=[pltpu.VMEM((tm, d), F32)],
        compiler_params=_cparams(("parallel", "parallel")),
        name="ffn_final" if final_norm else "ffn",
    )(*args)


def _inproj_kernel(x_ref, mod_ref, g_ref, w_ref, z_ref, xbc_ref, dt_ref,
                   sq_ref, sk_ref, sv_ref, dq_ref, dk_ref, dv_ref):
    x = x_ref[...]
    h = _norm_modulate(x, g_ref[...], mod_ref[3:4, :], mod_ref[4:5, :]).astype(BF16)

    def proj(off, width):
        return jnp.dot(h, w_ref[:, off:off + width], preferred_element_type=F32)

    qk_scale = HEAD_DIM ** -0.5 * LOG2E
    z_ref[...] = proj(OFF_Z, SSM_D_INNER).astype(BF16)
    xbc_ref[...] = proj(OFF_XBC, SSM_CONV_DIM).astype(BF16)
    dt_ref[...] = proj(OFF_DT, LANES)
    sq_ref[...] = (proj(OFF_SQ, SB_WIDTH) * qk_scale).astype(BF16)
    sk_ref[...] = proj(OFF_SK, SB_WIDTH).astype(BF16)
    sv_ref[...] = proj(OFF_SV, SB_WIDTH).astype(BF16)
    dq_ref[...] = (proj(OFF_DQ, DIFF_WIDTH) * qk_scale).astype(BF16)
    dk_ref[...] = proj(OFF_DK, DIFF_WIDTH).astype(BF16)
    dv_ref[...] = proj(OFF_DV, DIFF_WIDTH).astype(BF16)


def _pack_kernel(w_ref, o_ref):
    dt0 = SSM_D_INNER + SSM_CONV_DIM
    dt1 = dt0 + SSM_HEADS
    w = w_ref[...]
    rows = w.shape[0]
    o_ref[:, :dt0] = w[:, :dt0].astype(BF16)
    o_ref[:, dt0:OFF_DT] = w[:, dt1:].astype(BF16)
    o_ref[:, OFF_DT:] = jnp.concatenate(
        [w[:, dt0:dt1], jnp.zeros((rows, LANES - SSM_HEADS), F32)], axis=1).astype(BF16)


def _pack_w_in(w_in):
    depth, d, n_in = w_in.shape
    rows = 128
    return pl.pallas_call(
        _pack_kernel,
        grid=(depth, d // rows),
        in_specs=[pl.BlockSpec((None, rows, n_in), lambda l, r: (l, r, 0))],
        out_specs=pl.BlockSpec((None, rows, IN_PACKED), lambda l, r: (l, r, 0)),
        out_shape=jax.ShapeDtypeStruct((depth, d, IN_PACKED), BF16),
        compiler_params=_cparams(("arbitrary", "arbitrary")),
        name="pack_w_in",
    )(w_in)


def _inproj(x, mod, g, w, layer):
    b, s, d = x.shape
    tm = min(TOKEN_TILE, s)
    widths = (SSM_D_INNER, SSM_CONV_DIM, LANES) + (SB_WIDTH,) * 3 + (DIFF_WIDTH,) * 3
    dtypes = (BF16, BF16, F32) + (BF16,) * 6
    return pl.pallas_call(
        _inproj_kernel,
        grid=(b, s // tm),
        in_specs=[
            pl.BlockSpec((None, tm, d), lambda bi, i: (bi, i, 0)),
            pl.BlockSpec((None, N_MOD, d), lambda bi, i: (bi, 0, 0)),
            _resident((1, d), lambda bi, i: (0, 0)),
            _resident((None, d, IN_PACKED), lambda bi, i: (layer, 0, 0)),
        ],
        out_specs=[pl.BlockSpec((None, tm, wd), lambda bi, i: (bi, i, 0)) for wd in widths],
        out_shape=[jax.ShapeDtypeStruct((b, s, wd), dt) for wd, dt in zip(widths, dtypes)],
        compiler_params=_cparams(("parallel", "parallel")),
        name="mixer_inproj",
    )(x, mod, g.reshape(1, d), w)


def _ssd_kernel(z_ref, xbc_ref, dtr_ref, cw_ref, cb_ref, dtb_ref, alog_ref, dsk_ref, ng_ref, e_ref,
                y_ref, tail_ref, state_ref):
    rows = range(z_ref.shape[0])
    convs = [_ssd_conv(xbc_ref.at[bb], cw_ref, cb_ref, tail_ref.at[bb], state_ref.at[bb]) for bb in rows]
    for bb in rows:
        _ssd_scan(convs[bb], z_ref.at[bb], dtr_ref.at[bb], dtb_ref, alog_ref, dsk_ref, ng_ref, e_ref,
                  y_ref.at[bb], state_ref.at[bb])


def _ssd_conv(xbc_ref, cw_ref, cb_ref, tail_ref, state_ref):
    L = xbc_ref.shape[0]
    pad = tail_ref.shape[0]
    c = pl.program_id(1)

    @pl.when(c == 0)
    def _():
        tail_ref[...] = jnp.zeros(tail_ref.shape, BF16)
        state_ref[...] = jnp.zeros(state_ref.shape, F32)

    u = xbc_ref[...]
    u_ext = jnp.concatenate([tail_ref[...], u], axis=0)
    tail_ref[...] = u[L - pad:, :]
    trow = lax.broadcasted_iota(jnp.int32, (L, pad + L), 0)
    tcol = lax.broadcasted_iota(jnp.int32, (L, pad + L), 1)
    shifts = jnp.concatenate(
        [jnp.where(tcol == trow + (pad - (SSM_CONV - 1) + k), 1.0, 0.0) for k in range(SSM_CONV)], axis=0)
    shifted = jnp.dot(shifts.astype(BF16), u_ext, preferred_element_type=F32)
    conv = cb_ref[...]
    for k in range(SSM_CONV):
        conv = conv + cw_ref[k:k + 1, :] * shifted[k * L:(k + 1) * L]
    return _silu(conv)


def _ssd_scan(xa, z_ref, dtr_ref, dtb_ref, alog_ref, dsk_ref, ng_ref, e_ref, y_ref, state_ref):
    L = z_ref.shape[0]
    xs = xa[:, :SSM_D_INNER]
    bmat = xa[:, SSM_D_INNER:SSM_D_INNER + SSM_GROUPS * SSM_STATE]
    cmat = xa[:, SSM_D_INNER + SSM_GROUPS * SSM_STATE:]

    dt = _softplus(dtr_ref[...] + dtb_ref[...])
    a = -jnp.exp(alog_ref[...])
    da = dt * a
    row = lax.broadcasted_iota(jnp.int32, (L, L), 0)
    col = lax.broadcasted_iota(jnp.int32, (L, L), 1)
    causal = row >= col
    a_cum = jnp.dot(causal.astype(F32), da, preferred_element_type=F32, precision=HIGHEST)
    a_cum_t = a_cum.T
    exp_a = jnp.exp(a_cum)
    decay_to_end = jnp.exp(a_cum[L - 1:L, :] - a_cum)

    stacked = jnp.concatenate([dt, exp_a, decay_to_end], axis=0)
    hi = stacked.astype(BF16)
    lo = (stacked - hi.astype(F32)).astype(BF16)
    spread = jnp.dot(jnp.concatenate([hi, lo], axis=1), e_ref[...], preferred_element_type=F32)
    dt_e = spread[0:L]
    exp_a_e = spread[L:2 * L]
    dte_e = spread[2 * L:3 * L]

    xd = xs * dt_e
    xd_b = xd.astype(BF16)
    xdd_b = (xd * dte_e).astype(BF16)
    prev_b = state_ref[...].astype(BF16)
    lane = lax.broadcasted_iota(jnp.int32, (L, LANES), 1)
    heads_per_group = SSM_HEADS // SSM_GROUPS
    group_width = heads_per_group * HEAD_DIM

    y_diag_parts = []
    y_off_parts = []
    s_new_parts = []
    for g in range(SSM_GROUPS):
        bg = bmat[:, g * SSM_STATE:(g + 1) * SSM_STATE]
        cg_b = cmat[:, g * SSM_STATE:(g + 1) * SSM_STATE].astype(BF16)
        bg_b = bg.astype(BF16)
        bg_t_b = bg.T.astype(BF16)
        gs = slice(g * group_width, (g + 1) * group_width)
        cb = lax.dot_general(cg_b, bg_b, (((1,), (1,)), ((), ())), preferred_element_type=F32)
        y_off_parts.append(jnp.dot(cg_b, prev_b[:, gs], preferred_element_type=F32))
        s_new_parts.append(jnp.dot(bg_t_b, xdd_b[:, gs], preferred_element_type=F32))
        for hp in range(heads_per_group // 2):
            ms = []
            for h in (g * heads_per_group + 2 * hp, g * heads_per_group + 2 * hp + 1):
                seg = a_cum[:, h:h + 1] - a_cum_t[h:h + 1, :]
                decay = jnp.exp(jnp.where(causal, seg, -jnp.inf))
                ms.append((cb * decay).astype(BF16))
            pair = (g * heads_per_group) // 2 + hp
            yp = jnp.dot(jnp.concatenate(ms, axis=0), xd_b[:, pair * LANES:(pair + 1) * LANES],
                         preferred_element_type=F32)
            y_diag_parts.append(jnp.where(lane < HEAD_DIM, yp[:L], yp[L:]))
    y_diag = jnp.concatenate(y_diag_parts, axis=1)
    y_off = jnp.concatenate(y_off_parts, axis=1)
    s_new = jnp.concatenate(s_new_parts, axis=1)

    y = y_diag + y_off * exp_a_e + dsk_ref[...] * xs
    state_ref[...] = state_ref[...] * exp_a_e[L - 1:L, :] + s_new

    y = y * _silu(z_ref[...].astype(F32))
    outs = []
    for g in range(SSM_GROUPS):
        yg = y[:, g * group_width:(g + 1) * group_width]
        ms = jnp.mean(yg * yg, axis=-1, keepdims=True)
        outs.append(yg * lax.rsqrt(ms + EPS))
    y_ref[...] = (jnp.concatenate(outs, axis=1) * ng_ref[...]).astype(BF16)


def _head_spread_matrix():
    e = np.zeros((2 * LANES, SSM_D_INNER), np.float32)
    for h in range(SSM_HEADS):
        e[h, h * HEAD_DIM:(h + 1) * HEAD_DIM] = 1.0
        e[LANES + h, h * HEAD_DIM:(h + 1) * HEAD_DIM] = 1.0
    return jnp.asarray(e, BF16)


def _pad_lanes(v):
    return jnp.zeros((1, LANES), F32).at[0, :v.shape[0]].set(v)


def _ssd(z, xbc, dt_raw, conv_w, conv_b, dt_bias, a_log, d_skip, norm_g):
    b, s, _ = z.shape
    L = SSM_CHUNK
    nb = SSD_STREAMS if b % SSD_STREAMS == 0 else 1
    row = lambda width: _resident((1, width), lambda bi, c: (0, 0))
    return pl.pallas_call(
        _ssd_kernel,
        grid=(b // nb, s // L),
        in_specs=[
            pl.BlockSpec((nb, L, SSM_D_INNER), lambda bi, c: (bi, c, 0)),
            pl.BlockSpec((nb, L, SSM_CONV_DIM), lambda bi, c: (bi, c, 0)),
            pl.BlockSpec((nb, L, LANES), lambda bi, c: (bi, c, 0)),
            _resident((SSM_CONV, SSM_CONV_DIM), lambda bi, c: (0, 0)),
            row(SSM_CONV_DIM), row(LANES), row(LANES), row(SSM_D_INNER), row(SSM_D_INNER),
            _resident((2 * LANES, SSM_D_INNER), lambda bi, c: (0, 0)),
        ],
        out_specs=pl.BlockSpec((nb, L, SSM_D_INNER), lambda bi, c: (bi, c, 0)),
        out_shape=jax.ShapeDtypeStruct((b, s, SSM_D_INNER), BF16),
        scratch_shapes=[pltpu.VMEM((nb, BF16_SUBLANES, SSM_CONV_DIM), BF16),
                        pltpu.VMEM((nb, SSM_STATE, SSM_D_INNER), F32)],
        compiler_params=_cparams(("parallel", "arbitrary")),
        name="ssd_mixer",
    )(z, xbc, dt_raw, conv_w.T, conv_b.reshape(1, -1), _pad_lanes(dt_bias), _pad_lanes(a_log),
      jnp.repeat(d_skip, HEAD_DIM).reshape(1, -1), norm_g.reshape(1, -1), _head_spread_matrix())


def _split_lane_halves(x2):
    xf = x2.astype(F32)
    lane = lax.broadcasted_iota(jnp.int32, xf.shape, 1)
    lo = jnp.where(lane < HEAD_DIM, xf, 0.0)
    hi = jnp.where(lane >= HEAD_DIM, xf, 0.0)
    return jnp.concatenate([lo, hi], axis=0).astype(BF16)


def _scores_t(kb, qs):
    return lax.dot_general(kb, qs, (((1,), (1,)), ((), ())), preferred_element_type=F32)


def _pv_t(vb, wt):
    return lax.dot_general(vb, wt, (((0,), (0,)), ((), ())), preferred_element_type=F32)


def _sb_kernel(q_ref, k_ref, v_ref, o_ref, z_ref, cs_ref, tot_ref, acc_ref, carry_ref):
    T = q_ref.shape[0]
    n_streams = q_ref.shape[1] // LANES
    i = pl.program_id(2)
    qs = [_split_lane_halves(q_ref[:, p * LANES:(p + 1) * LANES]) for p in range(n_streams)]

    key = lax.broadcasted_iota(jnp.int32, (T, 2 * T), 0)
    qry = lax.broadcasted_iota(jnp.int32, (T, 2 * T), 1)
    strict = key < jnp.where(qry >= T, qry - T, qry)
    ur = lax.broadcasted_iota(jnp.int32, (T, T), 0)
    uc = lax.broadcasted_iota(jnp.int32, (T, T), 1)
    neg_suffix = jnp.where(uc >= ur, -1.0, 0.0).astype(BF16)
    streams = range(n_streams)
    sign_bit = jnp.uint32(0x80000000)

    def scores(j_lo, nb):
        rows = pl.ds(pl.multiple_of(j_lo * T, T), nb * T)
        return [_scores_t(k_ref[rows, p * LANES:(p + 1) * LANES], qs[p]) for p in streams]

    def softplus2(z, diagonal):
        neg_abs = lax.bitcast_convert_type(lax.bitcast_convert_type(z, jnp.uint32) | sign_bit, F32)
        sp = jnp.maximum(z, 0.0) + jnp.log(1.0 + jnp.exp2(neg_abs)) * LOG2E
        if diagonal:
            sp = jnp.where(strict, sp, 0.0)
        return sp.astype(BF16)

    def suffix_sums(sp, nb):
        parts = [jnp.dot(neg_suffix, sp[u * T:(u + 1) * T], preferred_element_type=F32) for u in range(nb)]
        rows = [None] * nb
        total = parts[nb - 1][0:1, :]
        for u in reversed(range(nb - 1)):
            rows[u] = total
            total = total + parts[u][0:1, :]
        rows[nb - 1] = total
        return parts, (rows[0] if nb == 1 else jnp.concatenate(rows, axis=0))

    def weights(z, parts, after_rows, carry, diagonal):
        nb = len(parts)
        ws = []
        for u in range(nb):
            later = carry if u == nb - 1 else carry + after_rows[u:u + 1, :]
            w = jnp.exp2(z[u * T:(u + 1) * T] + (parts[u] + later))
            if diagonal:
                w = jnp.where(strict, w, 0.0)
            ws.append(w.astype(BF16))
        return ws[0] if nb == 1 else jnp.concatenate(ws, axis=0)

    def accumulate(j_lo, nb, ws, tots):
        rows = pl.ds(pl.multiple_of(j_lo * T, T), nb * T)
        for p in streams:
            acc_ref[p] += _pv_t(v_ref[rows, p * LANES:(p + 1) * LANES], ws[p])
            carry_ref[p] += tots[p]

    def direct_step(j_lo, nb, diagonal=False):
        zs = scores(j_lo, nb)
        cts = [suffix_sums(softplus2(z, diagonal), nb) for z in zs]
        ws = [weights(zs[p], cts[p][0], cts[p][1], carry_ref[p], diagonal) for p in streams]
        accumulate(j_lo, nb, ws, [ct[1][nb - 1:nb, :] for ct in cts])

    def finish_produce(slot, zs, with_diagonal=False):
        for p in streams:
            z_ref[slot, p] = zs[p]
        for p in streams:
            if with_diagonal:
                sp = jnp.concatenate([softplus2(z_ref[slot, p, 0:T, :], False),
                                      softplus2(z_ref[slot, p, T:2 * T, :], True)], axis=0)
            else:
                sp = softplus2(z_ref[slot, p], False)
            parts, rows = suffix_sums(sp, 2)
            if with_diagonal:
                parts[1] = jnp.where(strict, parts[1], -jnp.inf)
            cs_ref[slot, p] = jnp.concatenate(parts, axis=0)
            tot_ref[slot, p] = rows

    for p in streams:
        acc_ref[p] = jnp.zeros((LANES, 2 * T), F32)
        carry_ref[p] = jnp.zeros((1, 2 * T), F32)
    n_double = (i + 1) // 2
    pair_lo = lambda t: i - 1 - 2 * t

    def consume(t, slot, produce_next):
        ws = [weights(z_ref[slot, p], [cs_ref[slot, p, 0:T, :], cs_ref[slot, p, T:2 * T, :]],
                      tot_ref[slot, p], carry_ref[p], False) for p in streams]
        tots = [tot_ref[slot, p, 1:2, :] for p in streams]
        if produce_next:
            finish_produce(1 - slot, scores(pair_lo(t + 1), 2))
        accumulate(pair_lo(t), 2, ws, tots)

    @pl.when(n_double > 0)
    def _():
        finish_produce(0, scores(pair_lo(0), 2), with_diagonal=True)

    n_twice = jnp.maximum(n_double - 1, 0) // 2

    def body(u, _):
        consume(2 * u, 0, True)
        consume(2 * u + 1, 1, True)
        return 0

    lax.fori_loop(0, n_twice, body, 0)
    left = n_double - 2 * n_twice

    @pl.when(left == 2)
    def _():
        consume(2 * n_twice, 0, True)
        consume(2 * n_twice + 1, 1, False)

    @pl.when(left == 1)
    def _():
        consume(2 * n_twice, 0, False)

    @pl.when(jnp.logical_and(i + 1 - 2 * n_double > 0, i > 0))
    def _():
        direct_step(0, 1)

    @pl.when(i == 0)
    def _():
        direct_step(0, 1, diagonal=True)

    for p in range(n_streams):
        acc = acc_ref[p]
        pair_t = jnp.concatenate([acc[:HEAD_DIM, :T], acc[HEAD_DIM:, T:]], axis=0)
        o_ref[:, p * LANES:(p + 1) * LANES] = pair_t.T.astype(BF16)


def _stick_breaking(q, k, v):
    b, s, width = q.shape
    T = min(ATT_BLOCK, s)
    gw = SB_STREAMS * LANES
    return pl.pallas_call(
        _sb_kernel,
        grid=(b, width // gw, s // T),
        in_specs=[
            pl.BlockSpec((None, T, gw), lambda bi, g, i: (bi, i, g)),
            pl.BlockSpec((None, s, gw), lambda bi, g, i: (bi, 0, g)),
            pl.BlockSpec((None, s, gw), lambda bi, g, i: (bi, 0, g)),
        ],
        out_specs=pl.BlockSpec((None, T, gw), lambda bi, g, i: (bi, i, g)),
        out_shape=jax.ShapeDtypeStruct((b, s, width), BF16),
        scratch_shapes=[
            pltpu.VMEM((2, SB_STREAMS, 2 * T, 2 * T), F32),
            pltpu.VMEM((2, SB_STREAMS, 2 * T, 2 * T), F32),
            pltpu.VMEM((2, SB_STREAMS, 2, 2 * T), F32),
            pltpu.VMEM((SB_STREAMS, LANES, 2 * T), F32),
            pltpu.VMEM((SB_STREAMS, 1, 2 * T), F32),
        ],
        compiler_params=_cparams(("parallel", "parallel", "arbitrary")),
        name="stick_breaking_attention",
    )(q, k, v)


def _bucket_starts():
    max_exact = N_REL_BUCKETS // 2
    dist = np.arange(0, 4 * REL_MAX_DIST, dtype=np.float64)
    ratio = np.log(np.maximum(dist, max_exact) / max_exact) / math.log(REL_MAX_DIST / max_exact)
    scaled = ratio * (N_REL_BUCKETS - max_exact)
    frac = np.abs(scaled - np.round(scaled))
    interior = (dist > max_exact) & (dist < REL_MAX_DIST)
    assert frac[interior].min() > 1e-3, "a bucket edge sits on an integer distance"
    large = np.minimum(max_exact + np.floor(scaled + 1e-9).astype(np.int64), N_REL_BUCKETS - 1)
    bucket = np.where(dist < max_exact, dist.astype(np.int64), large)
    assert np.all(np.diff(bucket) >= 0) and bucket[-1] == N_REL_BUCKETS - 1
    return [int(np.argmax(bucket >= bkt)) for bkt in range(N_REL_BUCKETS)]


def _bias_table_kernel(rel_ref, o_ref, *, starts):
    h = pl.program_id(0)
    T = o_ref.shape[-1]
    key = lax.broadcasted_iota(jnp.int32, (T, T), 0)
    qry = lax.broadcasted_iota(jnp.int32, (T, T), 1)
    far = rel_ref[N_REL_BUCKETS - 1, h]
    for o_blk in range(o_ref.shape[0]):
        dist = qry - key + o_blk * T
        bias = jnp.full((T, T), rel_ref[0, h], F32)
        for bkt in range(1, N_REL_BUCKETS):
            bias = jnp.where(dist >= starts[bkt], rel_ref[bkt, h], bias)
        o_ref[o_blk] = jnp.where(dist >= 0, (bias - far) * LOG2E, -jnp.inf)


def _bias_tables(rel_bias, T):
    starts = _bucket_starts()
    assert T + 1 >= starts[-1]
    return pl.pallas_call(
        functools.partial(_bias_table_kernel, starts=starts),
        grid=(DIFF_HEADS,),
        in_specs=[pl.BlockSpec(memory_space=pltpu.SMEM)],
        out_specs=pl.BlockSpec((None, 2, T, T), lambda h: (h, 0, 0, 0)),
        out_shape=jax.ShapeDtypeStruct((DIFF_HEADS, 2, T, T), F32),
        compiler_params=_cparams(("arbitrary",)),
        name="t5_bias_tables",
    )(rel_bias)


def _diff_kernel(q_ref, k_ref, v_ref, tb_ref, lq1_ref, lk1_ref, lq2_ref, lk2_ref, sg_ref, o_ref,
                 s_ref, m_ref, l_ref, acc_ref, *, lambda_init):
    T = q_ref.shape[0]
    n_streams = q_ref.shape[1] // LANES
    i = pl.program_id(2)
    qs = [_split_lane_halves(q_ref[:, h * LANES:(h + 1) * LANES]) for h in range(n_streams)]

    heads = range(n_streams)

    def scores(j_lo, nb):
        rows = pl.ds(pl.multiple_of(j_lo * T, T), nb * T)
        return [_scores_t(k_ref[rows, h * LANES:(h + 1) * LANES], qs[h]) for h in heads]

    def softmax_part(ss, table_rows):
        out = []
        for h in heads:
            read = ss[h] if callable(ss[h]) else (lambda v=ss[h]: v)
            if table_rows is not None:
                biases = [tb_ref[h, o] for o in table_rows]
                bias = biases[0] if len(biases) == 1 else jnp.concatenate(biases, axis=0)
                s = read() + jnp.concatenate([bias, bias], axis=1)
                read = lambda v=s: v
            m_old = m_ref[h]
            m_new = jnp.maximum(m_old, jnp.max(read(), axis=0, keepdims=True))
            alpha = jnp.exp2(m_old - m_new)
            p = jnp.exp2(read() - m_new)
            m_ref[h] = m_new
            l_ref[h] = alpha * l_ref[h] + jnp.sum(p, axis=0, keepdims=True)
            out.append((alpha, p.astype(BF16)))
        return out

    def accumulate(j_lo, nb, aps):
        rows = pl.ds(pl.multiple_of(j_lo * T, T), nb * T)
        for h in heads:
            alpha, p = aps[h]
            acc_ref[h] = alpha * acc_ref[h] + _pv_t(v_ref[rows, h * LANES:(h + 1) * LANES], p)

    for h in heads:
        m_ref[h] = jnp.full((1, 2 * T), -jnp.inf, F32)
        l_ref[h] = jnp.zeros((1, 2 * T), F32)
        acc_ref[h] = jnp.zeros((DIFF_V_DIM, 2 * T), F32)
    n_far = jnp.maximum(i - 1, 0)
    n_double = n_far // 2
    first = n_far - 2 * n_double
    pair_lo = lambda t: first + 2 * t

    def produce(t, slot):
        for h, s in enumerate(scores(pair_lo(t), 2)):
            s_ref[slot, h] = s

    def consume(t, slot, produce_next, table_rows=None):
        aps = softmax_part([lambda h=h: s_ref[slot, h] for h in heads], table_rows)
        if produce_next:
            produce(t + 1, 1 - slot)
        accumulate(pair_lo(t), 2, aps)

    @pl.when(first > 0)
    def _():
        accumulate(0, 1, softmax_part(scores(0, 1), None))

    @pl.when(i > 0)
    def _():
        produce(0, 0)

    n_twice = n_double // 2

    def body(u, _):
        consume(2 * u, 0, True)
        consume(2 * u + 1, 1, True)
        return 0

    lax.fori_loop(0, n_twice, body, 0)
    left = n_double - 2 * n_twice

    @pl.when(jnp.logical_and(i > 0, left == 1))
    def _():
        consume(2 * n_twice, 0, True)
        consume(2 * n_twice + 1, 1, False, [1, 0])

    @pl.when(jnp.logical_and(i > 0, left == 0))
    def _():
        consume(2 * n_twice, 0, False, [1, 0])

    @pl.when(i == 0)
    def _():
        accumulate(0, 1, softmax_part(scores(0, 1), [0]))

    lam = (jnp.exp(jnp.sum(lq1_ref[...] * lk1_ref[...], axis=-1, keepdims=True))
           - jnp.exp(jnp.sum(lq2_ref[...] * lk2_ref[...], axis=-1, keepdims=True)) + lambda_init)
    for h in range(n_streams):
        l, acc = l_ref[h], acc_ref[h]
        o_t = acc[:, :T] / l[:, :T] - lam * (acc[:, T:] / l[:, T:])
        o = o_t.T
        ms = jnp.mean(o * o, axis=-1, keepdims=True)
        o = o * lax.rsqrt(ms + EPS) * sg_ref[...] * (1.0 - lambda_init)
        o_ref[:, h * LANES:(h + 1) * LANES] = o.astype(BF16)


def _diff_attention(q, k, v, tables, lq1, lk1, lq2, lk2, subln_g, lambda_init):
    b, s, width = q.shape
    T = tables.shape[-1]
    vec = lambda n: _resident((1, n), lambda bi, g, i: (0, 0))
    gw = DIFF_STREAMS * LANES
    return pl.pallas_call(
        functools.partial(_diff_kernel, lambda_init=lambda_init),
        grid=(b, width // gw, s // T),
        in_specs=[
            pl.BlockSpec((None, T, gw), lambda bi, g, i: (bi, i, g)),
            pl.BlockSpec((None, s, gw), lambda bi, g, i: (bi, 0, g)),
            pl.BlockSpec((None, s, gw), lambda bi, g, i: (bi, 0, g)),
            pl.BlockSpec((DIFF_STREAMS,) + tables.shape[1:], lambda bi, g, i: (g, 0, 0, 0)),
            vec(HEAD_DIM), vec(HEAD_DIM), vec(HEAD_DIM), vec(HEAD_DIM), vec(DIFF_V_DIM),
        ],
        out_specs=pl.BlockSpec((None, T, gw), lambda bi, g, i: (bi, i, g)),
        out_shape=jax.ShapeDtypeStruct((b, s, width), BF16),
        scratch_shapes=[
            pltpu.VMEM((2, DIFF_STREAMS, 2 * T, 2 * T), F32),
            pltpu.VMEM((DIFF_STREAMS, 1, 2 * T), F32),
            pltpu.VMEM((DIFF_STREAMS, 1, 2 * T), F32),
            pltpu.VMEM((DIFF_STREAMS, DIFF_V_DIM, 2 * T), F32),
        ],
        compiler_params=_cparams(("parallel", "parallel", "arbitrary")),
        name="differential_attention",
    )(q, k, v, tables, lq1.reshape(1, -1), lk1.reshape(1, -1), lq2.reshape(1, -1), lk2.reshape(1, -1),
      subln_g.reshape(1, -1))


def _outproj_kernel(x_ref, mod_ref, ys_ref, yb_ref, yd_ref, w_ref, o_ref):
    o1 = SSM_D_INNER
    o2 = o1 + SB_WIDTH
    y = jnp.dot(ys_ref[...], w_ref[0:o1, :].astype(BF16), preferred_element_type=F32)
    y = y + jnp.dot(yb_ref[...], w_ref[o1:o2, :].astype(BF16), preferred_element_type=F32)
    y = y + jnp.dot(yd_ref[...], w_ref[o2:, :].astype(BF16), preferred_element_type=F32)
    o_ref[...] = x_ref[...] + mod_ref[5:6, :] * y


def _outproj(x, mod, y_ssm, y_sb, y_diff, w_out, layer):
    b, s, d = x.shape
    tm = min(TOKEN_TILE, s)
    tok = lambda width: pl.BlockSpec((None, tm, width), lambda bi, i: (bi, i, 0))
    return pl.pallas_call(
        _outproj_kernel,
        grid=(b, s // tm),
        in_specs=[
            tok(d),
            pl.BlockSpec((None, N_MOD, d), lambda bi, i: (bi, 0, 0)),
            tok(SSM_D_INNER), tok(SB_WIDTH), tok(DIFF_WIDTH),
            _resident((None,) + w_out.shape[1:], lambda bi, i: (layer, 0, 0)),
        ],
        out_specs=tok(d),
        out_shape=jax.ShapeDtypeStruct((b, s, d), F32),
        compiler_params=_cparams(("parallel", "parallel")),
        name="mixer_outproj",
    )(x, mod, y_ssm, y_sb, y_diff, w_out)


def kernel(x, c, ada_w, ada_b, ffn1_norm, ffn1_w13, ffn1_w2, mix_norm, w_in, ssm_conv_w, ssm_conv_b, ssm_dt_bias, ssm_a_log, ssm_d, ssm_norm, diff_lambda_q1, diff_lambda_k1, diff_lambda_q2, diff_lambda_k2, diff_subln, rel_bias, w_out, ffn2_norm, ffn2_w13, ffn2_w2, final_norm):
    depth = ada_w.shape[0]
    s = x.shape[1]
    mods = _ada_modulation(c, ada_w, ada_b)
    tables = _bias_tables(rel_bias, min(ATT_BLOCK, s))
    w_in_packed = _pack_w_in(w_in)
    for l in range(depth):
        mod = mods[l]
        x = _ffn(x, mod, ffn1_norm[l], ffn1_w13, ffn1_w2, l, mod_row=0)
        lambda_init = 0.8 - 0.6 * math.exp(-0.3 * l)
        z, xbc, dt_raw, sq, sk, sv, dq, dk, dv = _inproj(x, mod, mix_norm[l], w_in_packed, l)
        y_ssm = _ssd(z, xbc, dt_raw, ssm_conv_w[l], ssm_conv_b[l], ssm_dt_bias[l], ssm_a_log[l], ssm_d[l],
                     ssm_norm[l])
        y_sb = _stick_breaking(sq, sk, sv)
        y_diff = _diff_attention(dq, dk, dv, tables, diff_lambda_q1[l], diff_lambda_k1[l], diff_lambda_q2[l],
                                 diff_lambda_k2[l], diff_subln[l], lambda_init)
        x = _outproj(x, mod, y_ssm, y_sb, y_diff, w_out, l)
        x = _ffn(x, mod, ffn2_norm[l], ffn2_w13, ffn2_w2, l, mod_row=6,
                 final_g=final_norm if l == depth - 1 else None)
    return x
```

```python
import functools
import math

import numpy as np
import jax
import jax.numpy as jnp
from jax import lax
from jax.experimental import pallas as pl
from jax.experimental.pallas import tpu as pltpu

F32 = jnp.float32
BF16 = jnp.bfloat16
HIGHEST = lax.Precision.HIGHEST
LOG2E = math.log2(math.e)

HEAD_DIM = 64
SSM_HEADS = 16
SSM_GROUPS = 2
SSM_STATE = 128
SSM_CONV = 4
SSM_CHUNK = 128
SSM_D_INNER = SSM_HEADS * HEAD_DIM
SSM_CONV_DIM = SSM_D_INNER + 2 * SSM_GROUPS * SSM_STATE
SB_HEADS = 8
SB_WIDTH = SB_HEADS * HEAD_DIM
DIFF_HEADS = 4
DIFF_V_DIM = 2 * HEAD_DIM
DIFF_WIDTH = DIFF_HEADS * DIFF_V_DIM
N_MOD = 9
N_REL_BUCKETS = 32
REL_MAX_DIST = 128
EPS = 1e-6

LANES = 128
BF16_SUBLANES = 16
VMEM_LIMIT = 56 * 1024 * 1024

TOKEN_TILE = 512
FF_CHUNK = 256
ATT_BLOCK = 256

OFF_Z = 0
OFF_XBC = OFF_Z + SSM_D_INNER
OFF_SQ = OFF_XBC + SSM_CONV_DIM
OFF_SK = OFF_SQ + SB_WIDTH
OFF_SV = OFF_SK + SB_WIDTH
OFF_DQ = OFF_SV + SB_WIDTH
OFF_DK = OFF_DQ + DIFF_WIDTH
OFF_DV = OFF_DK + DIFF_WIDTH
OFF_DT = OFF_DV + DIFF_WIDTH
IN_PACKED = OFF_DT + LANES


def _cparams(semantics):
    return pltpu.CompilerParams(dimension_semantics=semantics, vmem_limit_bytes=VMEM_LIMIT)


SSD_STREAMS = 2
SB_STREAMS = 2
DIFF_STREAMS = 2


def _resident(block_shape, index_map):
    return pl.BlockSpec(block_shape, index_map, pipeline_mode=pl.Buffered(1))


def _silu(x):
    hx = 0.5 * x
    return hx + hx * jnp.tanh(hx)


def _softplus(x):
    return jnp.maximum(x, 0.0) + jnp.log1p(jnp.exp(-jnp.abs(x)))


def _norm_modulate(x, g, shift, scale):
    ms = jnp.mean(x * x, axis=-1, keepdims=True)
    y = x * lax.rsqrt(ms + EPS) * g
    return y * (1.0 + scale) + shift


def _ada_kernel(c_ref, w_ref, b_ref, o_ref):
    cond = _silu(c_ref[...])
    o_ref[...] = jnp.dot(cond, w_ref[...], preferred_element_type=F32, precision=HIGHEST) + b_ref[...]


def _ada_modulation(c, ada_w, ada_b):
    depth, d, nmod = ada_w.shape
    b = c.shape[0]
    rows = 8 * pl.cdiv(b, 8)
    c_pad = jnp.zeros((rows, d), F32).at[:b].set(c)
    out = pl.pallas_call(
        _ada_kernel,
        grid=(depth, nmod // d),
        in_specs=[
            pl.BlockSpec((rows, d), lambda l, j: (0, 0)),
            pl.BlockSpec((None, d, d), lambda l, j: (l, 0, j)),
            pl.BlockSpec((None, 1, d), lambda l, j: (l, 0, j)),
        ],
        out_specs=pl.BlockSpec((None, rows, d), lambda l, j: (l, 0, j)),
        out_shape=jax.ShapeDtypeStruct((depth, rows, nmod), F32),
        compiler_params=_cparams(("arbitrary", "arbitrary")),
        name="ada_modulation",
    )(c_pad, ada_w, ada_b.reshape(depth, 1, nmod))
    return out[:, :b].reshape(depth, b, N_MOD, d)


def _ffn_kernel(x_ref, mod_ref, g_ref, w13_ref, w2_ref, *rest, mod_row, final_norm):
    if final_norm:
        fg_ref, o_ref, acc_ref = rest
    else:
        o_ref, acc_ref = rest
    x = x_ref[...]
    shift = mod_ref[mod_row:mod_row + 1, :]
    scale = mod_ref[mod_row + 1:mod_row + 2, :]
    gate = mod_ref[mod_row + 2:mod_row + 3, :]
    h = _norm_modulate(x, g_ref[...], shift, scale).astype(BF16)
    d_ff = w2_ref.shape[0]
    for j in range(d_ff // FF_CHUNK):
        cols = slice(j * FF_CHUNK, (j + 1) * FF_CHUNK)
        w1 = w13_ref[:, cols].astype(BF16)
        w3 = w13_ref[:, d_ff + j * FF_CHUNK:d_ff + (j + 1) * FF_CHUNK].astype(BF16)
        a = jnp.dot(h, w1, preferred_element_type=F32)
        u = jnp.dot(h, w3, preferred_element_type=F32)
        act = (_silu(a) * u).astype(BF16)
        part = jnp.dot(act, w2_ref[cols, :].astype(BF16), preferred_element_type=F32)
        if j == 0:
            acc_ref[...] = part
        else:
            acc_ref[...] += part
    y = x + (0.5 * gate) * acc_ref[...]
    if final_norm:
        ms = jnp.mean(y * y, axis=-1, keepdims=True)
        y = y * lax.rsqrt(ms + EPS) * fg_ref[...]
    o_ref[...] = y


def _ffn(x, mod, g, w13, w2, layer, mod_row, final_g=None):
    b, s, d = x.shape
    d_ff = w2.shape[1]
    assert d_ff % FF_CHUNK == 0 and d_ff % LANES == 0
    tm = min(TOKEN_TILE, s)
    final_norm = final_g is not None
    in_specs = [
        pl.BlockSpec((None, tm, d), lambda bi, i: (bi, i, 0)),
        pl.BlockSpec((None, N_MOD, d), lambda bi, i: (bi, 0, 0)),
        _resident((1, d), lambda bi, i: (0, 0)),
        _resident((None, d, 2 * d_ff), lambda bi, i: (layer, 0, 0)),
        _resident((None, d_ff, d), lambda bi, i: (layer, 0, 0)),
    ]
    args = [x, mod, g.reshape(1, d), w13, w2]
    if final_norm:
        in_specs.append(_resident((1, d), lambda bi, i: (0, 0)))
        args.append(final_g.reshape(1, d))
    return pl.pallas_call(
        functools.partial(_ffn_kernel, mod_row=mod_row, final_norm=final_norm),
        grid=(b, s // tm),
        in_specs=in_specs,
        out_specs=pl.BlockSpec((None, tm, d), lambda bi, i: (bi, i, 0)),
        out_shape=jax.ShapeDtypeStruct((b, s, d), F32),
        scratch_shapes=[pltpu.VMEM((tm, d), F32)],
        compiler_params=_cparams(("parallel", "parallel")),
        name="ffn_final" if final_norm else "ffn",
    )(*args)


def _inproj_kernel(x_ref, mod_ref, g_ref, w_ref, z_ref, xbc_ref, dt_ref,
                   sq_ref, sk_ref, sv_ref, dq_ref, dk_ref, dv_ref):
    x = x_ref[...]
    h = _norm_modulate(x, g_ref[...], mod_ref[3:4, :], mod_ref[4:5, :]).astype(BF16)

    def proj(off, width):
        return jnp.dot(h, w_ref[:, off:off + width], preferred_element_type=F32)

    qk_scale = HEAD_DIM ** -0.5 * LOG2E
    z_ref[...] = proj(OFF_Z, SSM_D_INNER).astype(BF16)
    xbc_ref[...] = proj(OFF_XBC, SSM_CONV_DIM).astype(BF16)
    dt_ref[...] = proj(OFF_DT, LANES)
    sq_ref[...] = (proj(OFF_SQ, SB_WIDTH) * qk_scale).astype(BF16)
    sk_ref[...] = proj(OFF_SK, SB_WIDTH).astype(BF16)
    sv_ref[...] = proj(OFF_SV, SB_WIDTH).astype(BF16)
    dq_ref[...] = (proj(OFF_DQ, DIFF_WIDTH) * qk_scale).astype(BF16)
    dk_ref[...] = proj(OFF_DK, DIFF_WIDTH).astype(BF16)
    dv_ref[...] = proj(OFF_DV, DIFF_WIDTH).astype(BF16)


def _pack_kernel(w_ref, o_ref):
    dt0 = SSM_D_INNER + SSM_CONV_DIM
    dt1 = dt0 + SSM_HEADS
    w = w_ref[...]
    rows = w.shape[0]
    o_ref[:, :dt0] = w[:, :dt0].astype(BF16)
    o_ref[:, dt0:OFF_DT] = w[:, dt1:].astype(BF16)
    o_ref[:, OFF_DT:] = jnp.concatenate(
        [w[:, dt0:dt1], jnp.zeros((rows, LANES - SSM_HEADS), F32)], axis=1).astype(BF16)


def _pack_w_in(w_in):
    depth, d, n_in = w_in.shape
    rows = 128
    return pl.pallas_call(
        _pack_kernel,
        grid=(depth, d // rows),
        in_specs=[pl.BlockSpec((None, rows, n_in), lambda l, r: (l, r, 0))],
        out_specs=pl.BlockSpec((None, rows, IN_PACKED), lambda l, r: (l, r, 0)),
        out_shape=jax.ShapeDtypeStruct((depth, d, IN_PACKED), BF16),
        compiler_params=_cparams(("arbitrary", "arbitrary")),
        name="pack_w_in",
    )(w_in)


def _inproj(x, mod, g, w, layer):
    b, s, d = x.shape
    tm = min(TOKEN_TILE, s)
    widths = (SSM_D_INNER, SSM_CONV_DIM, LANES) + (SB_WIDTH,) * 3 + (DIFF_WIDTH,) * 3
    dtypes = (BF16, BF16, F32) + (BF16,) * 6
    return pl.pallas_call(
        _inproj_kernel,
        grid=(b, s // tm),
        in_specs=[
            pl.BlockSpec((None, tm, d), lambda bi, i: (bi, i, 0)),
            pl.BlockSpec((None, N_MOD, d), lambda bi, i: (bi, 0, 0)),
            _resident((1, d), lambda bi, i: (0, 0)),
            _resident((None, d, IN_PACKED), lambda bi, i: (layer, 0, 0)),
        ],
        out_specs=[pl.BlockSpec((None, tm, wd), lambda bi, i: (bi, i, 0)) for wd in widths],
        out_shape=[jax.ShapeDtypeStruct((b, s, wd), dt) for wd, dt in zip(widths, dtypes)],
        compiler_params=_cparams(("parallel", "parallel")),
        name="mixer_inproj",
    )(x, mod, g.reshape(1, d), w)


def _ssd_kernel(z_ref, xbc_ref, dtr_ref, cw_ref, cb_ref, dtb_ref, alog_ref, dsk_ref, ng_ref, e_ref,
                y_ref, tail_ref, state_ref):
    rows = range(z_ref.shape[0])
    convs = [_ssd_conv(xbc_ref.at[bb], cw_ref, cb_ref, tail_ref.at[bb], state_ref.at[bb]) for bb in rows]
    for bb in rows:
        _ssd_scan(convs[bb], z_ref.at[bb], dtr_ref.at[bb], dtb_ref, alog_ref, dsk_ref, ng_ref, e_ref,
                  y_ref.at[bb], state_ref.at[bb])


def _ssd_conv(xbc_ref, cw_ref, cb_ref, tail_ref, state_ref):
    L = xbc_ref.shape[0]
    pad = tail_ref.shape[0]
    c = pl.program_id(1)

    @pl.when(c == 0)
    def _():
        tail_ref[...] = jnp.zeros(tail_ref.shape, BF16)
        state_ref[...] = jnp.zeros(state_ref.shape, F32)

    u = xbc_ref[...]
    u_ext = jnp.concatenate([tail_ref[...], u], axis=0)
    tail_ref[...] = u[L - pad:, :]
    trow = lax.broadcasted_iota(jnp.int32, (L, pad + L), 0)
    tcol = lax.broadcasted_iota(jnp.int32, (L, pad + L), 1)
    shifts = jnp.concatenate(
        [jnp.where(tcol == trow + (pad - (SSM_CONV - 1) + k), 1.0, 0.0) for k in range(SSM_CONV)], axis=0)
    shifted = jnp.dot(shifts.astype(BF16), u_ext, preferred_element_type=F32)
    conv = cb_ref[...]
    for k in range(SSM_CONV):
        conv = conv + cw_ref[k:k + 1, :] * shifted[k * L:(k + 1) * L]
    return _silu(conv)


def _ssd_scan(xa, z_ref, dtr_ref, dtb_ref, alog_ref, dsk_ref, ng_ref, e_ref, y_ref, state_ref):
    L = z_ref.shape[0]
    xs = xa[:, :SSM_D_INNER]
    bmat = xa[:, SSM_D_INNER:SSM_D_INNER + SSM_GROUPS * SSM_STATE]
    cmat = xa[:, SSM_D_INNER + SSM_GROUPS * SSM_STATE:]

    dt = _softplus(dtr_ref[...] + dtb_ref[...])
    a = -jnp.exp(alog_ref[...])
    da = dt * a
    row = lax.broadcasted_iota(jnp.int32, (L, L), 0)
    col = lax.broadcasted_iota(jnp.int32, (L, L), 1)
    causal = row >= col
    a_cum = jnp.dot(causal.astype(F32), da, preferred_element_type=F32, precision=HIGHEST)
    a_cum_t = a_cum.T
    exp_a = jnp.exp(a_cum)
    decay_to_end = jnp.exp(a_cum[L - 1:L, :] - a_cum)

    stacked = jnp.concatenate([dt, exp_a, decay_to_end], axis=0)
    hi = stacked.astype(BF16)
    lo = (stacked - hi.astype(F32)).astype(BF16)
    spread = jnp.dot(jnp.concatenate([hi, lo], axis=1), e_ref[...], preferred_element_type=F32)
    dt_e = spread[0:L]
    exp_a_e = spread[L:2 * L]
    dte_e = spread[2 * L:3 * L]

    xd = xs * dt_e
    xd_b = xd.astype(BF16)
    xdd_b = (xd * dte_e).astype(BF16)
    prev_b = state_ref[...].astype(BF16)
    lane = lax.broadcasted_iota(jnp.int32, (L, LANES), 1)
    heads_per_group = SSM_HEADS // SSM_GROUPS
    group_width = heads_per_group * HEAD_DIM

    y_diag_parts = []
    y_off_parts = []
    s_new_parts = []
    for g in range(SSM_GROUPS):
        bg = bmat[:, g * SSM_STATE:(g + 1) * SSM_STATE]
        cg_b = cmat[:, g * SSM_STATE:(g + 1) * SSM_STATE].astype(BF16)
        bg_b = bg.astype(BF16)
        bg_t_b = bg.T.astype(BF16)
        gs = slice(g * group_width, (g + 1) * group_width)
        cb = lax.dot_general(cg_b, bg_b, (((1,), (1,)), ((), ())), preferred_element_type=F32)
        y_off_parts.append(jnp.dot(cg_b, prev_b[:, gs], preferred_element_type=F32))
        s_new_parts.append(jnp.dot(bg_t_b, xdd_b[:, gs], preferred_element_type=F32))
        for hp in range(heads_per_group // 2):
            ms = []
            for h in (g * heads_per_group + 2 * hp, g * heads_per_group + 2 * hp + 1):
                seg = a_cum[:, h:h + 1] - a_cum_t[h:h + 1, :]
                decay = jnp.exp(jnp.where(causal, seg, -jnp.inf))
                ms.append((cb * decay).astype(BF16))
            pair = (g * heads_per_group) // 2 + hp
            yp = jnp.dot(jnp.concatenate(ms, axis=0), xd_b[:, pair * LANES:(pair + 1) * LANES],
                         preferred_element_type=F32)
            y_diag_parts.append(jnp.where(lane < HEAD_DIM, yp[:L], yp[L:]))
    y_diag = jnp.concatenate(y_diag_parts, axis=1)
    y_off = jnp.concatenate(y_off_parts, axis=1)
    s_new = jnp.concatenate(s_new_parts, axis=1)

    y = y_diag + y_off * exp_a_e + dsk_ref[...] * xs
    state_ref[...] = state_ref[...] * exp_a_e[L - 1:L, :] + s_new

    y = y * _silu(z_ref[...].astype(F32))
    outs = []
    for g in range(SSM_GROUPS):
        yg = y[:, g * group_width:(g + 1) * group_width]
        ms = jnp.mean(yg * yg, axis=-1, keepdims=True)
        outs.append(yg * lax.rsqrt(ms + EPS))
    y_ref[...] = (jnp.concatenate(outs, axis=1) * ng_ref[...]).astype(BF16)


def _head_spread_matrix():
    e = np.zeros((2 * LANES, SSM_D_INNER), np.float32)
    for h in range(SSM_HEADS):
        e[h, h * HEAD_DIM:(h + 1) * HEAD_DIM] = 1.0
        e[LANES + h, h * HEAD_DIM:(h + 1) * HEAD_DIM] = 1.0
    return jnp.asarray(e, BF16)


def _pad_lanes(v):
    return jnp.zeros((1, LANES), F32).at[0, :v.shape[0]].set(v)


def _ssd(z, xbc, dt_raw, conv_w, conv_b, dt_bias, a_log, d_skip, norm_g):
    b, s, _ = z.shape
    L = SSM_CHUNK
    nb = SSD_STREAMS if b % SSD_STREAMS == 0 else 1
    row = lambda width: _resident((1, width), lambda bi, c: (0, 0))
    return pl.pallas_call(
        _ssd_kernel,
        grid=(b // nb, s // L),
        in_specs=[
            pl.BlockSpec((nb, L, SSM_D_INNER), lambda bi, c: (bi, c, 0)),
            pl.BlockSpec((nb, L, SSM_CONV_DIM), lambda bi, c: (bi, c, 0)),
            pl.BlockSpec((nb, L, LANES), lambda bi, c: (bi, c, 0)),
            _resident((SSM_CONV, SSM_CONV_DIM), lambda bi, c: (0, 0)),
            row(SSM_CONV_DIM), row(LANES), row(LANES), row(SSM_D_INNER), row(SSM_D_INNER),
            _resident((2 * LANES, SSM_D_INNER), lambda bi, c: (0, 0)),
        ],
        out_specs=pl.BlockSpec((nb, L, SSM_D_INNER), lambda bi, c: (bi, c, 0)),
        out_shape=jax.ShapeDtypeStruct((b, s, SSM_D_INNER), BF16),
        scratch_shapes=[pltpu.VMEM((nb, BF16_SUBLANES, SSM_CONV_DIM), BF16),
                        pltpu.VMEM((nb, SSM_STATE, SSM_D_INNER), F32)],
        compiler_params=_cparams(("parallel", "arbitrary")),
        name="ssd_mixer",
    )(z, xbc, dt_raw, conv_w.T, conv_b.reshape(1, -1), _pad_lanes(dt_bias), _pad_lanes(a_log),
      jnp.repeat(d_skip, HEAD_DIM).reshape(1, -1), norm_g.reshape(1, -1), _head_spread_matrix())


def _split_lane_halves(x2):
    xf = x2.astype(F32)
    lane = lax.broadcasted_iota(jnp.int32, xf.shape, 1)
    lo = jnp.where(lane < HEAD_DIM, xf, 0.0)
    hi = jnp.where(lane >= HEAD_DIM, xf, 0.0)
    return jnp.concatenate([lo, hi], axis=0).astype(BF16)


def _scores_t(kb, qs):
    return lax.dot_general(kb, qs, (((1,), (1,)), ((), ())), preferred_element_type=F32)


def _pv_t(vb, wt):
    return lax.dot_general(vb, wt, (((0,), (0,)), ((), ())), preferred_element_type=F32)


def _sb_kernel(q_ref, k_ref, v_ref, o_ref, z_ref, cs_ref, tot_ref, acc_ref, carry_ref):
    T = q_ref.shape[0]
    n_streams = q_ref.shape[1] // LANES
    i = pl.program_id(2)
    qs = [_split_lane_halves(q_ref[:, p * LANES:(p + 1) * LANES]) for p in range(n_streams)]

    key = lax.broadcasted_iota(jnp.int32, (T, 2 * T), 0)
    qry = lax.broadcasted_iota(jnp.int32, (T, 2 * T), 1)
    strict = key < jnp.where(qry >= T, qry - T, qry)
    ur = lax.broadcasted_iota(jnp.int32, (T, T), 0)
    uc = lax.broadcasted_iota(jnp.int32, (T, T), 1)
    neg_suffix = jnp.where(uc >= ur, -1.0, 0.0).astype(BF16)
    streams = range(n_streams)
    sign_bit = jnp.uint32(0x80000000)

    def scores(j_lo, nb):
        rows = pl.ds(pl.multiple_of(j_lo * T, T), nb * T)
        return [_scores_t(k_ref[rows, p * LANES:(p + 1) * LANES], qs[p]) for p in streams]

    def softplus2(z, diagonal):
        neg_abs = lax.bitcast_convert_type(lax.bitcast_convert_type(z, jnp.uint32) | sign_bit, F32)
        sp = jnp.maximum(z, 0.0) + jnp.log(1.0 + jnp.exp2(neg_abs)) * LOG2E
        if diagonal:
            sp = jnp.where(strict, sp, 0.0)
        return sp.astype(BF16)

    def suffix_sums(sp, nb):
        parts = [jnp.dot(neg_suffix, sp[u * T:(u + 1) * T], preferred_element_type=F32) for u in range(nb)]
        rows = [None] * nb
        total = parts[nb - 1][0:1, :]
        for u in reversed(range(nb - 1)):
            rows[u] = total
            total = total + parts[u][0:1, :]
        rows[nb - 1] = total
        return parts, (rows[0] if nb == 1 else jnp.concatenate(rows, axis=0))

    def weights(z, parts, after_rows, carry, diagonal):
        nb = len(parts)
        ws = []
        for u in range(nb):
            later = carry if u == nb - 1 else carry + after_rows[u:u + 1, :]
            w = jnp.exp2(z[u * T:(u + 1) * T] + (parts[u] + later))
            if diagonal:
                w = jnp.where(strict, w, 0.0)
            ws.append(w.astype(BF16))
        return ws[0] if nb == 1 else jnp.concatenate(ws, axis=0)

    def accumulate(j_lo, nb, ws, tots):
        rows = pl.ds(pl.multiple_of(j_lo * T, T), nb * T)
        for p in streams:
            acc_ref[p] += _pv_t(v_ref[rows, p * LANES:(p + 1) * LANES], ws[p])
            carry_ref[p] += tots[p]

    def direct_step(j_lo, nb, diagonal=False):
        zs = scores(j_lo, nb)
        cts = [suffix_sums(softplus2(z, diagonal), nb) for z in zs]
        ws = [weights(zs[p], cts[p][0], cts[p][1], carry_ref[p], diagonal) for p in streams]
        accumulate(j_lo, nb, ws, [ct[1][nb - 1:nb, :] for ct in cts])

    def finish_produce(slot, zs, with_diagonal=False):
        for p in streams:
            z_ref[slot, p] = zs[p]
        for p in streams:
            if with_diagonal:
                sp = jnp.concatenate([softplus2(z_ref[slot, p, 0:T, :], False),
                                      softplus2(z_ref[slot, p, T:2 * T, :], True)], axis=0)
            else:
                sp = softplus2(z_ref[slot, p], False)
            parts, rows = suffix_sums(sp, 2)
            if with_diagonal:
                parts[1] = jnp.where(strict, parts[1], -jnp.inf)
            cs_ref[slot, p] = jnp.concatenate(parts, axis=0)
            tot_ref[slot, p] = rows

    for p in streams:
        acc_ref[p] = jnp.zeros((LANES, 2 * T), F32)
        carry_ref[p] = jnp.zeros((1, 2 * T), F32)
    n_double = (i + 1) // 2
    pair_lo = lambda t: i - 1 - 2 * t

    def consume(t, slot, produce_next):
        ws = [weights(z_ref[slot, p], [cs_ref[slot, p, 0:T, :], cs_ref[slot, p, T:2 * T, :]],
                      tot_ref[slot, p], carry_ref[p], False) for p in streams]
        tots = [tot_ref[slot, p, 1:2, :] for p in streams]
        if produce_next:
            finish_produce(1 - slot, scores(pair_lo(t + 1), 2))
        accumulate(pair_lo(t), 2, ws, tots)

    @pl.when(n_double > 0)
    def _():
        finish_produce(0, scores(pair_lo(0), 2), with_diagonal=True)

    n_twice = jnp.maximum(n_double - 1, 0) // 2

    def body(u, _):
        consume(2 * u, 0, True)
        consume(2 * u + 1, 1, True)
        return 0

    lax.fori_loop(0, n_twice, body, 0)
    left = n_double - 2 * n_twice

    @pl.when(left == 2)
    def _():
        consume(2 * n_twice, 0, True)
        consume(2 * n_twice + 1, 1, False)

    @pl.when(left == 1)
    def _():
        consume(2 * n_twice, 0, False)

    @pl.when(jnp.logical_and(i + 1 - 2 * n_double > 0, i > 0))
    def _():
        direct_step(0, 1)

    @pl.when(i == 0)
    def _():
        direct_step(0, 1, diagonal=True)

    for p in range(n_streams):
        acc = acc_ref[p]
        pair_t = jnp.concatenate([acc[:HEAD_DIM, :T], acc[HEAD_DIM:, T:]], axis=0)
        o_ref[:, p * LANES:(p + 1) * LANES] = pair_t.T.astype(BF16)


def _stick_breaking(q, k, v):
    b, s, width = q.shape
    T = min(ATT_BLOCK, s)
    gw = SB_STREAMS * LANES
    return pl.pallas_call(
        _sb_kernel,
        grid=(b, width // gw, s // T),
        in_specs=[
            pl.BlockSpec((None, T, gw), lambda bi, g, i: (bi, i, g)),
            pl.BlockSpec((None, s, gw), lambda bi, g, i: (bi, 0, g)),
            pl.BlockSpec((None, s, gw), lambda bi, g, i: (bi, 0, g)),
        ],
        out_specs=pl.BlockSpec((None, T, gw), lambda bi, g, i: (bi, i, g)),
        out_shape=jax.ShapeDtypeStruct((b, s, width), BF16),
        scratch_shapes=[
            pltpu.VMEM((2, SB_STREAMS, 2 * T, 2 * T), F32),
            pltpu.VMEM((2, SB_STREAMS, 2 * T, 2 * T), F32),
            pltpu.VMEM((2, SB_STREAMS, 2, 2 * T), F32),
            pltpu.VMEM((SB_STREAMS, LANES, 2 * T), F32),
            pltpu.VMEM((SB_STREAMS, 1, 2 * T), F32),
        ],
        compiler_params=_cparams(("parallel", "parallel", "arbitrary")),
        name="stick_breaking_attention",
    )(q, k, v)


def _bucket_starts():
    max_exact = N_REL_BUCKETS // 2
    dist = np.arange(0, 4 * REL_MAX_DIST, dtype=np.float64)
    ratio = np.log(np.maximum(dist, max_exact) / max_exact) / math.log(REL_MAX_DIST / max_exact)
    scaled = ratio * (N_REL_BUCKETS - max_exact)
    frac = np.abs(scaled - np.round(scaled))
    interior = (dist > max_exact) & (dist < REL_MAX_DIST)
    assert frac[interior].min() > 1e-3, "a bucket edge sits on an integer distance"
    large = np.minimum(max_exact + np.floor(scaled + 1e-9).astype(np.int64), N_REL_BUCKETS - 1)
    bucket = np.where(dist < max_exact, dist.astype(np.int64), large)
    assert np.all(np.diff(bucket) >= 0) and bucket[-1] == N_REL_BUCKETS - 1
    return [int(np.argmax(bucket >= bkt)) for bkt in range(N_REL_BUCKETS)]


def _bias_table_kernel(rel_ref, o_ref, *, starts):
    h = pl.program_id(0)
    T = o_ref.shape[-1]
    key = lax.broadcasted_iota(jnp.int32, (T, T), 0)
    qry = lax.broadcasted_iota(jnp.int32, (T, T), 1)
    far = rel_ref[N_REL_BUCKETS - 1, h]
    for o_blk in range(o_ref.shape[0]):
        dist = qry - key + o_blk * T
        bias = jnp.full((T, T), rel_ref[0, h], F32)
        for bkt in range(1, N_REL_BUCKETS):
            bias = jnp.where(dist >= starts[bkt], rel_ref[bkt, h], bias)
        o_ref[o_blk] = jnp.where(dist >= 0, (bias - far) * LOG2E, -jnp.inf)


def _bias_tables(rel_bias, T):
    starts = _bucket_starts()
    assert T + 1 >= starts[-1]
    return pl.pallas_call(
        functools.partial(_bias_table_kernel, starts=starts),
        grid=(DIFF_HEADS,),
        in_specs=[pl.BlockSpec(memory_space=pltpu.SMEM)],
        out_specs=pl.BlockSpec((None, 2, T, T), lambda h: (h, 0, 0, 0)),
        out_shape=jax.ShapeDtypeStruct((DIFF_HEADS, 2, T, T), F32),
        compiler_params=_cparams(("arbitrary",)),
        name="t5_bias_tables",
    )(rel_bias)


def _diff_kernel(q_ref, k_ref, v_ref, tb_ref, lq1_ref, lk1_ref, lq2_ref, lk2_ref, sg_ref, o_ref,
                 s_ref, m_ref, l_ref, acc_ref, *, lambda_init):
    T = q_ref.shape[0]
    n_streams = q_ref.shape[1] // LANES
    i = pl.program_id(2)
    qs = [_split_lane_halves(q_ref[:, h * LANES:(h + 1) * LANES]) for h in range(n_streams)]

    heads = range(n_streams)

    def scores(j_lo, nb):
        rows = pl.ds(pl.multiple_of(j_lo * T, T), nb * T)
        return [_scores_t(k_ref[rows, h * LANES:(h + 1) * LANES], qs[h]) for h in heads]

    def softmax_part(ss, table_rows):
        out = []
        for h in heads:
            read = ss[h] if callable(ss[h]) else (lambda v=ss[h]: v)
            if table_rows is not None:
                biases = [tb_ref[h, o] for o in table_rows]
                bias = biases[0] if len(biases) == 1 else jnp.concatenate(biases, axis=0)
                s = read() + jnp.concatenate([bias, bias], axis=1)
                read = lambda v=s: v
            m_old = m_ref[h]
            m_new = jnp.maximum(m_old, jnp.max(read(), axis=0, keepdims=True))
            alpha = jnp.exp2(m_old - m_new)
            p = jnp.exp2(read() - m_new)
            m_ref[h] = m_new
            l_ref[h] = alpha * l_ref[h] + jnp.sum(p, axis=0, keepdims=True)
            out.append((alpha, p.astype(BF16)))
        return out

    def accumulate(j_lo, nb, aps):
        rows = pl.ds(pl.multiple_of(j_lo * T, T), nb * T)
        for h in heads:
            alpha, p = aps[h]
            acc_ref[h] = alpha * acc_ref[h] + _pv_t(v_ref[rows, h * LANES:(h + 1) * LANES], p)

    for h in heads:
        m_ref[h] = jnp.full((1, 2 * T), -jnp.inf, F32)
        l_ref[h] = jnp.zeros((1, 2 * T), F32)
        acc_ref[h] = jnp.zeros((DIFF_V_DIM, 2 * T), F32)
    n_far = jnp.maximum(i - 1, 0)
    n_double = n_far // 2
    first = n_far - 2 * n_double
    pair_lo = lambda t: first + 2 * t

    def produce(t, slot):
        for h, s in enumerate(scores(pair_lo(t), 2)):
            s_ref[slot, h] = s

    def consume(t, slot, produce_next, table_rows=None):
        aps = softmax_part([lambda h=h: s_ref[slot, h] for h in heads], table_rows)
        if produce_next:
            produce(t + 1, 1 - slot)
        accumulate(pair_lo(t), 2, aps)

    @pl.when(first > 0)
    def _():
        accumulate(0, 1, softmax_part(scores(0, 1), None))

    @pl.when(i > 0)
    def _():
        produce(0, 0)

    n_twice = n_double // 2

    def body(u, _):
        consume(2 * u, 0, True)
        consume(2 * u + 1, 1, True)
        return 0

    lax.fori_loop(0, n_twice, body, 0)
    left = n_double - 2 * n_twice

    @pl.when(jnp.logical_and(i > 0, left == 1))
    def _():
        consume(2 * n_twice, 0, True)
        consume(2 * n_twice + 1, 1, False, [1, 0])

    @pl.when(jnp.logical_and(i > 0, left == 0))
    def _():
        consume(2 * n_twice, 0, False, [1, 0])

    @pl.when(i == 0)
    def _():
        accumulate(0, 1, softmax_part(scores(0, 1), [0]))

    lam = (jnp.exp(jnp.sum(lq1_ref[...] * lk1_ref[...], axis=-1, keepdims=True))
           - jnp.exp(jnp.sum(lq2_ref[...] * lk2_ref[...], axis=-1, keepdims=True)) + lambda_init)
    for h in range(n_streams):
        l, acc = l_ref[h], acc_ref[h]
        o_t = acc[:, :T] / l[:, :T] - lam * (acc[:, T:] / l[:, T:])
        o = o_t.T
        ms = jnp.mean(o * o, axis=-1, keepdims=True)
        o = o * lax.rsqrt(ms + EPS) * sg_ref[...] * (1.0 - lambda_init)
        o_ref[:, h * LANES:(h + 1) * LANES] = o.astype(BF16)


def _diff_attention(q, k, v, tables, lq1, lk1, lq2, lk2, subln_g, lambda_init):
    b, s, width = q.shape
    T = tables.shape[-1]
    vec = lambda n: _resident((1, n), lambda bi, g, i: (0, 0))
    gw = DIFF_STREAMS * LANES
    return pl.pallas_call(
        functools.partial(_diff_kernel, lambda_init=lambda_init),
        grid=(b, width // gw, s // T),
        in_specs=[
            pl.BlockSpec((None, T, gw), lambda bi, g, i: (bi, i, g)),
            pl.BlockSpec((None, s, gw), lambda bi, g, i: (bi, 0, g)),
            pl.BlockSpec((None, s, gw), lambda bi, g, i: (bi, 0, g)),
            pl.BlockSpec((DIFF_STREAMS,) + tables.shape[1:], lambda bi, g, i: (g, 0, 0, 0)),
            vec(HEAD_DIM), vec(HEAD_DIM), vec(HEAD_DIM), vec(HEAD_DIM), vec(DIFF_V_DIM),
        ],
        out_specs=pl.BlockSpec((None, T, gw), lambda bi, g, i: (bi, i, g)),
        out_shape=jax.ShapeDtypeStruct((b, s, width), BF16),
        scratch_shapes=[
            pltpu.VMEM((2, DIFF_STREAMS, 2 * T, 2 * T), F32),
            pltpu.VMEM((DIFF_STREAMS, 1, 2 * T), F32),
            pltpu.VMEM((DIFF_STREAMS, 1, 2 * T), F32),
            pltpu.VMEM((DIFF_STREAMS, DIFF_V_DIM, 2 * T), F32),
        ],
        compiler_params=_cparams(("parallel", "parallel", "arbitrary")),
        name="differential_attention",
    )(q, k, v, tables, lq1.reshape(1, -1), lk1.reshape(1, -1), lq2.reshape(1, -1), lk2.reshape(1, -1),
      subln_g.reshape(1, -1))


def _outproj_kernel(x_ref, mod_ref, ys_ref, yb_ref, yd_ref, w_ref, o_ref):
    o1 = SSM_D_INNER
    o2 = o1 + SB_WIDTH
    y = jnp.dot(ys_ref[...], w_ref[0:o1, :].astype(BF16), preferred_element_type=F32)
    y = y + jnp.dot(yb_ref[...], w_ref[o1:o2, :].astype(BF16), preferred_element_type=F32)
    y = y + jnp.dot(yd_ref[...], w_ref[o2:, :].astype(BF16), preferred_element_type=F32)
    o_ref[...] = x_ref[...] + mod_ref[5:6, :] * y


def _outproj(x, mod, y_ssm, y_sb, y_diff, w_out, layer):
    b, s, d = x.shape
    tm = min(TOKEN_TILE, s)
    tok = lambda width: pl.BlockSpec((None, tm, width), lambda bi, i: (bi, i, 0))
    return pl.pallas_call(
        _outproj_kernel,
        grid=(b, s // tm),
        in_specs=[
            tok(d),
            pl.BlockSpec((None, N_MOD, d), lambda bi, i: (bi, 0, 0)),
            tok(SSM_D_INNER), tok(SB_WIDTH), tok(DIFF_WIDTH),
            _resident((None,) + w_out.shape[1:], lambda bi, i: (layer, 0, 0)),
        ],
        out_specs=tok(d),
        out_shape=jax.ShapeDtypeStruct((b, s, d), F32),
        compiler_params=_cparams(("parallel", "parallel")),
        name="mixer_outproj",
    )(x, mod, y_ssm, y_sb, y_diff, w_out)


def kernel(x, c, ada_w, ada_b, ffn1_norm, ffn1_w13, ffn1_w2, mix_norm, w_in, ssm_conv_w, ssm_conv_b, ssm_dt_bias, ssm_a_log, ssm_d, ssm_norm, diff_lambda_q1, diff_lambda_k1, diff_lambda_q2, diff_lambda_k2, diff_subln, rel_bias, w_out, ffn2_norm, ffn2_w13, ffn2_w2, final_norm):
    depth = ada_w.shape[0]
    s = x.shape[1]
    mods = _ada_modulation(c, ada_w, ada_b)
    tables = _bias_tables(rel_bias, min(ATT_BLOCK, s))
    w_in_packed = _pack_w_in(w_in)
    for l in range(depth):
        mod = mods[l]
        x = _ffn(x, mod, ffn1_norm[l], ffn1_w13, ffn1_w2, l, mod_row=0)
        lambda_init = 0.8 - 0.6 * math.exp(-0.3 * l)
        z, xbc, dt_raw, sq, sk, sv, dq, dk, dv = _inproj(x, mod, mix_norm[l], w_in_packed, l)
        y_ssm = _ssd(z, xbc, dt_raw, ssm_conv_w[l], ssm_conv_b[l], ssm_dt_bias[l], ssm_a_log[l], ssm_d[l],
                     ssm_norm[l])
        y_sb = _stick_breaking(sq, sk, sv)
        y_diff = _diff_attention(dq, dk, dv, tables, diff_lambda_q1[l], diff_lambda_k1[l], diff_lambda_q2[l],
                                 diff_lambda_k2[l], diff_subln[l], lambda_init)
        x = _outproj(x, mod, y_ssm, y_sb, y_diff, w_out, l)
        x = _ffn(x, mod, ffn2_norm[l], ffn2_w13, ffn2_w2, l, mod_row=6,
                 final_g=final_norm if l == depth - 1 else None)
    return x
```

```python
import functools
import math

import numpy as np
import jax
import jax.numpy as jnp
from jax import lax
from jax.experimental import pallas as pl
from jax.experimental.pallas import tpu as pltpu

F32 = jnp.float32
BF16 = jnp.bfloat16
HIGHEST = lax.Precision.HIGHEST
LOG2E = math.log2(math.e)

HEAD_DIM = 64
SSM_HEADS = 16
SSM_GROUPS = 2
SSM_STATE = 128
SSM_CONV = 4
SSM_CHUNK = 128
SSM_D_INNER = SSM_HEADS * HEAD_DIM
SSM_CONV_DIM = SSM_D_INNER + 2 * SSM_GROUPS * SSM_STATE
SB_HEADS = 8
SB_WIDTH = SB_HEADS * HEAD_DIM
DIFF_HEADS = 4
DIFF_V_DIM = 2 * HEAD_DIM
DIFF_WIDTH = DIFF_HEADS * DIFF_V_DIM
N_MOD = 9
N_REL_BUCKETS = 32
REL_MAX_DIST = 128
EPS = 1e-6

LANES = 128
BF16_SUBLANES = 16
VMEM_LIMIT = 56 * 1024 * 1024

TOKEN_TILE = 512
FF_CHUNK = 256
ATT_BLOCK = 256

OFF_Z = 0
OFF_XBC = OFF_Z + SSM_D_INNER
OFF_SQ = OFF_XBC + SSM_CONV_DIM
OFF_SK = OFF_SQ + SB_WIDTH
OFF_SV = OFF_SK + SB_WIDTH
OFF_DQ = OFF_SV + SB_WIDTH
OFF_DK = OFF_DQ + DIFF_WIDTH
OFF_DV = OFF_DK + DIFF_WIDTH
OFF_DT = OFF_DV + DIFF_WIDTH
IN_PACKED = OFF_DT + LANES


def _cparams(semantics):
    return pltpu.CompilerParams(dimension_semantics=semantics, vmem_limit_bytes=VMEM_LIMIT)


SSD_STREAMS = 4
SB_STREAMS = 4
DIFF_STREAMS = 4


def _resident(block_shape, index_map):
    return pl.BlockSpec(block_shape, index_map, pipeline_mode=pl.Buffered(1))


def _silu(x):
    hx = 0.5 * x
    return hx + hx * jnp.tanh(hx)


def _softplus(x):
    return jnp.maximum(x, 0.0) + jnp.log1p(jnp.exp(-jnp.abs(x)))


def _norm_modulate(x, g, shift, scale):
    ms = jnp.mean(x * x, axis=-1, keepdims=True)
    y = x * lax.rsqrt(ms + EPS) * g
    return y * (1.0 + scale) + shift


def _ada_kernel(c_ref, w_ref, b_ref, o_ref):
    cond = _silu(c_ref[...])
    o_ref[...] = jnp.dot(cond, w_ref[...], preferred_element_type=F32, precision=HIGHEST) + b_ref[...]


def _ada_modulation(c, ada_w, ada_b):
    depth, d, nmod = ada_w.shape
    b = c.shape[0]
    rows = 8 * pl.cdiv(b, 8)
    c_pad = jnp.zeros((rows, d), F32).at[:b].set(c)
    out = pl.pallas_call(
        _ada_kernel,
        grid=(depth, nmod // d),
        in_specs=[
            pl.BlockSpec((rows, d), lambda l, j: (0, 0)),
            pl.BlockSpec((None, d, d), lambda l, j: (l, 0, j)),
            pl.BlockSpec((None, 1, d), lambda l, j: (l, 0, j)),
        ],
        out_specs=pl.BlockSpec((None, rows, d), lambda l, j: (l, 0, j)),
        out_shape=jax.ShapeDtypeStruct((depth, rows, nmod), F32),
        compiler_params=_cparams(("arbitrary", "arbitrary")),
        name="ada_modulation",
    )(c_pad, ada_w, ada_b.reshape(depth, 1, nmod))
    return out[:, :b].reshape(depth, b, N_MOD, d)


def _ffn_kernel(x_ref, mod_ref, g_ref, w13_ref, w2_ref, *rest, mod_row, final_norm):
    if final_norm:
        fg_ref, o_ref, acc_ref = rest
    else:
        o_ref, acc_ref = rest
    x = x_ref[...]
    shift = mod_ref[mod_row:mod_row + 1, :]
    scale = mod_ref[mod_row + 1:mod_row + 2, :]
    gate = mod_ref[mod_row + 2:mod_row + 3, :]
    h = _norm_modulate(x, g_ref[...], shift, scale).astype(BF16)
    d_ff = w2_ref.shape[0]
    for j in range(d_ff // FF_CHUNK):
        cols = slice(j * FF_CHUNK, (j + 1) * FF_CHUNK)
        w1 = w13_ref[:, cols].astype(BF16)
        w3 = w13_ref[:, d_ff + j * FF_CHUNK:d_ff + (j + 1) * FF_CHUNK].astype(BF16)
        a = jnp.dot(h, w1, preferred_element_type=F32)
        u = jnp.dot(h, w3, preferred_element_type=F32)
        act = (_silu(a) * u).astype(BF16)
        part = jnp.dot(act, w2_ref[cols, :].astype(BF16), preferred_element_type=F32)
        if j == 0:
            acc_ref[...] = part
        else:
            acc_ref[...] += part
    y = x + (0.5 * gate) * acc_ref[...]
    if final_norm:
        ms = jnp.mean(y * y, axis=-1, keepdims=True)
        y = y * lax.rsqrt(ms + EPS) * fg_ref[...]
    o_ref[...] = y


def _ffn(x, mod, g, w13, w2, layer, mod_row, final_g=None):
    b, s, d = x.shape
    d_ff = w2.shape[1]
    assert d_ff % FF_CHUNK == 0 and d_ff % LANES == 0
    tm = min(TOKEN_TILE, s)
    final_norm = final_g is not None
    in_specs = [
        pl.BlockSpec((None, tm, d), lambda bi, i: (bi, i, 0)),
        pl.BlockSpec((None, N_MOD, d), lambda bi, i: (bi, 0, 0)),
        _resident((1, d), lambda bi, i: (0, 0)),
        _resident((None, d, 2 * d_ff), lambda bi, i: (layer, 0, 0)),
        _resident((None, d_ff, d), lambda bi, i: (layer, 0, 0)),
    ]
    args = [x, mod, g.reshape(1, d), w13, w2]
    if final_norm:
        in_specs.append(_resident((1, d), lambda bi, i: (0, 0)))
        args.append(final_g.reshape(1, d))
    return pl.pallas_call(
        functools.partial(_ffn_kernel, mod_row=mod_row, final_norm=final_norm),
        grid=(b, s // tm),
        in_specs=in_specs,
        out_specs=pl.BlockSpec((None, tm, d), lambda bi, i: (bi, i, 0)),
        out_shape=jax.ShapeDtypeStruct((b, s, d), F32),
        scratch_shapes=[pltpu.VMEM((tm, d), F32)],
        compiler_params=_cparams(("parallel", "parallel")),
        name="ffn_final" if final_norm else "ffn",
    )(*args)


def _inproj_kernel(x_ref, mod_ref, g_ref, w_ref, z_ref, xbc_ref, dt_ref,
                   sq_ref, sk_ref, sv_ref, dq_ref, dk_ref, dv_ref):
    x = x_ref[...]
    h = _norm_modulate(x, g_ref[...], mod_ref[3:4, :], mod_ref[4:5, :]).astype(BF16)

    def proj(off, width):
        return jnp.dot(h, w_ref[:, off:off + width], preferred_element_type=F32)

    qk_scale = HEAD_DIM ** -0.5 * LOG2E
    z_ref[...] = proj(OFF_Z, SSM_D_INNER).astype(BF16)
    xbc_ref[...] = proj(OFF_XBC, SSM_CONV_DIM).astype(BF16)
    dt_ref[...] = proj(OFF_DT, LANES)
    sq_ref[...] = (proj(OFF_SQ, SB_WIDTH) * qk_scale).astype(BF16)
    sk_ref[...] = proj(OFF_SK, SB_WIDTH).astype(BF16)
    sv_ref[...] = proj(OFF_SV, SB_WIDTH).astype(BF16)
    dq_ref[...] = (proj(OFF_DQ, DIFF_WIDTH) * qk_scale).astype(BF16)
    dk_ref[...] = proj(OFF_DK, DIFF_WIDTH).astype(BF16)
    dv_ref[...] = proj(OFF_DV, DIFF_WIDTH).astype(BF16)


def _pack_kernel(w_ref, o_ref):
    dt0 = SSM_D_INNER + SSM_CONV_DIM
    dt1 = dt0 + SSM_HEADS
    w = w_ref[...]
    rows = w.shape[0]
    o_ref[:, :dt0] = w[:, :dt0].astype(BF16)
    o_ref[:, dt0:OFF_DT] = w[:, dt1:].astype(BF16)
    o_ref[:, OFF_DT:] = jnp.concatenate(
        [w[:, dt0:dt1], jnp.zeros((rows, LANES - SSM_HEADS), F32)], axis=1).astype(BF16)


def _pack_w_in(w_in):
    depth, d, n_in = w_in.shape
    rows = 128
    return pl.pallas_call(
        _pack_kernel,
        grid=(depth, d // rows),
        in_specs=[pl.BlockSpec((None, rows, n_in), lambda l, r: (l, r, 0))],
        out_specs=pl.BlockSpec((None, rows, IN_PACKED), lambda l, r: (l, r, 0)),
        out_shape=jax.ShapeDtypeStruct((depth, d, IN_PACKED), BF16),
        compiler_params=_cparams(("arbitrary", "arbitrary")),
        name="pack_w_in",
    )(w_in)


def _inproj(x, mod, g, w, layer):
    b, s, d = x.shape
    tm = min(TOKEN_TILE, s)
    widths = (SSM_D_INNER, SSM_CONV_DIM, LANES) + (SB_WIDTH,) * 3 + (DIFF_WIDTH,) * 3
    dtypes = (BF16, BF16, F32) + (BF16,) * 6
    return pl.pallas_call(
        _inproj_kernel,
        grid=(b, s // tm),
        in_specs=[
            pl.BlockSpec((None, tm, d), lambda bi, i: (bi, i, 0)),
            pl.BlockSpec((None, N_MOD, d), lambda bi, i: (bi, 0, 0)),
            _resident((1, d), lambda bi, i: (0, 0)),
            _resident((None, d, IN_PACKED), lambda bi, i: (layer, 0, 0)),
        ],
        out_specs=[pl.BlockSpec((None, tm, wd), lambda bi, i: (bi, i, 0)) for wd in widths],
        out_shape=[jax.ShapeDtypeStruct((b, s, wd), dt) for wd, dt in zip(widths, dtypes)],
        compiler_params=_cparams(("parallel", "parallel")),
        name="mixer_inproj",
    )(x, mod, g.reshape(1, d), w)


def _ssd_kernel(z_ref, xbc_ref, dtr_ref, cw_ref, cb_ref, dtb_ref, alog_ref, dsk_ref, ng_ref, e_ref,
                y_ref, tail_ref, state_ref):
    rows = range(z_ref.shape[0])
    convs = [_ssd_conv(xbc_ref.at[bb], cw_ref, cb_ref, tail_ref.at[bb], state_ref.at[bb]) for bb in rows]
    for bb in rows:
        _ssd_scan(convs[bb], z_ref.at[bb], dtr_ref.at[bb], dtb_ref, alog_ref, dsk_ref, ng_ref, e_ref,
                  y_ref.at[bb], state_ref.at[bb])


def _ssd_conv(xbc_ref, cw_ref, cb_ref, tail_ref, state_ref):
    L = xbc_ref.shape[0]
    pad = tail_ref.shape[0]
    c = pl.program_id(1)

    @pl.when(c == 0)
    def _():
        tail_ref[...] = jnp.zeros(tail_ref.shape, BF16)
        state_ref[...] = jnp.zeros(state_ref.shape, F32)

    u = xbc_ref[...]
    u_ext = jnp.concatenate([tail_ref[...], u], axis=0)
    tail_ref[...] = u[L - pad:, :]
    trow = lax.broadcasted_iota(jnp.int32, (L, pad + L), 0)
    tcol = lax.broadcasted_iota(jnp.int32, (L, pad + L), 1)
    shifts = jnp.concatenate(
        [jnp.where(tcol == trow + (pad - (SSM_CONV - 1) + k), 1.0, 0.0) for k in range(SSM_CONV)], axis=0)
    shifted = jnp.dot(shifts.astype(BF16), u_ext, preferred_element_type=F32)
    conv = cb_ref[...]
    for k in range(SSM_CONV):
        conv = conv + cw_ref[k:k + 1, :] * shifted[k * L:(k + 1) * L]
    return _silu(conv)


def _ssd_scan(xa, z_ref, dtr_ref, dtb_ref, alog_ref, dsk_ref, ng_ref, e_ref, y_ref, state_ref):
    L = z_ref.shape[0]
    xs = xa[:, :SSM_D_INNER]
    bmat = xa[:, SSM_D_INNER:SSM_D_INNER + SSM_GROUPS * SSM_STATE]
    cmat = xa[:, SSM_D_INNER + SSM_GROUPS * SSM_STATE:]

    dt = _softplus(dtr_ref[...] + dtb_ref[...])
    a = -jnp.exp(alog_ref[...])
    da = dt * a
    row = lax.broadcasted_iota(jnp.int32, (L, L), 0)
    col = lax.broadcasted_iota(jnp.int32, (L, L), 1)
    causal = row >= col
    a_cum = jnp.dot(causal.astype(F32), da, preferred_element_type=F32, precision=HIGHEST)
    a_cum_t = a_cum.T
    exp_a = jnp.exp(a_cum)
    decay_to_end = jnp.exp(a_cum[L - 1:L, :] - a_cum)

    stacked = jnp.concatenate([dt, exp_a, decay_to_end], axis=0)
    hi = stacked.astype(BF16)
    lo = (stacked - hi.astype(F32)).astype(BF16)
    spread = jnp.dot(jnp.concatenate([hi, lo], axis=1), e_ref[...], preferred_element_type=F32)
    dt_e = spread[0:L]
    exp_a_e = spread[L:2 * L]
    dte_e = spread[2 * L:3 * L]

    xd = xs * dt_e
    xd_b = xd.astype(BF16)
    xdd_b = (xd * dte_e).astype(BF16)
    prev_b = state_ref[...].astype(BF16)
    lane = lax.broadcasted_iota(jnp.int32, (L, LANES), 1)
    heads_per_group = SSM_HEADS // SSM_GROUPS
    group_width = heads_per_group * HEAD_DIM

    y_diag_parts = []
    y_off_parts = []
    s_new_parts = []
    for g in range(SSM_GROUPS):
        bg = bmat[:, g * SSM_STATE:(g + 1) * SSM_STATE]
        cg_b = cmat[:, g * SSM_STATE:(g + 1) * SSM_STATE].astype(BF16)
        bg_b = bg.astype(BF16)
        bg_t_b = bg.T.astype(BF16)
        gs = slice(g * group_width, (g + 1) * group_width)
        cb = lax.dot_general(cg_b, bg_b, (((1,), (1,)), ((), ())), preferred_element_type=F32)
        y_off_parts.append(jnp.dot(cg_b, prev_b[:, gs], preferred_element_type=F32))
        s_new_parts.append(jnp.dot(bg_t_b, xdd_b[:, gs], preferred_element_type=F32))
        for hp in range(heads_per_group // 2):
            ms = []
            for h in (g * heads_per_group + 2 * hp, g * heads_per_group + 2 * hp + 1):
                seg = a_cum[:, h:h + 1] - a_cum_t[h:h + 1, :]
                decay = jnp.exp(jnp.where(causal, seg, -jnp.inf))
                ms.append((cb * decay).astype(BF16))
            pair = (g * heads_per_group) // 2 + hp
            yp = jnp.dot(jnp.concatenate(ms, axis=0), xd_b[:, pair * LANES:(pair + 1) * LANES],
                         preferred_element_type=F32)
            y_diag_parts.append(jnp.where(lane < HEAD_DIM, yp[:L], yp[L:]))
    y_diag = jnp.concatenate(y_diag_parts, axis=1)
    y_off = jnp.concatenate(y_off_parts, axis=1)
    s_new = jnp.concatenate(s_new_parts, axis=1)

    y = y_diag + y_off * exp_a_e + dsk_ref[...] * xs
    state_ref[...] = state_ref[...] * exp_a_e[L - 1:L, :] + s_new

    y = y * _silu(z_ref[...].astype(F32))
    outs = []
    for g in range(SSM_GROUPS):
        yg = y[:, g * group_width:(g + 1) * group_width]
        ms = jnp.mean(yg * yg, axis=-1, keepdims=True)
        outs.append(yg * lax.rsqrt(ms + EPS))
    y_ref[...] = (jnp.concatenate(outs, axis=1) * ng_ref[...]).astype(BF16)


def _head_spread_matrix():
    e = np.zeros((2 * LANES, SSM_D_INNER), np.float32)
    for h in range(SSM_HEADS):
        e[h, h * HEAD_DIM:(h + 1) * HEAD_DIM] = 1.0
        e[LANES + h, h * HEAD_DIM:(h + 1) * HEAD_DIM] = 1.0
    return jnp.asarray(e, BF16)


def _pad_lanes(v):
    return jnp.zeros((1, LANES), F32).at[0, :v.shape[0]].set(v)


def _ssd(z, xbc, dt_raw, conv_w, conv_b, dt_bias, a_log, d_skip, norm_g):
    b, s, _ = z.shape
    L = SSM_CHUNK
    nb = SSD_STREAMS if b % SSD_STREAMS == 0 else 1
    row = lambda width: _resident((1, width), lambda bi, c: (0, 0))
    return pl.pallas_call(
        _ssd_kernel,
        grid=(b // nb, s // L),
        in_specs=[
            pl.BlockSpec((nb, L, SSM_D_INNER), lambda bi, c: (bi, c, 0)),
            pl.BlockSpec((nb, L, SSM_CONV_DIM), lambda bi, c: (bi, c, 0)),
            pl.BlockSpec((nb, L, LANES), lambda bi, c: (bi, c, 0)),
            _resident((SSM_CONV, SSM_CONV_DIM), lambda bi, c: (0, 0)),
            row(SSM_CONV_DIM), row(LANES), row(LANES), row(SSM_D_INNER), row(SSM_D_INNER),
            _resident((2 * LANES, SSM_D_INNER), lambda bi, c: (0, 0)),
        ],
        out_specs=pl.BlockSpec((nb, L, SSM_D_INNER), lambda bi, c: (bi, c, 0)),
        out_shape=jax.ShapeDtypeStruct((b, s, SSM_D_INNER), BF16),
        scratch_shapes=[pltpu.VMEM((nb, BF16_SUBLANES, SSM_CONV_DIM), BF16),
                        pltpu.VMEM((nb, SSM_STATE, SSM_D_INNER), F32)],
        compiler_params=_cparams(("parallel", "arbitrary")),
        name="ssd_mixer",
    )(z, xbc, dt_raw, conv_w.T, conv_b.reshape(1, -1), _pad_lanes(dt_bias), _pad_lanes(a_log),
      jnp.repeat(d_skip, HEAD_DIM).reshape(1, -1), norm_g.reshape(1, -1), _head_spread_matrix())


def _split_lane_halves(x2):
    xf = x2.astype(F32)
    lane = lax.broadcasted_iota(jnp.int32, xf.shape, 1)
    lo = jnp.where(lane < HEAD_DIM, xf, 0.0)
    hi = jnp.where(lane >= HEAD_DIM, xf, 0.0)
    return jnp.concatenate([lo, hi], axis=0).astype(BF16)


def _scores_t(kb, qs):
    return lax.dot_general(kb, qs, (((1,), (1,)), ((), ())), preferred_element_type=F32)


def _pv_t(vb, wt):
    return lax.dot_general(vb, wt, (((0,), (0,)), ((), ())), preferred_element_type=F32)


def _sb_kernel(q_ref, k_ref, v_ref, o_ref, z_ref, cs_ref, tot_ref, acc_ref, carry_ref):
    T = q_ref.shape[0]
    n_streams = q_ref.shape[1] // LANES
    i = pl.program_id(2)
    qs = [_split_lane_halves(q_ref[:, p * LANES:(p + 1) * LANES]) for p in range(n_streams)]

    key = lax.broadcasted_iota(jnp.int32, (T, 2 * T), 0)
    qry = lax.broadcasted_iota(jnp.int32, (T, 2 * T), 1)
    strict = key < jnp.where(qry >= T, qry - T, qry)
    ur = lax.broadcasted_iota(jnp.int32, (T, T), 0)
    uc = lax.broadcasted_iota(jnp.int32, (T, T), 1)
    neg_suffix = jnp.where(uc >= ur, -1.0, 0.0).astype(BF16)
    streams = range(n_streams)
    sign_bit = jnp.uint32(0x80000000)

    def scores(j_lo, nb):
        rows = pl.ds(pl.multiple_of(j_lo * T, T), nb * T)
        return [_scores_t(k_ref[rows, p * LANES:(p + 1) * LANES], qs[p]) for p in streams]

    def softplus2(z, diagonal):
        neg_abs = lax.bitcast_convert_type(lax.bitcast_convert_type(z, jnp.uint32) | sign_bit, F32)
        sp = jnp.maximum(z, 0.0) + jnp.log(1.0 + jnp.exp2(neg_abs)) * LOG2E
        if diagonal:
            sp = jnp.where(strict, sp, 0.0)
        return sp.astype(BF16)

    def suffix_sums(sp, nb):
        parts = [jnp.dot(neg_suffix, sp[u * T:(u + 1) * T], preferred_element_type=F32) for u in range(nb)]
        rows = [None] * nb
        total = parts[nb - 1][0:1, :]
        for u in reversed(range(nb - 1)):
            rows[u] = total
            total = total + parts[u][0:1, :]
        rows[nb - 1] = total
        return parts, (rows[0] if nb == 1 else jnp.concatenate(rows, axis=0))

    def weights(z, parts, after_rows, carry, diagonal):
        nb = len(parts)
        ws = []
        for u in range(nb):
            later = carry if u == nb - 1 else carry + after_rows[u:u + 1, :]
            w = jnp.exp2(z[u * T:(u + 1) * T] + (parts[u] + later))
            if diagonal:
                w = jnp.where(strict, w, 0.0)
            ws.append(w.astype(BF16))
        return ws[0] if nb == 1 else jnp.concatenate(ws, axis=0)

    def accumulate(j_lo, nb, ws, tots):
        rows = pl.ds(pl.multiple_of(j_lo * T, T), nb * T)
        for p in streams:
            acc_ref[p] += _pv_t(v_ref[rows, p * LANES:(p + 1) * LANES], ws[p])
            carry_ref[p] += tots[p]

    def direct_step(j_lo, nb, diagonal=False):
        zs = scores(j_lo, nb)
        cts = [suffix_sums(softplus2(z, diagonal), nb) for z in zs]
        ws = [weights(zs[p], cts[p][0], cts[p][1], carry_ref[p], diagonal) for p in streams]
        accumulate(j_lo, nb, ws, [ct[1][nb - 1:nb, :] for ct in cts])

    def finish_produce(slot, zs, with_diagonal=False):
        for p in streams:
            z_ref[slot, p] = zs[p]
        for p in streams:
            if with_diagonal:
                sp = jnp.concatenate([softplus2(z_ref[slot, p, 0:T, :], False),
                                      softplus2(z_ref[slot, p, T:2 * T, :], True)], axis=0)
            else:
                sp = softplus2(z_ref[slot, p], False)
            parts, rows = suffix_sums(sp, 2)
            if with_diagonal:
                parts[1] = jnp.where(strict, parts[1], -jnp.inf)
            cs_ref[slot, p] = jnp.concatenate(parts, axis=0)
            tot_ref[slot, p] = rows

    for p in streams:
        acc_ref[p] = jnp.zeros((LANES, 2 * T), F32)
        carry_ref[p] = jnp.zeros((1, 2 * T), F32)
    n_double = (i + 1) // 2
    pair_lo = lambda t: i - 1 - 2 * t

    def consume(t, slot, produce_next):
        ws = [weights(z_ref[slot, p], [cs_ref[slot, p, 0:T, :], cs_ref[slot, p, T:2 * T, :]],
                      tot_ref[slot, p], carry_ref[p], False) for p in streams]
        tots = [tot_ref[slot, p, 1:2, :] for p in streams]
        if produce_next:
            finish_produce(1 - slot, scores(pair_lo(t + 1), 2))
        accumulate(pair_lo(t), 2, ws, tots)

    @pl.when(n_double > 0)
    def _():
        finish_produce(0, scores(pair_lo(0), 2), with_diagonal=True)

    n_twice = jnp.maximum(n_double - 1, 0) // 2

    def body(u, _):
        consume(2 * u, 0, True)
        consume(2 * u + 1, 1, True)
        return 0

    lax.fori_loop(0, n_twice, body, 0)
    left = n_double - 2 * n_twice

    @pl.when(left == 2)
    def _():
        consume(2 * n_twice, 0, True)
        consume(2 * n_twice + 1, 1, False)

    @pl.when(left == 1)
    def _():
        consume(2 * n_twice, 0, False)

    @pl.when(jnp.logical_and(i + 1 - 2 * n_double > 0, i > 0))
    def _():
        direct_step(0, 1)

    @pl.when(i == 0)
    def _():
        direct_step(0, 1, diagonal=True)

    for p in range(n_streams):
        acc = acc_ref[p]
        pair_t = jnp.concatenate([acc[:HEAD_DIM, :T], acc[HEAD_DIM:, T:]], axis=0)
        o_ref[:, p * LANES:(p + 1) * LANES] = pair_t.T.astype(BF16)


def _stick_breaking(q, k, v):
    b, s, width = q.shape
    T = min(ATT_BLOCK, s)
    gw = SB_STREAMS * LANES
    return pl.pallas_call(
        _sb_kernel,
        grid=(b, width // gw, s // T),
        in_specs=[
            pl.BlockSpec((None, T, gw), lambda bi, g, i: (bi, i, g)),
            pl.BlockSpec((None, s, gw), lambda bi, g, i: (bi, 0, g)),
            pl.BlockSpec((None, s, gw), lambda bi, g, i: (bi, 0, g)),
        ],
        out_specs=pl.BlockSpec((None, T, gw), lambda bi, g, i: (bi, i, g)),
        out_shape=jax.ShapeDtypeStruct((b, s, width), BF16),
        scratch_shapes=[
            pltpu.VMEM((2, SB_STREAMS, 2 * T, 2 * T), F32),
            pltpu.VMEM((2, SB_STREAMS, 2 * T, 2 * T), F32),
            pltpu.VMEM((2, SB_STREAMS, 2, 2 * T), F32),
            pltpu.VMEM((SB_STREAMS, LANES, 2 * T), F32),
            pltpu.VMEM((SB_STREAMS, 1, 2 * T), F32),
        ],
        compiler_params=_cparams(("parallel", "parallel", "arbitrary")),
        name="stick_breaking_attention",
    )(q, k, v)


def _bucket_starts():
    max_exact = N_REL_BUCKETS // 2
    dist = np.arange(0, 4 * REL_MAX_DIST, dtype=np.float64)
    ratio = np.log(np.maximum(dist, max_exact) / max_exact) / math.log(REL_MAX_DIST / max_exact)
    scaled = ratio * (N_REL_BUCKETS - max_exact)
    frac = np.abs(scaled - np.round(scaled))
    interior = (dist > max_exact) & (dist < REL_MAX_DIST)
    assert frac[interior].min() > 1e-3, "a bucket edge sits on an integer distance"
    large = np.minimum(max_exact + np.floor(scaled + 1e-9).astype(np.int64), N_REL_BUCKETS - 1)
    bucket = np.where(dist < max_exact, dist.astype(np.int64), large)
    assert np.all(np.diff(bucket) >= 0) and bucket[-1] == N_REL_BUCKETS - 1
    return [int(np.argmax(bucket >= bkt)) for bkt in range(N_REL_BUCKETS)]


def _bias_table_kernel(rel_ref, o_ref, *, starts):
    h = pl.program_id(0)
    T = o_ref.shape[-1]
    key = lax.broadcasted_iota(jnp.int32, (T, T), 0)
    qry = lax.broadcasted_iota(jnp.int32, (T, T), 1)
    far = rel_ref[N_REL_BUCKETS - 1, h]
    for o_blk in range(o_ref.shape[0]):
        dist = qry - key + o_blk * T
        bias = jnp.full((T, T), rel_ref[0, h], F32)
        for bkt in range(1, N_REL_BUCKETS):
            bias = jnp.where(dist >= starts[bkt], rel_ref[bkt, h], bias)
        o_ref[o_blk] = jnp.where(dist >= 0, (bias - far) * LOG2E, -jnp.inf)


def _bias_tables(rel_bias, T):
    starts = _bucket_starts()
    assert T + 1 >= starts[-1]
    return pl.pallas_call(
        functools.partial(_bias_table_kernel, starts=starts),
        grid=(DIFF_HEADS,),
        in_specs=[pl.BlockSpec(memory_space=pltpu.SMEM)],
        out_specs=pl.BlockSpec((None, 2, T, T), lambda h: (h, 0, 0, 0)),
        out_shape=jax.ShapeDtypeStruct((DIFF_HEADS, 2, T, T), F32),
        compiler_params=_cparams(("arbitrary",)),
        name="t5_bias_tables",
    )(rel_bias)


def _diff_kernel(q_ref, k_ref, v_ref, tb_ref, lq1_ref, lk1_ref, lq2_ref, lk2_ref, sg_ref, o_ref,
                 s_ref, m_ref, l_ref, acc_ref, *, lambda_init):
    T = q_ref.shape[0]
    n_streams = q_ref.shape[1] // LANES
    i = pl.program_id(2)
    qs = [_split_lane_halves(q_ref[:, h * LANES:(h + 1) * LANES]) for h in range(n_streams)]

    heads = range(n_streams)

    def scores(j_lo, nb):
        rows = pl.ds(pl.multiple_of(j_lo * T, T), nb * T)
        return [_scores_t(k_ref[rows, h * LANES:(h + 1) * LANES], qs[h]) for h in heads]

    def softmax_part(ss, table_rows):
        out = []
        for h in heads:
            read = ss[h] if callable(ss[h]) else (lambda v=ss[h]: v)
            if table_rows is not None:
                biases = [tb_ref[h, o] for o in table_rows]
                bias = biases[0] if len(biases) == 1 else jnp.concatenate(biases, axis=0)
                s = read() + jnp.concatenate([bias, bias], axis=1)
                read = lambda v=s: v
            m_old = m_ref[h]
            m_new = jnp.maximum(m_old, jnp.max(read(), axis=0, keepdims=True))
            alpha = jnp.exp2(m_old - m_new)
            p = jnp.exp2(read() - m_new)
            m_ref[h] = m_new
            l_ref[h] = alpha * l_ref[h] + jnp.sum(p, axis=0, keepdims=True)
            out.append((alpha, p.astype(BF16)))
        return out

    def accumulate(j_lo, nb, aps):
        rows = pl.ds(pl.multiple_of(j_lo * T, T), nb * T)
        for h in heads:
            alpha, p = aps[h]
            acc_ref[h] = alpha * acc_ref[h] + _pv_t(v_ref[rows, h * LANES:(h + 1) * LANES], p)

    for h in heads:
        m_ref[h] = jnp.full((1, 2 * T), -jnp.inf, F32)
        l_ref[h] = jnp.zeros((1, 2 * T), F32)
        acc_ref[h] = jnp.zeros((DIFF_V_DIM, 2 * T), F32)
    n_far = jnp.maximum(i - 1, 0)
    n_double = n_far // 2
    first = n_far - 2 * n_double
    pair_lo = lambda t: first + 2 * t

    def produce(t, slot):
        for h, s in enumerate(scores(pair_lo(t), 2)):
            s_ref[slot, h] = s

    def consume(t, slot, produce_next, table_rows=None):
        aps = softmax_part([lambda h=h: s_ref[slot, h] for h in heads], table_rows)
        if produce_next:
            produce(t + 1, 1 - slot)
        accumulate(pair_lo(t), 2, aps)

    @pl.when(first > 0)
    def _():
        accumulate(0, 1, softmax_part(scores(0, 1), None))

    @pl.when(i > 0)
    def _():
        produce(0, 0)

    n_twice = n_double // 2

    def body(u, _):
        consume(2 * u, 0, True)
        consume(2 * u + 1, 1, True)
        return 0

    lax.fori_loop(0, n_twice, body, 0)
    left = n_double - 2 * n_twice

    @pl.when(jnp.logical_and(i > 0, left == 1))
    def _():
        consume(2 * n_twice, 0, True)
        consume(2 * n_twice + 1, 1, False, [1, 0])

    @pl.when(jnp.logical_and(i > 0, left == 0))
    def _():
        consume(2 * n_twice, 0, False, [1, 0])

    @pl.when(i == 0)
    def _():
        accumulate(0, 1, softmax_part(scores(0, 1), [0]))

    lam = (jnp.exp(jnp.sum(lq1_ref[...] * lk1_ref[...], axis=-1, keepdims=True))
           - jnp.exp(jnp.sum(lq2_ref[...] * lk2_ref[...], axis=-1, keepdims=True)) + lambda_init)
    for h in range(n_streams):
        l, acc = l_ref[h], acc_ref[h]
        o_t = acc[:, :T] / l[:, :T] - lam * (acc[:, T:] / l[:, T:])
        o = o_t.T
        ms = jnp.mean(o * o, axis=-1, keepdims=True)
        o = o * lax.rsqrt(ms + EPS) * sg_ref[...] * (1.0 - lambda_init)
        o_ref[:, h * LANES:(h + 1) * LANES] = o.astype(BF16)


def _diff_attention(q, k, v, tables, lq1, lk1, lq2, lk2, subln_g, lambda_init):
    b, s, width = q.shape
    T = tables.shape[-1]
    vec = lambda n: _resident((1, n), lambda bi, g, i: (0, 0))
    gw = DIFF_STREAMS * LANES
    return pl.pallas_call(
        functools.partial(_diff_kernel, lambda_init=lambda_init),
        grid=(b, width // gw, s // T),
        in_specs=[
            pl.BlockSpec((None, T, gw), lambda bi, g, i: (bi, i, g)),
            pl.BlockSpec((None, s, gw), lambda bi, g, i: (bi, 0, g)),
            pl.BlockSpec((None, s, gw), lambda bi, g, i: (bi, 0, g)),
            pl.BlockSpec((DIFF_STREAMS,) + tables.shape[1:], lambda bi, g, i: (g, 0, 0, 0)),
            vec(HEAD_DIM), vec(HEAD_DIM), vec(HEAD_DIM), vec(HEAD_DIM), vec(DIFF_V_DIM),
        ],
        out_specs=pl.BlockSpec((None, T, gw), lambda bi, g, i: (bi, i, g)),
        out_shape=jax.ShapeDtypeStruct((b, s, width), BF16),
        scratch_shapes=[
            pltpu.VMEM((2, DIFF_STREAMS, 2 * T, 2 * T), F32),
            pltpu.VMEM((DIFF_STREAMS, 1, 2 * T), F32),
            pltpu.VMEM((DIFF_STREAMS, 1, 2 * T), F32),
            pltpu.VMEM((DIFF_STREAMS, DIFF_V_DIM, 2 * T), F32),
        ],
        compiler_params=_cparams(("parallel", "parallel", "arbitrary")),
        name="differential_attention",
    )(q, k, v, tables, lq1.reshape(1, -1), lk1.reshape(1, -1), lq2.reshape(1, -1), lk2.reshape(1, -1),
      subln_g.reshape(1, -1))


def _outproj_kernel(x_ref, mod_ref, ys_ref, yb_ref, yd_ref, w_ref, o_ref):
    o1 = SSM_D_INNER
    o2 = o1 + SB_WIDTH
    y = jnp.dot(ys_ref[...], w_ref[0:o1, :].astype(BF16), preferred_element_type=F32)
    y = y + jnp.dot(yb_ref[...], w_ref[o1:o2, :].astype(BF16), preferred_element_type=F32)
    y = y + jnp.dot(yd_ref[...], w_ref[o2:, :].astype(BF16), preferred_element_type=F32)
    o_ref[...] = x_ref[...] + mod_ref[5:6, :] * y


def _outproj(x, mod, y_ssm, y_sb, y_diff, w_out, layer):
    b, s, d = x.shape
    tm = min(TOKEN_TILE, s)
    tok = lambda width: pl.BlockSpec((None, tm, width), lambda bi, i: (bi, i, 0))
    return pl.pallas_call(
        _outproj_kernel,
        grid=(b, s // tm),
        in_specs=[
            tok(d),
            pl.BlockSpec((None, N_MOD, d), lambda bi, i: (bi, 0, 0)),
            tok(SSM_D_INNER), tok(SB_WIDTH), tok(DIFF_WIDTH),
            _resident((None,) + w_out.shape[1:], lambda bi, i: (layer, 0, 0)),
        ],
        out_specs=tok(d),
        out_shape=jax.ShapeDtypeStruct((b, s, d), F32),
        compiler_params=_cparams(("parallel", "parallel")),
        name="mixer_outproj",
    )(x, mod, y_ssm, y_sb, y_diff, w_out)


def kernel(x, c, ada_w, ada_b, ffn1_norm, ffn1_w13, ffn1_w2, mix_norm, w_in, ssm_conv_w, ssm_conv_b, ssm_dt_bias, ssm_a_log, ssm_d, ssm_norm, diff_lambda_q1, diff_lambda_k1, diff_lambda_q2, diff_lambda_k2, diff_subln, rel_bias, w_out, ffn2_norm, ffn2_w13, ffn2_w2, final_norm):
    depth = ada_w.shape[0]
    s = x.shape[1]
    mods = _ada_modulation(c, ada_w, ada_b)
    tables = _bias_tables(rel_bias, min(ATT_BLOCK, s))
    w_in_packed = _pack_w_in(w_in)
    for l in range(depth):
        mod = mods[l]
        x = _ffn(x, mod, ffn1_norm[l], ffn1_w13, ffn1_w2, l, mod_row=0)
        lambda_init = 0.8 - 0.6 * math.exp(-0.3 * l)
        z, xbc, dt_raw, sq, sk, sv, dq, dk, dv = _inproj(x, mod, mix_norm[l], w_in_packed, l)
        y_ssm = _ssd(z, xbc, dt_raw, ssm_conv_w[l], ssm_conv_b[l], ssm_dt_bias[l], ssm_a_log[l], ssm_d[l],
                     ssm_norm[l])
        y_sb = _stick_breaking(sq, sk, sv)
        y_diff = _diff_attention(dq, dk, dv, tables, diff_lambda_q1[l], diff_lambda_k1[l], diff_lambda_q2[l],
                                 diff_lambda_k2[l], diff_subln[l], lambda_init)
        x = _outproj(x, mod, y_ssm, y_sb, y_diff, w_out, l)
        x = _ffn(x, mod, ffn2_norm[l], ffn2_w13, ffn2_w2, l, mod_row=6,
                 final_g=final_norm if l == depth - 1 else None)
    return x
```

```python
import functools
import math

import numpy as np
import jax
import jax.numpy as jnp
from jax import lax
from jax.experimental import pallas as pl
from jax.experimental.pallas import tpu as pltpu

F32 = jnp.float32
BF16 = jnp.bfloat16
HIGHEST = lax.Precision.HIGHEST
LOG2E = math.log2(math.e)

HEAD_DIM = 64
SSM_HEADS = 16
SSM_GROUPS = 2
SSM_STATE = 128
SSM_CONV = 4
SSM_CHUNK = 128
SSM_D_INNER = SSM_HEADS * HEAD_DIM
SSM_CONV_DIM = SSM_D_INNER + 2 * SSM_GROUPS * SSM_STATE
SB_HEADS = 8
SB_WIDTH = SB_HEADS * HEAD_DIM
DIFF_HEADS = 4
DIFF_V_DIM = 2 * HEAD_DIM
DIFF_WIDTH = DIFF_HEADS * DIFF_V_DIM
N_MOD = 9
N_REL_BUCKETS = 32
REL_MAX_DIST = 128
EPS = 1e-6

LANES = 128
BF16_SUBLANES = 16
VMEM_LIMIT = 56 * 1024 * 1024

TOKEN_TILE = 512
FF_CHUNK = 256
ATT_BLOCK = 256

OFF_Z = 0
OFF_XBC = OFF_Z + SSM_D_INNER
OFF_SQ = OFF_XBC + SSM_CONV_DIM
OFF_SK = OFF_SQ + SB_WIDTH
OFF_SV = OFF_SK + SB_WIDTH
OFF_DQ = OFF_SV + SB_WIDTH
OFF_DK = OFF_DQ + DIFF_WIDTH
OFF_DV = OFF_DK + DIFF_WIDTH
OFF_DT = OFF_DV + DIFF_WIDTH
IN_PACKED = OFF_DT + LANES


def _cparams(semantics):
    return pltpu.CompilerParams(dimension_semantics=semantics, vmem_limit_bytes=VMEM_LIMIT)


SSD_STREAMS = 4
SB_STREAMS = 4
DIFF_STREAMS = 4


def _resident(block_shape, index_map):
    return pl.BlockSpec(block_shape, index_map, pipeline_mode=pl.Buffered(1))


def _silu(x):
    hx = 0.5 * x
    return hx + hx * jnp.tanh(hx)


def _softplus(x):
    return jnp.maximum(x, 0.0) + jnp.log1p(jnp.exp(-jnp.abs(x)))


def _norm_modulate(x, g, shift, scale):
    ms = jnp.mean(x * x, axis=-1, keepdims=True)
    y = x * lax.rsqrt(ms + EPS) * g
    return y * (1.0 + scale) + shift


def _ada_kernel(c_ref, w_ref, b_ref, o_ref):
    cond = _silu(c_ref[...])
    w = w_ref[...]
    c_hi = cond.astype(BF16)
    c_lo = (cond - c_hi.astype(F32)).astype(BF16)
    w_hi = w.astype(BF16)
    w_lo = (w - w_hi.astype(F32)).astype(BF16)
    acc = jnp.dot(c_hi, w_hi, preferred_element_type=F32)
    acc = acc + jnp.dot(c_lo, w_hi, preferred_element_type=F32)
    acc = acc + jnp.dot(c_hi, w_lo, preferred_element_type=F32)
    o_ref[...] = acc + b_ref[...]


def _ada_modulation(c, ada_w, ada_b):
    depth, d, nmod = ada_w.shape
    b = c.shape[0]
    rows = 8 * pl.cdiv(b, 8)
    c_pad = jnp.zeros((rows, d), F32).at[:b].set(c)
    out = pl.pallas_call(
        _ada_kernel,
        grid=(depth, nmod // d),
        in_specs=[
            pl.BlockSpec((rows, d), lambda l, j: (0, 0)),
            pl.BlockSpec((None, d, d), lambda l, j: (l, 0, j)),
            pl.BlockSpec((None, 1, d), lambda l, j: (l, 0, j)),
        ],
        out_specs=pl.BlockSpec((None, rows, d), lambda l, j: (l, 0, j)),
        out_shape=jax.ShapeDtypeStruct((depth, rows, nmod), F32),
        compiler_params=_cparams(("arbitrary", "arbitrary")),
        name="ada_modulation",
    )(c_pad, ada_w, ada_b.reshape(depth, 1, nmod))
    return out[:, :b].reshape(depth, b, N_MOD, d)


def _ffn_kernel(x_ref, mod_ref, g_ref, w13_ref, w2_ref, *rest, mod_row, final_norm):
    if final_norm:
        fg_ref, o_ref, acc_ref = rest
    else:
        o_ref, acc_ref = rest
    x = x_ref[...]
    shift = mod_ref[mod_row:mod_row + 1, :]
    scale = mod_ref[mod_row + 1:mod_row + 2, :]
    gate = mod_ref[mod_row + 2:mod_row + 3, :]
    h = _norm_modulate(x, g_ref[...], shift, scale).astype(BF16)
    d_ff = w2_ref.shape[0]
    for j in range(d_ff // FF_CHUNK):
        cols = slice(j * FF_CHUNK, (j + 1) * FF_CHUNK)
        w1 = w13_ref[:, cols].astype(BF16)
        w3 = w13_ref[:, d_ff + j * FF_CHUNK:d_ff + (j + 1) * FF_CHUNK].astype(BF16)
        a = jnp.dot(h, w1, preferred_element_type=F32)
        u = jnp.dot(h, w3, preferred_element_type=F32)
        act = (_silu(a) * u).astype(BF16)
        part = jnp.dot(act, w2_ref[cols, :].astype(BF16), preferred_element_type=F32)
        if j == 0:
            acc_ref[...] = part
        else:
            acc_ref[...] += part
    y = x + (0.5 * gate) * acc_ref[...]
    if final_norm:
        ms = jnp.mean(y * y, axis=-1, keepdims=True)
        y = y * lax.rsqrt(ms + EPS) * fg_ref[...]
    o_ref[...] = y


def _ffn(x, mod, g, w13, w2, layer, mod_row, final_g=None):
    b, s, d = x.shape
    d_ff = w2.shape[1]
    assert d_ff % FF_CHUNK == 0 and d_ff % LANES == 0
    tm = min(TOKEN_TILE, s)
    final_norm = final_g is not None
    in_specs = [
        pl.BlockSpec((None, tm, d), lambda bi, i: (bi, i, 0)),
        pl.BlockSpec((None, N_MOD, d), lambda bi, i: (bi, 0, 0)),
        _resident((1, d), lambda bi, i: (0, 0)),
        _resident((None, d, 2 * d_ff), lambda bi, i: (layer, 0, 0)),
        _resident((None, d_ff, d), lambda bi, i: (layer, 0, 0)),
    ]
    args = [x, mod, g.reshape(1, d), w13, w2]
    if final_norm:
        in_specs.append(_resident((1, d), lambda bi, i: (0, 0)))
        args.append(final_g.reshape(1, d))
    return pl.pallas_call(
        functools.partial(_ffn_kernel, mod_row=mod_row, final_norm=final_norm),
        grid=(b, s // tm),
        in_specs=in_specs,
        out_specs=pl.BlockSpec((None, tm, d), lambda bi, i: (bi, i, 0)),
        out_shape=jax.ShapeDtypeStruct((b, s, d), F32),
        scratch_shapes=[pltpu.VMEM((tm, d), F32)],
        compiler_params=_cparams(("parallel", "parallel")),
        name="ffn_final" if final_norm else "ffn",
    )(*args)


def _inproj_kernel(x_ref, mod_ref, g_ref, w_ref, z_ref, xbc_ref, dt_ref,
                   sq_ref, sk_ref, sv_ref, dq_ref, dk_ref, dv_ref):
    x = x_ref[...]
    h = _norm_modulate(x, g_ref[...], mod_ref[3:4, :], mod_ref[4:5, :]).astype(BF16)

    def proj(off, width):
        return jnp.dot(h, w_ref[:, off:off + width], preferred_element_type=F32)

    qk_scale = HEAD_DIM ** -0.5 * LOG2E
    z_ref[...] = proj(OFF_Z, SSM_D_INNER).astype(BF16)
    xbc_ref[...] = proj(OFF_XBC, SSM_CONV_DIM).astype(BF16)
    dt_ref[...] = proj(OFF_DT, LANES)
    sq_ref[...] = (proj(OFF_SQ, SB_WIDTH) * qk_scale).astype(BF16)
    sk_ref[...] = proj(OFF_SK, SB_WIDTH).astype(BF16)
    sv_ref[...] = proj(OFF_SV, SB_WIDTH).astype(BF16)
    dq_ref[...] = (proj(OFF_DQ, DIFF_WIDTH) * qk_scale).astype(BF16)
    dk_ref[...] = proj(OFF_DK, DIFF_WIDTH).astype(BF16)
    dv_ref[...] = proj(OFF_DV, DIFF_WIDTH).astype(BF16)


def _pack_kernel(w_ref, o_ref):
    dt0 = SSM_D_INNER + SSM_CONV_DIM
    dt1 = dt0 + SSM_HEADS
    w = w_ref[...]
    rows = w.shape[0]
    o_ref[:, :dt0] = w[:, :dt0].astype(BF16)
    o_ref[:, dt0:OFF_DT] = w[:, dt1:].astype(BF16)
    o_ref[:, OFF_DT:] = jnp.concatenate(
        [w[:, dt0:dt1], jnp.zeros((rows, LANES - SSM_HEADS), F32)], axis=1).astype(BF16)


def _pack_w_in(w_in):
    depth, d, n_in = w_in.shape
    rows = 128
    return pl.pallas_call(
        _pack_kernel,
        grid=(depth, d // rows),
        in_specs=[pl.BlockSpec((None, rows, n_in), lambda l, r: (l, r, 0))],
        out_specs=pl.BlockSpec((None, rows, IN_PACKED), lambda l, r: (l, r, 0)),
        out_shape=jax.ShapeDtypeStruct((depth, d, IN_PACKED), BF16),
        compiler_params=_cparams(("arbitrary", "arbitrary")),
        name="pack_w_in",
    )(w_in)


def _inproj(x, mod, g, w, layer):
    b, s, d = x.shape
    tm = min(TOKEN_TILE, s)
    widths = (SSM_D_INNER, SSM_CONV_DIM, LANES) + (SB_WIDTH,) * 3 + (DIFF_WIDTH,) * 3
    dtypes = (BF16, BF16, F32) + (BF16,) * 6
    return pl.pallas_call(
        _inproj_kernel,
        grid=(b, s // tm),
        in_specs=[
            pl.BlockSpec((None, tm, d), lambda bi, i: (bi, i, 0)),
            pl.BlockSpec((None, N_MOD, d), lambda bi, i: (bi, 0, 0)),
            _resident((1, d), lambda bi, i: (0, 0)),
            _resident((None, d, IN_PACKED), lambda bi, i: (layer, 0, 0)),
        ],
        out_specs=[pl.BlockSpec((None, tm, wd), lambda bi, i: (bi, i, 0)) for wd in widths],
        out_shape=[jax.ShapeDtypeStruct((b, s, wd), dt) for wd, dt in zip(widths, dtypes)],
        compiler_params=_cparams(("parallel", "parallel")),
        name="mixer_inproj",
    )(x, mod, g.reshape(1, d), w)


def _ssd_kernel(z_ref, xbc_ref, dtr_ref, cw_ref, cb_ref, dtb_ref, alog_ref, dsk_ref, ng_ref, e_ref,
                y_ref, tail_ref, state_ref):
    rows = range(z_ref.shape[0])
    convs = [_ssd_conv(xbc_ref.at[bb], cw_ref, cb_ref, tail_ref.at[bb], state_ref.at[bb]) for bb in rows]
    for bb in rows:
        _ssd_scan(convs[bb], z_ref.at[bb], dtr_ref.at[bb], dtb_ref, alog_ref, dsk_ref, ng_ref, e_ref,
                  y_ref.at[bb], state_ref.at[bb])


def _ssd_conv(xbc_ref, cw_ref, cb_ref, tail_ref, state_ref):
    L = xbc_ref.shape[0]
    pad = tail_ref.shape[0]
    c = pl.program_id(1)

    @pl.when(c == 0)
    def _():
        tail_ref[...] = jnp.zeros(tail_ref.shape, BF16)
        state_ref[...] = jnp.zeros(state_ref.shape, F32)

    u = xbc_ref[...]
    u_ext = jnp.concatenate([tail_ref[...], u], axis=0)
    tail_ref[...] = u[L - pad:, :]
    trow = lax.broadcasted_iota(jnp.int32, (L, pad + L), 0)
    tcol = lax.broadcasted_iota(jnp.int32, (L, pad + L), 1)
    shifts = jnp.concatenate(
        [jnp.where(tcol == trow + (pad - (SSM_CONV - 1) + k), 1.0, 0.0) for k in range(SSM_CONV)], axis=0)
    shifted = jnp.dot(shifts.astype(BF16), u_ext, preferred_element_type=F32)
    conv = cb_ref[...]
    for k in range(SSM_CONV):
        conv = conv + cw_ref[k:k + 1, :] * shifted[k * L:(k + 1) * L]
    return _silu(conv)


def _ssd_scan(xa, z_ref, dtr_ref, dtb_ref, alog_ref, dsk_ref, ng_ref, e_ref, y_ref, state_ref):
    L = z_ref.shape[0]
    xs = xa[:, :SSM_D_INNER]
    bmat = xa[:, SSM_D_INNER:SSM_D_INNER + SSM_GROUPS * SSM_STATE]
    cmat = xa[:, SSM_D_INNER + SSM_GROUPS * SSM_STATE:]

    dt = _softplus(dtr_ref[...] + dtb_ref[...])
    a = -jnp.exp(alog_ref[...])
    da = dt * a
    row = lax.broadcasted_iota(jnp.int32, (L, L), 0)
    col = lax.broadcasted_iota(jnp.int32, (L, L), 1)
    causal = row >= col
    a_cum = jnp.dot(causal.astype(F32), da, preferred_element_type=F32, precision=HIGHEST)
    a_cum_t = a_cum.T
    exp_a = jnp.exp(a_cum)
    decay_to_end = jnp.exp(a_cum[L - 1:L, :] - a_cum)

    stacked = jnp.concatenate([dt, exp_a, decay_to_end], axis=0)
    hi = stacked.astype(BF16)
    lo = (stacked - hi.astype(F32)).astype(BF16)
    spread = jnp.dot(jnp.concatenate([hi, lo], axis=1), e_ref[...], preferred_element_type=F32)
    dt_e = spread[0:L]
    exp_a_e = spread[L:2 * L]
    dte_e = spread[2 * L:3 * L]

    xd = xs * dt_e
    xd_b = xd.astype(BF16)
    xdd_b = (xd * dte_e).astype(BF16)
    prev_b = state_ref[...].astype(BF16)
    lane = lax.broadcasted_iota(jnp.int32, (L, LANES), 1)
    heads_per_group = SSM_HEADS // SSM_GROUPS
    group_width = heads_per_group * HEAD_DIM

    y_diag_parts = []
    y_off_parts = []
    s_new_parts = []
    for g in range(SSM_GROUPS):
        bg = bmat[:, g * SSM_STATE:(g + 1) * SSM_STATE]
        cg_b = cmat[:, g * SSM_STATE:(g + 1) * SSM_STATE].astype(BF16)
        bg_b = bg.astype(BF16)
        bg_t_b = bg.T.astype(BF16)
        gs = slice(g * group_width, (g + 1) * group_width)
        cb = lax.dot_general(cg_b, bg_b, (((1,), (1,)), ((), ())), preferred_element_type=F32)
        y_off_parts.append(jnp.dot(cg_b, prev_b[:, gs], preferred_element_type=F32))
        s_new_parts.append(jnp.dot(bg_t_b, xdd_b[:, gs], preferred_element_type=F32))
        for hp in range(heads_per_group // 2):
            ms = []
            for h in (g * heads_per_group + 2 * hp, g * heads_per_group + 2 * hp + 1):
                seg = a_cum[:, h:h + 1] - a_cum_t[h:h + 1, :]
                decay = jnp.exp(jnp.where(causal, seg, -jnp.inf))
                ms.append((cb * decay).astype(BF16))
            pair = (g * heads_per_group) // 2 + hp
            yp = jnp.dot(jnp.concatenate(ms, axis=0), xd_b[:, pair * LANES:(pair + 1) * LANES],
                         preferred_element_type=F32)
            y_diag_parts.append(jnp.where(lane < HEAD_DIM, yp[:L], yp[L:]))
    y_diag = jnp.concatenate(y_diag_parts, axis=1)
    y_off = jnp.concatenate(y_off_parts, axis=1)
    s_new = jnp.concatenate(s_new_parts, axis=1)

    y = y_diag + y_off * exp_a_e + dsk_ref[...] * xs
    state_ref[...] = state_ref[...] * exp_a_e[L - 1:L, :] + s_new

    y = y * _silu(z_ref[...].astype(F32))
    outs = []
    for g in range(SSM_GROUPS):
        yg = y[:, g * group_width:(g + 1) * group_width]
        ms = jnp.mean(yg * yg, axis=-1, keepdims=True)
        outs.append(yg * lax.rsqrt(ms + EPS))
    y_ref[...] = (jnp.concatenate(outs, axis=1) * ng_ref[...]).astype(BF16)


def _head_spread_matrix():
    e = np.zeros((2 * LANES, SSM_D_INNER), np.float32)
    for h in range(SSM_HEADS):
        e[h, h * HEAD_DIM:(h + 1) * HEAD_DIM] = 1.0
        e[LANES + h, h * HEAD_DIM:(h + 1) * HEAD_DIM] = 1.0
    return jnp.asarray(e, BF16)


def _pad_lanes(v):
    return jnp.zeros((1, LANES), F32).at[0, :v.shape[0]].set(v)


def _ssd(z, xbc, dt_raw, conv_w, conv_b, dt_bias, a_log, d_skip, norm_g):
    b, s, _ = z.shape
    L = SSM_CHUNK
    nb = SSD_STREAMS if b % SSD_STREAMS == 0 else 1
    row = lambda width: _resident((1, width), lambda bi, c: (0, 0))
    return pl.pallas_call(
        _ssd_kernel,
        grid=(b // nb, s // L),
        in_specs=[
            pl.BlockSpec((nb, L, SSM_D_INNER), lambda bi, c: (bi, c, 0)),
            pl.BlockSpec((nb, L, SSM_CONV_DIM), lambda bi, c: (bi, c, 0)),
            pl.BlockSpec((nb, L, LANES), lambda bi, c: (bi, c, 0)),
            _resident((SSM_CONV, SSM_CONV_DIM), lambda bi, c: (0, 0)),
            row(SSM_CONV_DIM), row(LANES), row(LANES), row(SSM_D_INNER), row(SSM_D_INNER),
            _resident((2 * LANES, SSM_D_INNER), lambda bi, c: (0, 0)),
        ],
        out_specs=pl.BlockSpec((nb, L, SSM_D_INNER), lambda bi, c: (bi, c, 0)),
        out_shape=jax.ShapeDtypeStruct((b, s, SSM_D_INNER), BF16),
        scratch_shapes=[pltpu.VMEM((nb, BF16_SUBLANES, SSM_CONV_DIM), BF16),
                        pltpu.VMEM((nb, SSM_STATE, SSM_D_INNER), F32)],
        compiler_params=_cparams(("parallel", "arbitrary")),
        name="ssd_mixer",
    )(z, xbc, dt_raw, conv_w.T, conv_b.reshape(1, -1), _pad_lanes(dt_bias), _pad_lanes(a_log),
      jnp.repeat(d_skip, HEAD_DIM).reshape(1, -1), norm_g.reshape(1, -1), _head_spread_matrix())


def _split_lane_halves(x2):
    xf = x2.astype(F32)
    lane = lax.broadcasted_iota(jnp.int32, xf.shape, 1)
    lo = jnp.where(lane < HEAD_DIM, xf, 0.0)
    hi = jnp.where(lane >= HEAD_DIM, xf, 0.0)
    return jnp.concatenate([lo, hi], axis=0).astype(BF16)


def _scores_t(kb, qs):
    return lax.dot_general(kb, qs, (((1,), (1,)), ((), ())), preferred_element_type=F32)


def _pv_t(vb, wt):
    return lax.dot_general(vb, wt, (((0,), (0,)), ((), ())), preferred_element_type=F32)


def _sb_kernel(q_ref, k_ref, v_ref, o_ref, z_ref, cs_ref, tot_ref, acc_ref, carry_ref):
    T = q_ref.shape[0]
    n_streams = q_ref.shape[1] // LANES
    i = pl.program_id(2)
    qs = [_split_lane_halves(q_ref[:, p * LANES:(p + 1) * LANES]) for p in range(n_streams)]

    key = lax.broadcasted_iota(jnp.int32, (T, 2 * T), 0)
    qry = lax.broadcasted_iota(jnp.int32, (T, 2 * T), 1)
    strict = key < jnp.where(qry >= T, qry - T, qry)
    ur = lax.broadcasted_iota(jnp.int32, (T, T), 0)
    uc = lax.broadcasted_iota(jnp.int32, (T, T), 1)
    neg_suffix = jnp.where(uc >= ur, -1.0, 0.0).astype(BF16)
    streams = range(n_streams)
    sign_bit = jnp.uint32(0x80000000)

    def scores(j_lo, nb):
        rows = pl.ds(pl.multiple_of(j_lo * T, T), nb * T)
        return [_scores_t(k_ref[rows, p * LANES:(p + 1) * LANES], qs[p]) for p in streams]

    def softplus2(z, diagonal):
        neg_abs = lax.bitcast_convert_type(lax.bitcast_convert_type(z, jnp.uint32) | sign_bit, F32)
        sp = jnp.maximum(z, 0.0) + jnp.log(1.0 + jnp.exp2(neg_abs)) * LOG2E
        if diagonal:
            sp = jnp.where(strict, sp, 0.0)
        return sp.astype(BF16)

    def suffix_sums(sp, nb):
        parts = [jnp.dot(neg_suffix, sp[u * T:(u + 1) * T], preferred_element_type=F32) for u in range(nb)]
        rows = [None] * nb
        total = parts[nb - 1][0:1, :]
        for u in reversed(range(nb - 1)):
            rows[u] = total
            total = total + parts[u][0:1, :]
        rows[nb - 1] = total
        return parts, (rows[0] if nb == 1 else jnp.concatenate(rows, axis=0))

    def weights(z, parts, after_rows, carry, diagonal):
        nb = len(parts)
        ws = []
        for u in range(nb):
            later = carry if u == nb - 1 else carry + after_rows[u:u + 1, :]
            w = jnp.exp2(z[u * T:(u + 1) * T] + (parts[u] + later))
            if diagonal:
                w = jnp.where(strict, w, 0.0)
            ws.append(w.astype(BF16))
        return ws[0] if nb == 1 else jnp.concatenate(ws, axis=0)

    def accumulate(j_lo, nb, ws, tots):
        rows = pl.ds(pl.multiple_of(j_lo * T, T), nb * T)
        for p in streams:
            acc_ref[p] += _pv_t(v_ref[rows, p * LANES:(p + 1) * LANES], ws[p])
            carry_ref[p] += tots[p]

    def direct_step(j_lo, nb, diagonal=False):
        zs = scores(j_lo, nb)
        cts = [suffix_sums(softplus2(z, diagonal), nb) for z in zs]
        ws = [weights(zs[p], cts[p][0], cts[p][1], carry_ref[p], diagonal) for p in streams]
        accumulate(j_lo, nb, ws, [ct[1][nb - 1:nb, :] for ct in cts])

    def finish_produce(slot, zs, with_diagonal=False):
        for p in streams:
            z_ref[slot, p] = zs[p]
        for p in streams:
            if with_diagonal:
                sp = jnp.concatenate([softplus2(z_ref[slot, p, 0:T, :], False),
                                      softplus2(z_ref[slot, p, T:2 * T, :], True)], axis=0)
            else:
                sp = softplus2(z_ref[slot, p], False)
            parts, rows = suffix_sums(sp, 2)
            if with_diagonal:
                parts[1] = jnp.where(strict, parts[1], -jnp.inf)
            cs_ref[slot, p] = jnp.concatenate(parts, axis=0)
            tot_ref[slot, p] = rows

    for p in streams:
        acc_ref[p] = jnp.zeros((LANES, 2 * T), F32)
        carry_ref[p] = jnp.zeros((1, 2 * T), F32)
    n_double = (i + 1) // 2
    pair_lo = lambda t: i - 1 - 2 * t

    def consume(t, slot, produce_next):
        ws = [weights(z_ref[slot, p], [cs_ref[slot, p, 0:T, :], cs_ref[slot, p, T:2 * T, :]],
                      tot_ref[slot, p], carry_ref[p], False) for p in streams]
        tots = [tot_ref[slot, p, 1:2, :] for p in streams]
        if produce_next:
            finish_produce(1 - slot, scores(pair_lo(t + 1), 2))
        accumulate(pair_lo(t), 2, ws, tots)

    @pl.when(n_double > 0)
    def _():
        finish_produce(0, scores(pair_lo(0), 2), with_diagonal=True)

    n_twice = jnp.maximum(n_double - 1, 0) // 2

    def body(u, _):
        consume(2 * u, 0, True)
        consume(2 * u + 1, 1, True)
        return 0

    lax.fori_loop(0, n_twice, body, 0)
    left = n_double - 2 * n_twice

    @pl.when(left == 2)
    def _():
        consume(2 * n_twice, 0, True)
        consume(2 * n_twice + 1, 1, False)

    @pl.when(left == 1)
    def _():
        consume(2 * n_twice, 0, False)

    @pl.when(jnp.logical_and(i + 1 - 2 * n_double > 0, i > 0))
    def _():
        direct_step(0, 1)

    @pl.when(i == 0)
    def _():
        direct_step(0, 1, diagonal=True)

    for p in range(n_streams):
        acc = acc_ref[p]
        pair_t = jnp.concatenate([acc[:HEAD_DIM, :T], acc[HEAD_DIM:, T:]], axis=0)
        o_ref[:, p * LANES:(p + 1) * LANES] = pair_t.T.astype(BF16)


def _stick_breaking(q, k, v):
    b, s, width = q.shape
    T = min(ATT_BLOCK, s)
    gw = SB_STREAMS * LANES
    return pl.pallas_call(
        _sb_kernel,
        grid=(b, width // gw, s // T),
        in_specs=[
            pl.BlockSpec((None, T, gw), lambda bi, g, i: (bi, i, g)),
            pl.BlockSpec((None, s, gw), lambda bi, g, i: (bi, 0, g)),
            pl.BlockSpec((None, s, gw), lambda bi, g, i: (bi, 0, g)),
        ],
        out_specs=pl.BlockSpec((None, T, gw), lambda bi, g, i: (bi, i, g)),
        out_shape=jax.ShapeDtypeStruct((b, s, width), BF16),
        scratch_shapes=[
            pltpu.VMEM((2, SB_STREAMS, 2 * T, 2 * T), F32),
            pltpu.VMEM((2, SB_STREAMS, 2 * T, 2 * T), F32),
            pltpu.VMEM((2, SB_STREAMS, 2, 2 * T), F32),
            pltpu.VMEM((SB_STREAMS, LANES, 2 * T), F32),
            pltpu.VMEM((SB_STREAMS, 1, 2 * T), F32),
        ],
        compiler_params=_cparams(("parallel", "parallel", "arbitrary")),
        name="stick_breaking_attention",
    )(q, k, v)


def _bucket_starts():
    max_exact = N_REL_BUCKETS // 2
    dist = np.arange(0, 4 * REL_MAX_DIST, dtype=np.float64)
    ratio = np.log(np.maximum(dist, max_exact) / max_exact) / math.log(REL_MAX_DIST / max_exact)
    scaled = ratio * (N_REL_BUCKETS - max_exact)
    frac = np.abs(scaled - np.round(scaled))
    interior = (dist > max_exact) & (dist < REL_MAX_DIST)
    assert frac[interior].min() > 1e-3, "a bucket edge sits on an integer distance"
    large = np.minimum(max_exact + np.floor(scaled + 1e-9).astype(np.int64), N_REL_BUCKETS - 1)
    bucket = np.where(dist < max_exact, dist.astype(np.int64), large)
    assert np.all(np.diff(bucket) >= 0) and bucket[-1] == N_REL_BUCKETS - 1
    return [int(np.argmax(bucket >= bkt)) for bkt in range(N_REL_BUCKETS)]


def _bias_table_kernel(rel_ref, o_ref, *, starts):
    h = pl.program_id(0)
    T = o_ref.shape[-1]
    key = lax.broadcasted_iota(jnp.int32, (T, T), 0)
    qry = lax.broadcasted_iota(jnp.int32, (T, T), 1)
    far = rel_ref[N_REL_BUCKETS - 1, h]
    for o_blk in range(o_ref.shape[0]):
        dist = qry - key + o_blk * T
        bias = jnp.full((T, T), rel_ref[0, h], F32)
        for bkt in range(1, N_REL_BUCKETS):
            bias = jnp.where(dist >= starts[bkt], rel_ref[bkt, h], bias)
        o_ref[o_blk] = jnp.where(dist >= 0, (bias - far) * LOG2E, -jnp.inf)


def _bias_tables(rel_bias, T):
    starts = _bucket_starts()
    assert T + 1 >= starts[-1]
    return pl.pallas_call(
        functools.partial(_bias_table_kernel, starts=starts),
        grid=(DIFF_HEADS,),
        in_specs=[pl.BlockSpec(memory_space=pltpu.SMEM)],
        out_specs=pl.BlockSpec((None, 2, T, T), lambda h: (h, 0, 0, 0)),
        out_shape=jax.ShapeDtypeStruct((DIFF_HEADS, 2, T, T), F32),
        compiler_params=_cparams(("arbitrary",)),
        name="t5_bias_tables",
    )(rel_bias)


def _diff_kernel(q_ref, k_ref, v_ref, tb_ref, lq1_ref, lk1_ref, lq2_ref, lk2_ref, sg_ref, o_ref,
                 s_ref, m_ref, l_ref, acc_ref, *, lambda_init):
    T = q_ref.shape[0]
    n_streams = q_ref.shape[1] // LANES
    i = pl.program_id(2)
    qs = [_split_lane_halves(q_ref[:, h * LANES:(h + 1) * LANES]) for h in range(n_streams)]

    heads = range(n_streams)

    def scores(j_lo, nb):
        rows = pl.ds(pl.multiple_of(j_lo * T, T), nb * T)
        return [_scores_t(k_ref[rows, h * LANES:(h + 1) * LANES], qs[h]) for h in heads]

    def softmax_part(ss, table_rows):
        out = []
        for h in heads:
            read = ss[h] if callable(ss[h]) else (lambda v=ss[h]: v)
            if table_rows is not None:
                biases = [tb_ref[h, o] for o in table_rows]
                bias = biases[0] if len(biases) == 1 else jnp.concatenate(biases, axis=0)
                s = read() + jnp.concatenate([bias, bias], axis=1)
                read = lambda v=s: v
            m_old = m_ref[h]
            m_new = jnp.maximum(m_old, jnp.max(read(), axis=0, keepdims=True))
            alpha = jnp.exp2(m_old - m_new)
            p = jnp.exp2(read() - m_new)
            m_ref[h] = m_new
            l_ref[h] = alpha * l_ref[h] + jnp.sum(p, axis=0, keepdims=True)
            out.append((alpha, p.astype(BF16)))
        return out

    def accumulate(j_lo, nb, aps):
        rows = pl.ds(pl.multiple_of(j_lo * T, T), nb * T)
        for h in heads:
            alpha, p = aps[h]
            acc_ref[h] = alpha * acc_ref[h] + _pv_t(v_ref[rows, h * LANES:(h + 1) * LANES], p)

    for h in heads:
        m_ref[h] = jnp.full((1, 2 * T), -jnp.inf, F32)
        l_ref[h] = jnp.zeros((1, 2 * T), F32)
        acc_ref[h] = jnp.zeros((DIFF_V_DIM, 2 * T), F32)
    n_far = jnp.maximum(i - 1, 0)
    n_double = n_far // 2
    first = n_far - 2 * n_double
    pair_lo = lambda t: first + 2 * t

    def produce(t, slot):
        for h, s in enumerate(scores(pair_lo(t), 2)):
            s_ref[slot, h] = s

    def consume(t, slot, produce_next, table_rows=None):
        aps = softmax_part([lambda h=h: s_ref[slot, h] for h in heads], table_rows)
        if produce_next:
            produce(t + 1, 1 - slot)
        accumulate(pair_lo(t), 2, aps)

    @pl.when(first > 0)
    def _():
        accumulate(0, 1, softmax_part(scores(0, 1), None))

    @pl.when(i > 0)
    def _():
        produce(0, 0)

    n_twice = n_double // 2

    def body(u, _):
        consume(2 * u, 0, True)
        consume(2 * u + 1, 1, True)
        return 0

    lax.fori_loop(0, n_twice, body, 0)
    left = n_double - 2 * n_twice

    @pl.when(jnp.logical_and(i > 0, left == 1))
    def _():
        consume(2 * n_twice, 0, True)
        consume(2 * n_twice + 1, 1, False, [1, 0])

    @pl.when(jnp.logical_and(i > 0, left == 0))
    def _():
        consume(2 * n_twice, 0, False, [1, 0])

    @pl.when(i == 0)
    def _():
        accumulate(0, 1, softmax_part(scores(0, 1), [0]))

    lam = (jnp.exp(jnp.sum(lq1_ref[...] * lk1_ref[...], axis=-1, keepdims=True))
           - jnp.exp(jnp.sum(lq2_ref[...] * lk2_ref[...], axis=-1, keepdims=True)) + lambda_init)
    for h in range(n_streams):
        l, acc = l_ref[h], acc_ref[h]
        o_t = acc[:, :T] / l[:, :T] - lam * (acc[:, T:] / l[:, T:])
        o = o_t.T
        ms = jnp.mean(o * o, axis=-1, keepdims=True)
        o = o * lax.rsqrt(ms + EPS) * sg_ref[...] * (1.0 - lambda_init)
        o_ref[:, h * LANES:(h + 1) * LANES] = o.astype(BF16)


def _diff_attention(q, k, v, tables, lq1, lk1, lq2, lk2, subln_g, lambda_init):
    b, s, width = q.shape
    T = tables.shape[-1]
    vec = lambda n: _resident((1, n), lambda bi, g, i: (0, 0))
    gw = DIFF_STREAMS * LANES
    return pl.pallas_call(
        functools.partial(_diff_kernel, lambda_init=lambda_init),
        grid=(b, width // gw, s // T),
        in_specs=[
            pl.BlockSpec((None, T, gw), lambda bi, g, i: (bi, i, g)),
            pl.BlockSpec((None, s, gw), lambda bi, g, i: (bi, 0, g)),
            pl.BlockSpec((None, s, gw), lambda bi, g, i: (bi, 0, g)),
            pl.BlockSpec((DIFF_STREAMS,) + tables.shape[1:], lambda bi, g, i: (g, 0, 0, 0)),
            vec(HEAD_DIM), vec(HEAD_DIM), vec(HEAD_DIM), vec(HEAD_DIM), vec(DIFF_V_DIM),
        ],
        out_specs=pl.BlockSpec((None, T, gw), lambda bi, g, i: (bi, i, g)),
        out_shape=jax.ShapeDtypeStruct((b, s, width), BF16),
        scratch_shapes=[
            pltpu.VMEM((2, DIFF_STREAMS, 2 * T, 2 * T), F32),
            pltpu.VMEM((DIFF_STREAMS, 1, 2 * T), F32),
            pltpu.VMEM((DIFF_STREAMS, 1, 2 * T), F32),
            pltpu.VMEM((DIFF_STREAMS, DIFF_V_DIM, 2 * T), F32),
        ],
        compiler_params=_cparams(("parallel", "parallel", "arbitrary")),
        name="differential_attention",
    )(q, k, v, tables, lq1.reshape(1, -1), lk1.reshape(1, -1), lq2.reshape(1, -1), lk2.reshape(1, -1),
      subln_g.reshape(1, -1))


def _outproj_kernel(x_ref, mod_ref, ys_ref, yb_ref, yd_ref, w_ref, o_ref):
    o1 = SSM_D_INNER
    o2 = o1 + SB_WIDTH
    y = jnp.dot(ys_ref[...], w_ref[0:o1, :].astype(BF16), preferred_element_type=F32)
    y = y + jnp.dot(yb_ref[...], w_ref[o1:o2, :].astype(BF16), preferred_element_type=F32)
    y = y + jnp.dot(yd_ref[...], w_ref[o2:, :].astype(BF16), preferred_element_type=F32)
    o_ref[...] = x_ref[...] + mod_ref[5:6, :] * y


def _outproj(x, mod, y_ssm, y_sb, y_diff, w_out, layer):
    b, s, d = x.shape
    tm = min(TOKEN_TILE, s)
    tok = lambda width: pl.BlockSpec((None, tm, width), lambda bi, i: (bi, i, 0))
    return pl.pallas_call(
        _outproj_kernel,
        grid=(b, s // tm),
        in_specs=[
            tok(d),
            pl.BlockSpec((None, N_MOD, d), lambda bi, i: (bi, 0, 0)),
            tok(SSM_D_INNER), tok(SB_WIDTH), tok(DIFF_WIDTH),
            _resident((None,) + w_out.shape[1:], lambda bi, i: (layer, 0, 0)),
        ],
        out_specs=tok(d),
        out_shape=jax.ShapeDtypeStruct((b, s, d), F32),
        compiler_params=_cparams(("parallel", "parallel")),
        name="mixer_outproj",
    )(x, mod, y_ssm, y_sb, y_diff, w_out)


def kernel(x, c, ada_w, ada_b, ffn1_norm, ffn1_w13, ffn1_w2, mix_norm, w_in, ssm_conv_w, ssm_conv_b, ssm_dt_bias, ssm_a_log, ssm_d, ssm_norm, diff_lambda_q1, diff_lambda_k1, diff_lambda_q2, diff_lambda_k2, diff_subln, rel_bias, w_out, ffn2_norm, ffn2_w13, ffn2_w2, final_norm):
    depth = ada_w.shape[0]
    s = x.shape[1]
    mods = _ada_modulation(c, ada_w, ada_b)
    tables = _bias_tables(rel_bias, min(ATT_BLOCK, s))
    w_in_packed = _pack_w_in(w_in)
    for l in range(depth):
        mod = mods[l]
        x = _ffn(x, mod, ffn1_norm[l], ffn1_w13, ffn1_w2, l, mod_row=0)
        lambda_init = 0.8 - 0.6 * math.exp(-0.3 * l)
        z, xbc, dt_raw, sq, sk, sv, dq, dk, dv = _inproj(x, mod, mix_norm[l], w_in_packed, l)
        y_ssm = _ssd(z, xbc, dt_raw, ssm_conv_w[l], ssm_conv_b[l], ssm_dt_bias[l], ssm_a_log[l], ssm_d[l],
                     ssm_norm[l])
        y_sb = _stick_breaking(sq, sk, sv)
        y_diff = _diff_attention(dq, dk, dv, tables, diff_lambda_q1[l], diff_lambda_k1[l], diff_lambda_q2[l],
                                 diff_lambda_k2[l], diff_subln[l], lambda_init)
        x = _outproj(x, mod, y_ssm, y_sb, y_diff, w_out, l)
        x = _ffn(x, mod, ffn2_norm[l], ffn2_w13, ffn2_w2, l, mod_row=6,
                 final_g=final_norm if l == depth - 1 else None)
    return x
```

```python
import functools
import math

import numpy as np
import jax
import jax.numpy as jnp
from jax import lax
from jax.experimental import pallas as pl
from jax.experimental.pallas import tpu as pltpu

F32 = jnp.float32
BF16 = jnp.bfloat16
HIGHEST = lax.Precision.HIGHEST
LOG2E = math.log2(math.e)

HEAD_DIM = 64
SSM_HEADS = 16
SSM_GROUPS = 2
SSM_STATE = 128
SSM_CONV = 4
SSM_CHUNK = 128
SSM_D_INNER = SSM_HEADS * HEAD_DIM
SSM_CONV_DIM = SSM_D_INNER + 2 * SSM_GROUPS * SSM_STATE
SB_HEADS = 8
SB_WIDTH = SB_HEADS * HEAD_DIM
DIFF_HEADS = 4
DIFF_V_DIM = 2 * HEAD_DIM
DIFF_WIDTH = DIFF_HEADS * DIFF_V_DIM
N_MOD = 9
N_REL_BUCKETS = 32
REL_MAX_DIST = 128
EPS = 1e-6

LANES = 128
BF16_SUBLANES = 16
VMEM_LIMIT = 56 * 1024 * 1024

TOKEN_TILE = 512
PROJ_TILE = 1024
FF_CHUNK = 256
ATT_BLOCK = 256

OFF_Z = 0
OFF_XBC = OFF_Z + SSM_D_INNER
OFF_SQ = OFF_XBC + SSM_CONV_DIM
OFF_SK = OFF_SQ + SB_WIDTH
OFF_SV = OFF_SK + SB_WIDTH
OFF_DQ = OFF_SV + SB_WIDTH
OFF_DK = OFF_DQ + DIFF_WIDTH
OFF_DV = OFF_DK + DIFF_WIDTH
OFF_DT = OFF_DV + DIFF_WIDTH
IN_PACKED = OFF_DT + LANES


def _cparams(semantics):
    return pltpu.CompilerParams(dimension_semantics=semantics, vmem_limit_bytes=VMEM_LIMIT)


SSD_STREAMS = 4
SB_STREAMS = 4
DIFF_STREAMS = 4


def _resident(block_shape, index_map):
    return pl.BlockSpec(block_shape, index_map, pipeline_mode=pl.Buffered(1))


def _silu(x):
    hx = 0.5 * x
    return hx + hx * jnp.tanh(hx)


def _softplus(x):
    return jnp.maximum(x, 0.0) + jnp.log1p(jnp.exp(-jnp.abs(x)))


def _norm_modulate(x, g, shift, scale):
    ms = jnp.mean(x * x, axis=-1, keepdims=True)
    y = x * lax.rsqrt(ms + EPS) * g
    return y * (1.0 + scale) + shift


def _ada_kernel(c_ref, w_ref, b_ref, o_ref):
    cond = _silu(c_ref[...])
    w = w_ref[...]
    c_hi = cond.astype(BF16)
    c_lo = (cond - c_hi.astype(F32)).astype(BF16)
    w_hi = w.astype(BF16)
    w_lo = (w - w_hi.astype(F32)).astype(BF16)
    acc = jnp.dot(c_hi, w_hi, preferred_element_type=F32)
    acc = acc + jnp.dot(c_lo, w_hi, preferred_element_type=F32)
    acc = acc + jnp.dot(c_hi, w_lo, preferred_element_type=F32)
    o_ref[...] = acc + b_ref[...]


def _ada_modulation(c, ada_w, ada_b):
    depth, d, nmod = ada_w.shape
    b = c.shape[0]
    rows = 8 * pl.cdiv(b, 8)
    c_pad = jnp.zeros((rows, d), F32).at[:b].set(c)
    out = pl.pallas_call(
        _ada_kernel,
        grid=(depth, nmod // d),
        in_specs=[
            pl.BlockSpec((rows, d), lambda l, j: (0, 0)),
            pl.BlockSpec((None, d, d), lambda l, j: (l, 0, j)),
            pl.BlockSpec((None, 1, d), lambda l, j: (l, 0, j)),
        ],
        out_specs=pl.BlockSpec((None, rows, d), lambda l, j: (l, 0, j)),
        out_shape=jax.ShapeDtypeStruct((depth, rows, nmod), F32),
        compiler_params=_cparams(("arbitrary", "arbitrary")),
        name="ada_modulation",
    )(c_pad, ada_w, ada_b.reshape(depth, 1, nmod))
    return out[:, :b].reshape(depth, b, N_MOD, d)


def _ffn_kernel(x_ref, mod_ref, g_ref, w13_ref, w2_ref, *rest, mod_row, final_norm):
    if final_norm:
        fg_ref, o_ref, acc_ref = rest
    else:
        o_ref, acc_ref = rest
    x = x_ref[...]
    shift = mod_ref[mod_row:mod_row + 1, :]
    scale = mod_ref[mod_row + 1:mod_row + 2, :]
    gate = mod_ref[mod_row + 2:mod_row + 3, :]
    h = _norm_modulate(x, g_ref[...], shift, scale).astype(BF16)
    d_ff = w2_ref.shape[0]
    for j in range(d_ff // FF_CHUNK):
        cols = slice(j * FF_CHUNK, (j + 1) * FF_CHUNK)
        w1 = w13_ref[:, cols].astype(BF16)
        w3 = w13_ref[:, d_ff + j * FF_CHUNK:d_ff + (j + 1) * FF_CHUNK].astype(BF16)
        a = jnp.dot(h, w1, preferred_element_type=F32)
        u = jnp.dot(h, w3, preferred_element_type=F32)
        act = (_silu(a) * u).astype(BF16)
        part = jnp.dot(act, w2_ref[cols, :].astype(BF16), preferred_element_type=F32)
        if j == 0:
            acc_ref[...] = part
        else:
            acc_ref[...] += part
    y = x + (0.5 * gate) * acc_ref[...]
    if final_norm:
        ms = jnp.mean(y * y, axis=-1, keepdims=True)
        y = y * lax.rsqrt(ms + EPS) * fg_ref[...]
    o_ref[...] = y


def _ffn(x, mod, g, w13, w2, layer, mod_row, final_g=None):
    b, s, d = x.shape
    d_ff = w2.shape[1]
    assert d_ff % FF_CHUNK == 0 and d_ff % LANES == 0
    tm = min(TOKEN_TILE, s)
    final_norm = final_g is not None
    in_specs = [
        pl.BlockSpec((None, tm, d), lambda bi, i: (bi, i, 0)),
        pl.BlockSpec((None, N_MOD, d), lambda bi, i: (bi, 0, 0)),
        _resident((1, d), lambda bi, i: (0, 0)),
        _resident((None, d, 2 * d_ff), lambda bi, i: (layer, 0, 0)),
        _resident((None, d_ff, d), lambda bi, i: (layer, 0, 0)),
    ]
    args = [x, mod, g.reshape(1, d), w13, w2]
    if final_norm:
        in_specs.append(_resident((1, d), lambda bi, i: (0, 0)))
        args.append(final_g.reshape(1, d))
    return pl.pallas_call(
        functools.partial(_ffn_kernel, mod_row=mod_row, final_norm=final_norm),
        grid=(b, s // tm),
        in_specs=in_specs,
        out_specs=pl.BlockSpec((None, tm, d), lambda bi, i: (bi, i, 0)),
        out_shape=jax.ShapeDtypeStruct((b, s, d), F32),
        scratch_shapes=[pltpu.VMEM((tm, d), F32)],
        compiler_params=_cparams(("parallel", "parallel")),
        name="ffn_final" if final_norm else "ffn",
    )(*args)


def _inproj_kernel(x_ref, mod_ref, g_ref, w_ref, z_ref, xbc_ref, dt_ref,
                   sq_ref, sk_ref, sv_ref, dq_ref, dk_ref, dv_ref):
    x = x_ref[...]
    h = _norm_modulate(x, g_ref[...], mod_ref[3:4, :], mod_ref[4:5, :]).astype(BF16)

    def proj(off, width):
        return jnp.dot(h, w_ref[:, off:off + width], preferred_element_type=F32)

    qk_scale = HEAD_DIM ** -0.5 * LOG2E
    z_ref[...] = proj(OFF_Z, SSM_D_INNER).astype(BF16)
    xbc_ref[...] = proj(OFF_XBC, SSM_CONV_DIM).astype(BF16)
    dt_ref[...] = proj(OFF_DT, LANES)
    sq_ref[...] = (proj(OFF_SQ, SB_WIDTH) * qk_scale).astype(BF16)
    sk_ref[...] = proj(OFF_SK, SB_WIDTH).astype(BF16)
    sv_ref[...] = proj(OFF_SV, SB_WIDTH).astype(BF16)
    dq_ref[...] = (proj(OFF_DQ, DIFF_WIDTH) * qk_scale).astype(BF16)
    dk_ref[...] = proj(OFF_DK, DIFF_WIDTH).astype(BF16)
    dv_ref[...] = proj(OFF_DV, DIFF_WIDTH).astype(BF16)


def _pack_kernel(w_ref, o_ref):
    dt0 = SSM_D_INNER + SSM_CONV_DIM
    dt1 = dt0 + SSM_HEADS
    w = w_ref[...]
    rows = w.shape[0]
    o_ref[:, :dt0] = w[:, :dt0].astype(BF16)
    o_ref[:, dt0:OFF_DT] = w[:, dt1:].astype(BF16)
    o_ref[:, OFF_DT:] = jnp.concatenate(
        [w[:, dt0:dt1], jnp.zeros((rows, LANES - SSM_HEADS), F32)], axis=1).astype(BF16)


def _pack_w_in(w_in):
    depth, d, n_in = w_in.shape
    rows = 128
    return pl.pallas_call(
        _pack_kernel,
        grid=(depth, d // rows),
        in_specs=[pl.BlockSpec((None, rows, n_in), lambda l, r: (l, r, 0))],
        out_specs=pl.BlockSpec((None, rows, IN_PACKED), lambda l, r: (l, r, 0)),
        out_shape=jax.ShapeDtypeStruct((depth, d, IN_PACKED), BF16),
        compiler_params=_cparams(("arbitrary", "arbitrary")),
        name="pack_w_in",
    )(w_in)


def _inproj(x, mod, g, w, layer):
    b, s, d = x.shape
    tm = min(PROJ_TILE, s)
    widths = (SSM_D_INNER, SSM_CONV_DIM, LANES) + (SB_WIDTH,) * 3 + (DIFF_WIDTH,) * 3
    dtypes = (BF16, BF16, F32) + (BF16,) * 6
    return pl.pallas_call(
        _inproj_kernel,
        grid=(b, s // tm),
        in_specs=[
            pl.BlockSpec((None, tm, d), lambda bi, i: (bi, i, 0)),
            pl.BlockSpec((None, N_MOD, d), lambda bi, i: (bi, 0, 0)),
            _resident((1, d), lambda bi, i: (0, 0)),
            _resident((None, d, IN_PACKED), lambda bi, i: (layer, 0, 0)),
        ],
        out_specs=[pl.BlockSpec((None, tm, wd), lambda bi, i: (bi, i, 0)) for wd in widths],
        out_shape=[jax.ShapeDtypeStruct((b, s, wd), dt) for wd, dt in zip(widths, dtypes)],
        compiler_params=_cparams(("parallel", "parallel")),
        name="mixer_inproj",
    )(x, mod, g.reshape(1, d), w)


def _ssd_kernel(z_ref, xbc_ref, dtr_ref, cw_ref, cb_ref, dtb_ref, alog_ref, dsk_ref, ng_ref, e_ref,
                y_ref, tail_ref, state_ref):
    rows = range(z_ref.shape[0])
    convs = [_ssd_conv(xbc_ref.at[bb], cw_ref, cb_ref, tail_ref.at[bb], state_ref.at[bb]) for bb in rows]
    for bb in rows:
        _ssd_scan(convs[bb], z_ref.at[bb], dtr_ref.at[bb], dtb_ref, alog_ref, dsk_ref, ng_ref, e_ref,
                  y_ref.at[bb], state_ref.at[bb])


def _ssd_conv(xbc_ref, cw_ref, cb_ref, tail_ref, state_ref):
    L = xbc_ref.shape[0]
    pad = tail_ref.shape[0]
    c = pl.program_id(1)

    @pl.when(c == 0)
    def _():
        tail_ref[...] = jnp.zeros(tail_ref.shape, BF16)
        state_ref[...] = jnp.zeros(state_ref.shape, F32)

    u = xbc_ref[...]
    u_ext = jnp.concatenate([tail_ref[...], u], axis=0)
    tail_ref[...] = u[L - pad:, :]
    trow = lax.broadcasted_iota(jnp.int32, (L, pad + L), 0)
    tcol = lax.broadcasted_iota(jnp.int32, (L, pad + L), 1)
    shifts = jnp.concatenate(
        [jnp.where(tcol == trow + (pad - (SSM_CONV - 1) + k), 1.0, 0.0) for k in range(SSM_CONV)], axis=0)
    shifted = jnp.dot(shifts.astype(BF16), u_ext, preferred_element_type=F32)
    conv = cb_ref[...]
    for k in range(SSM_CONV):
        conv = conv + cw_ref[k:k + 1, :] * shifted[k * L:(k + 1) * L]
    return _silu(conv)


def _ssd_scan(xa, z_ref, dtr_ref, dtb_ref, alog_ref, dsk_ref, ng_ref, e_ref, y_ref, state_ref):
    L = z_ref.shape[0]
    xs = xa[:, :SSM_D_INNER]
    bmat = xa[:, SSM_D_INNER:SSM_D_INNER + SSM_GROUPS * SSM_STATE]
    cmat = xa[:, SSM_D_INNER + SSM_GROUPS * SSM_STATE:]

    dt = _softplus(dtr_ref[...] + dtb_ref[...])
    a = -jnp.exp(alog_ref[...])
    da = dt * a
    row = lax.broadcasted_iota(jnp.int32, (L, L), 0)
    col = lax.broadcasted_iota(jnp.int32, (L, L), 1)
    causal = row >= col
    a_cum = jnp.dot(causal.astype(F32), da, preferred_element_type=F32, precision=HIGHEST)
    a_cum_t = a_cum.T
    exp_a = jnp.exp(a_cum)
    decay_to_end = jnp.exp(a_cum[L - 1:L, :] - a_cum)

    stacked = jnp.concatenate([dt, exp_a, decay_to_end], axis=0)
    hi = stacked.astype(BF16)
    lo = (stacked - hi.astype(F32)).astype(BF16)
    spread = jnp.dot(jnp.concatenate([hi, lo], axis=1), e_ref[...], preferred_element_type=F32)
    dt_e = spread[0:L]
    exp_a_e = spread[L:2 * L]
    dte_e = spread[2 * L:3 * L]

    xd = xs * dt_e
    xd_b = xd.astype(BF16)
    xdd_b = (xd * dte_e).astype(BF16)
    prev_b = state_ref[...].astype(BF16)
    lane = lax.broadcasted_iota(jnp.int32, (L, LANES), 1)
    heads_per_group = SSM_HEADS // SSM_GROUPS
    group_width = heads_per_group * HEAD_DIM

    y_diag_parts = []
    y_off_parts = []
    s_new_parts = []
    for g in range(SSM_GROUPS):
        bg = bmat[:, g * SSM_STATE:(g + 1) * SSM_STATE]
        cg_b = cmat[:, g * SSM_STATE:(g + 1) * SSM_STATE].astype(BF16)
        bg_b = bg.astype(BF16)
        bg_t_b = bg.T.astype(BF16)
        gs = slice(g * group_width, (g + 1) * group_width)
        cb = lax.dot_general(cg_b, bg_b, (((1,), (1,)), ((), ())), preferred_element_type=F32)
        y_off_parts.append(jnp.dot(cg_b, prev_b[:, gs], preferred_element_type=F32))
        s_new_parts.append(jnp.dot(bg_t_b, xdd_b[:, gs], preferred_element_type=F32))
        for hp in range(heads_per_group // 2):
            ms = []
            for h in (g * heads_per_group + 2 * hp, g * heads_per_group + 2 * hp + 1):
                seg = a_cum[:, h:h + 1] - a_cum_t[h:h + 1, :]
                decay = jnp.exp(jnp.where(causal, seg, -jnp.inf))
                ms.append((cb * decay).astype(BF16))
            pair = (g * heads_per_group) // 2 + hp
            yp = jnp.dot(jnp.concatenate(ms, axis=0), xd_b[:, pair * LANES:(pair + 1) * LANES],
                         preferred_element_type=F32)
            y_diag_parts.append(jnp.where(lane < HEAD_DIM, yp[:L], yp[L:]))
    y_diag = jnp.concatenate(y_diag_parts, axis=1)
    y_off = jnp.concatenate(y_off_parts, axis=1)
    s_new = jnp.concatenate(s_new_parts, axis=1)

    y = y_diag + y_off * exp_a_e + dsk_ref[...] * xs
    state_ref[...] = state_ref[...] * exp_a_e[L - 1:L, :] + s_new

    y = y * _silu(z_ref[...].astype(F32))
    outs = []
    for g in range(SSM_GROUPS):
        yg = y[:, g * group_width:(g + 1) * group_width]
        ms = jnp.mean(yg * yg, axis=-1, keepdims=True)
        outs.append(yg * lax.rsqrt(ms + EPS))
    y_ref[...] = (jnp.concatenate(outs, axis=1) * ng_ref[...]).astype(BF16)


def _head_spread_matrix():
    e = np.zeros((2 * LANES, SSM_D_INNER), np.float32)
    for h in range(SSM_HEADS):
        e[h, h * HEAD_DIM:(h + 1) * HEAD_DIM] = 1.0
        e[LANES + h, h * HEAD_DIM:(h + 1) * HEAD_DIM] = 1.0
    return jnp.asarray(e, BF16)


def _pad_lanes(v):
    return jnp.zeros((1, LANES), F32).at[0, :v.shape[0]].set(v)


def _ssd(z, xbc, dt_raw, conv_w, conv_b, dt_bias, a_log, d_skip, norm_g):
    b, s, _ = z.shape
    L = SSM_CHUNK
    nb = SSD_STREAMS if b % SSD_STREAMS == 0 else 1
    row = lambda width: _resident((1, width), lambda bi, c: (0, 0))
    return pl.pallas_call(
        _ssd_kernel,
        grid=(b // nb, s // L),
        in_specs=[
            pl.BlockSpec((nb, L, SSM_D_INNER), lambda bi, c: (bi, c, 0)),
            pl.BlockSpec((nb, L, SSM_CONV_DIM), lambda bi, c: (bi, c, 0)),
            pl.BlockSpec((nb, L, LANES), lambda bi, c: (bi, c, 0)),
            _resident((SSM_CONV, SSM_CONV_DIM), lambda bi, c: (0, 0)),
            row(SSM_CONV_DIM), row(LANES), row(LANES), row(SSM_D_INNER), row(SSM_D_INNER),
            _resident((2 * LANES, SSM_D_INNER), lambda bi, c: (0, 0)),
        ],
        out_specs=pl.BlockSpec((nb, L, SSM_D_INNER), lambda bi, c: (bi, c, 0)),
        out_shape=jax.ShapeDtypeStruct((b, s, SSM_D_INNER), BF16),
        scratch_shapes=[pltpu.VMEM((nb, BF16_SUBLANES, SSM_CONV_DIM), BF16),
                        pltpu.VMEM((nb, SSM_STATE, SSM_D_INNER), F32)],
        compiler_params=_cparams(("parallel", "arbitrary")),
        name="ssd_mixer",
    )(z, xbc, dt_raw, conv_w.T, conv_b.reshape(1, -1), _pad_lanes(dt_bias), _pad_lanes(a_log),
      jnp.repeat(d_skip, HEAD_DIM).reshape(1, -1), norm_g.reshape(1, -1), _head_spread_matrix())


def _split_lane_halves(x2):
    xf = x2.astype(F32)
    lane = lax.broadcasted_iota(jnp.int32, xf.shape, 1)
    lo = jnp.where(lane < HEAD_DIM, xf, 0.0)
    hi = jnp.where(lane >= HEAD_DIM, xf, 0.0)
    return jnp.concatenate([lo, hi], axis=0).astype(BF16)


def _scores_t(kb, qs):
    return lax.dot_general(kb, qs, (((1,), (1,)), ((), ())), preferred_element_type=F32)


def _pv_t(vb, wt):
    return lax.dot_general(vb, wt, (((0,), (0,)), ((), ())), preferred_element_type=F32)


def _sb_kernel(q_ref, k_ref, v_ref, o_ref, z_ref, cs_ref, tot_ref, acc_ref, carry_ref):
    T = q_ref.shape[0]
    n_streams = q_ref.shape[1] // LANES
    i = pl.program_id(2)
    qs = [_split_lane_halves(q_ref[:, p * LANES:(p + 1) * LANES]) for p in range(n_streams)]

    key = lax.broadcasted_iota(jnp.int32, (T, 2 * T), 0)
    qry = lax.broadcasted_iota(jnp.int32, (T, 2 * T), 1)
    strict = key < jnp.where(qry >= T, qry - T, qry)
    ur = lax.broadcasted_iota(jnp.int32, (T, T), 0)
    uc = lax.broadcasted_iota(jnp.int32, (T, T), 1)
    neg_suffix = jnp.where(uc >= ur, -1.0, 0.0).astype(BF16)
    streams = range(n_streams)
    sign_bit = jnp.uint32(0x80000000)

    def scores(j_lo, nb):
        rows = pl.ds(pl.multiple_of(j_lo * T, T), nb * T)
        return [_scores_t(k_ref[rows, p * LANES:(p + 1) * LANES], qs[p]) for p in streams]

    def softplus2(z, diagonal):
        neg_abs = lax.bitcast_convert_type(lax.bitcast_convert_type(z, jnp.uint32) | sign_bit, F32)
        sp = jnp.maximum(z, 0.0) + jnp.log(1.0 + jnp.exp2(neg_abs)) * LOG2E
        if diagonal:
            sp = jnp.where(strict, sp, 0.0)
        return sp.astype(BF16)

    def suffix_sums(sp, nb):
        parts = [jnp.dot(neg_suffix, sp[u * T:(u + 1) * T], preferred_element_type=F32) for u in range(nb)]
        rows = [None] * nb
        total = parts[nb - 1][0:1, :]
        for u in reversed(range(nb - 1)):
            rows[u] = total
            total = total + parts[u][0:1, :]
        rows[nb - 1] = total
        return parts, (rows[0] if nb == 1 else jnp.concatenate(rows, axis=0))

    def weights(z, parts, after_rows, carry, diagonal):
        nb = len(parts)
        ws = []
        for u in range(nb):
            later = carry if u == nb - 1 else carry + after_rows[u:u + 1, :]
            w = jnp.exp2(z[u * T:(u + 1) * T] + (parts[u] + later))
            if diagonal:
                w = jnp.where(strict, w, 0.0)
            ws.append(w.astype(BF16))
        return ws[0] if nb == 1 else jnp.concatenate(ws, axis=0)

    def accumulate(j_lo, nb, ws, tots):
        rows = pl.ds(pl.multiple_of(j_lo * T, T), nb * T)
        for p in streams:
            acc_ref[p] += _pv_t(v_ref[rows, p * LANES:(p + 1) * LANES], ws[p])
            carry_ref[p] += tots[p]

    def direct_step(j_lo, nb, diagonal=False):
        zs = scores(j_lo, nb)
        cts = [suffix_sums(softplus2(z, diagonal), nb) for z in zs]
        ws = [weights(zs[p], cts[p][0], cts[p][1], carry_ref[p], diagonal) for p in streams]
        accumulate(j_lo, nb, ws, [ct[1][nb - 1:nb, :] for ct in cts])

    def finish_produce(slot, zs, with_diagonal=False):
        for p in streams:
            z_ref[slot, p] = zs[p]
        for p in streams:
            if with_diagonal:
                sp = jnp.concatenate([softplus2(z_ref[slot, p, 0:T, :], False),
                                      softplus2(z_ref[slot, p, T:2 * T, :], True)], axis=0)
            else:
                sp = softplus2(z_ref[slot, p], False)
            parts, rows = suffix_sums(sp, 2)
            if with_diagonal:
                parts[1] = jnp.where(strict, parts[1], -jnp.inf)
            cs_ref[slot, p] = jnp.concatenate(parts, axis=0)
            tot_ref[slot, p] = rows

    for p in streams:
        acc_ref[p] = jnp.zeros((LANES, 2 * T), F32)
        carry_ref[p] = jnp.zeros((1, 2 * T), F32)
    n_double = (i + 1) // 2
    pair_lo = lambda t: i - 1 - 2 * t

    def consume(t, slot, produce_next):
        ws = [weights(z_ref[slot, p], [cs_ref[slot, p, 0:T, :], cs_ref[slot, p, T:2 * T, :]],
                      tot_ref[slot, p], carry_ref[p], False) for p in streams]
        tots = [tot_ref[slot, p, 1:2, :] for p in streams]
        if produce_next:
            finish_produce(1 - slot, scores(pair_lo(t + 1), 2))
        accumulate(pair_lo(t), 2, ws, tots)

    @pl.when(n_double > 0)
    def _():
        finish_produce(0, scores(pair_lo(0), 2), with_diagonal=True)

    n_twice = jnp.maximum(n_double - 1, 0) // 2

    def body(u, _):
        consume(2 * u, 0, True)
        consume(2 * u + 1, 1, True)
        return 0

    lax.fori_loop(0, n_twice, body, 0)
    left = n_double - 2 * n_twice

    @pl.when(left == 2)
    def _():
        consume(2 * n_twice, 0, True)
        consume(2 * n_twice + 1, 1, False)

    @pl.when(left == 1)
    def _():
        consume(2 * n_twice, 0, False)

    @pl.when(jnp.logical_and(i + 1 - 2 * n_double > 0, i > 0))
    def _():
        direct_step(0, 1)

    @pl.when(i == 0)
    def _():
        direct_step(0, 1, diagonal=True)

    for p in range(n_streams):
        acc = acc_ref[p]
        pair_t = jnp.concatenate([acc[:HEAD_DIM, :T], acc[HEAD_DIM:, T:]], axis=0)
        o_ref[:, p * LANES:(p + 1) * LANES] = pair_t.T.astype(BF16)


def _stick_breaking(q, k, v):
    b, s, width = q.shape
    T = min(ATT_BLOCK, s)
    gw = SB_STREAMS * LANES
    return pl.pallas_call(
        _sb_kernel,
        grid=(b, width // gw, s // T),
        in_specs=[
            pl.BlockSpec((None, T, gw), lambda bi, g, i: (bi, i, g)),
            pl.BlockSpec((None, s, gw), lambda bi, g, i: (bi, 0, g)),
            pl.BlockSpec((None, s, gw), lambda bi, g, i: (bi, 0, g)),
        ],
        out_specs=pl.BlockSpec((None, T, gw), lambda bi, g, i: (bi, i, g)),
        out_shape=jax.ShapeDtypeStruct((b, s, width), BF16),
        scratch_shapes=[
            pltpu.VMEM((2, SB_STREAMS, 2 * T, 2 * T), F32),
            pltpu.VMEM((2, SB_STREAMS, 2 * T, 2 * T), F32),
            pltpu.VMEM((2, SB_STREAMS, 2, 2 * T), F32),
            pltpu.VMEM((SB_STREAMS, LANES, 2 * T), F32),
            pltpu.VMEM((SB_STREAMS, 1, 2 * T), F32),
        ],
        compiler_params=_cparams(("parallel", "parallel", "arbitrary")),
        name="stick_breaking_attention",
    )(q, k, v)


def _bucket_starts():
    max_exact = N_REL_BUCKETS // 2
    dist = np.arange(0, 4 * REL_MAX_DIST, dtype=np.float64)
    ratio = np.log(np.maximum(dist, max_exact) / max_exact) / math.log(REL_MAX_DIST / max_exact)
    scaled = ratio * (N_REL_BUCKETS - max_exact)
    frac = np.abs(scaled - np.round(scaled))
    interior = (dist > max_exact) & (dist < REL_MAX_DIST)
    assert frac[interior].min() > 1e-3, "a bucket edge sits on an integer distance"
    large = np.minimum(max_exact + np.floor(scaled + 1e-9).astype(np.int64), N_REL_BUCKETS - 1)
    bucket = np.where(dist < max_exact, dist.astype(np.int64), large)
    assert np.all(np.diff(bucket) >= 0) and bucket[-1] == N_REL_BUCKETS - 1
    return [int(np.argmax(bucket >= bkt)) for bkt in range(N_REL_BUCKETS)]


def _bias_table_kernel(rel_ref, o_ref, *, starts):
    h = pl.program_id(0)
    T = o_ref.shape[-1]
    key = lax.broadcasted_iota(jnp.int32, (T, T), 0)
    qry = lax.broadcasted_iota(jnp.int32, (T, T), 1)
    far = rel_ref[N_REL_BUCKETS - 1, h]
    for o_blk in range(o_ref.shape[0]):
        dist = qry - key + o_blk * T
        bias = jnp.full((T, T), rel_ref[0, h], F32)
        for bkt in range(1, N_REL_BUCKETS):
            bias = jnp.where(dist >= starts[bkt], rel_ref[bkt, h], bias)
        o_ref[o_blk] = jnp.where(dist >= 0, (bias - far) * LOG2E, -jnp.inf)


def _bias_tables(rel_bias, T):
    starts = _bucket_starts()
    assert T + 1 >= starts[-1]
    return pl.pallas_call(
        functools.partial(_bias_table_kernel, starts=starts),
        grid=(DIFF_HEADS,),
        in_specs=[pl.BlockSpec(memory_space=pltpu.SMEM)],
        out_specs=pl.BlockSpec((None, 2, T, T), lambda h: (h, 0, 0, 0)),
        out_shape=jax.ShapeDtypeStruct((DIFF_HEADS, 2, T, T), F32),
        compiler_params=_cparams(("arbitrary",)),
        name="t5_bias_tables",
    )(rel_bias)


def _diff_kernel(q_ref, k_ref, v_ref, tb_ref, lq1_ref, lk1_ref, lq2_ref, lk2_ref, sg_ref, o_ref,
                 s_ref, m_ref, l_ref, acc_ref, *, lambda_init):
    T = q_ref.shape[0]
    n_streams = q_ref.shape[1] // LANES
    i = pl.program_id(2)
    qs = [_split_lane_halves(q_ref[:, h * LANES:(h + 1) * LANES]) for h in range(n_streams)]

    heads = range(n_streams)

    def scores(j_lo, nb):
        rows = pl.ds(pl.multiple_of(j_lo * T, T), nb * T)
        return [_scores_t(k_ref[rows, h * LANES:(h + 1) * LANES], qs[h]) for h in heads]

    def softmax_part(ss, table_rows):
        out = []
        for h in heads:
            read = ss[h] if callable(ss[h]) else (lambda v=ss[h]: v)
            if table_rows is not None:
                biases = [tb_ref[h, o] for o in table_rows]
                bias = biases[0] if len(biases) == 1 else jnp.concatenate(biases, axis=0)
                s = read() + jnp.concatenate([bias, bias], axis=1)
                read = lambda v=s: v
            m_old = m_ref[h]
            m_new = jnp.maximum(m_old, jnp.max(read(), axis=0, keepdims=True))
            alpha = jnp.exp2(m_old - m_new)
            p = jnp.exp2(read() - m_new)
            m_ref[h] = m_new
            l_ref[h] = alpha * l_ref[h] + jnp.sum(p, axis=0, keepdims=True)
            out.append((alpha, p.astype(BF16)))
        return out

    def accumulate(j_lo, nb, aps):
        rows = pl.ds(pl.multiple_of(j_lo * T, T), nb * T)
        for h in heads:
            alpha, p = aps[h]
            acc_ref[h] = alpha * acc_ref[h] + _pv_t(v_ref[rows, h * LANES:(h + 1) * LANES], p)

    for h in heads:
        m_ref[h] = jnp.full((1, 2 * T), -jnp.inf, F32)
        l_ref[h] = jnp.zeros((1, 2 * T), F32)
        acc_ref[h] = jnp.zeros((DIFF_V_DIM, 2 * T), F32)
    n_far = jnp.maximum(i - 1, 0)
    n_double = n_far // 2
    first = n_far - 2 * n_double
    pair_lo = lambda t: first + 2 * t

    def produce(t, slot):
        for h, s in enumerate(scores(pair_lo(t), 2)):
            s_ref[slot, h] = s

    def consume(t, slot, produce_next, table_rows=None):
        aps = softmax_part([lambda h=h: s_ref[slot, h] for h in heads], table_rows)
        if produce_next:
            produce(t + 1, 1 - slot)
        accumulate(pair_lo(t), 2, aps)

    @pl.when(first > 0)
    def _():
        accumulate(0, 1, softmax_part(scores(0, 1), None))

    @pl.when(i > 0)
    def _():
        produce(0, 0)

    n_twice = n_double // 2

    def body(u, _):
        consume(2 * u, 0, True)
        consume(2 * u + 1, 1, True)
        return 0

    lax.fori_loop(0, n_twice, body, 0)
    left = n_double - 2 * n_twice

    @pl.when(jnp.logical_and(i > 0, left == 1))
    def _():
        consume(2 * n_twice, 0, True)
        consume(2 * n_twice + 1, 1, False, [1, 0])

    @pl.when(jnp.logical_and(i > 0, left == 0))
    def _():
        consume(2 * n_twice, 0, False, [1, 0])

    @pl.when(i == 0)
    def _():
        accumulate(0, 1, softmax_part(scores(0, 1), [0]))

    lam = (jnp.exp(jnp.sum(lq1_ref[...] * lk1_ref[...], axis=-1, keepdims=True))
           - jnp.exp(jnp.sum(lq2_ref[...] * lk2_ref[...], axis=-1, keepdims=True)) + lambda_init)
    for h in range(n_streams):
        l, acc = l_ref[h], acc_ref[h]
        o_t = acc[:, :T] / l[:, :T] - lam * (acc[:, T:] / l[:, T:])
        o = o_t.T
        ms = jnp.mean(o * o, axis=-1, keepdims=True)
        o = o * lax.rsqrt(ms + EPS) * sg_ref[...] * (1.0 - lambda_init)
        o_ref[:, h * LANES:(h + 1) * LANES] = o.astype(BF16)


def _diff_attention(q, k, v, tables, lq1, lk1, lq2, lk2, subln_g, lambda_init):
    b, s, width = q.shape
    T = tables.shape[-1]
    vec = lambda n: _resident((1, n), lambda bi, g, i: (0, 0))
    gw = DIFF_STREAMS * LANES
    return pl.pallas_call(
        functools.partial(_diff_kernel, lambda_init=lambda_init),
        grid=(b, width // gw, s // T),
        in_specs=[
            pl.BlockSpec((None, T, gw), lambda bi, g, i: (bi, i, g)),
            pl.BlockSpec((None, s, gw), lambda bi, g, i: (bi, 0, g)),
            pl.BlockSpec((None, s, gw), lambda bi, g, i: (bi, 0, g)),
            pl.BlockSpec((DIFF_STREAMS,) + tables.shape[1:], lambda bi, g, i: (g, 0, 0, 0)),
            vec(HEAD_DIM), vec(HEAD_DIM), vec(HEAD_DIM), vec(HEAD_DIM), vec(DIFF_V_DIM),
        ],
        out_specs=pl.BlockSpec((None, T, gw), lambda bi, g, i: (bi, i, g)),
        out_shape=jax.ShapeDtypeStruct((b, s, width), BF16),
        scratch_shapes=[
            pltpu.VMEM((2, DIFF_STREAMS, 2 * T, 2 * T), F32),
            pltpu.VMEM((DIFF_STREAMS, 1, 2 * T), F32),
            pltpu.VMEM((DIFF_STREAMS, 1, 2 * T), F32),
            pltpu.VMEM((DIFF_STREAMS, DIFF_V_DIM, 2 * T), F32),
        ],
        compiler_params=_cparams(("parallel", "parallel", "arbitrary")),
        name="differential_attention",
    )(q, k, v, tables, lq1.reshape(1, -1), lk1.reshape(1, -1), lq2.reshape(1, -1), lk2.reshape(1, -1),
      subln_g.reshape(1, -1))


def _outproj_kernel(x_ref, mod_ref, ys_ref, yb_ref, yd_ref, w_ref, o_ref):
    o1 = SSM_D_INNER
    o2 = o1 + SB_WIDTH
    y = jnp.dot(ys_ref[...], w_ref[0:o1, :].astype(BF16), preferred_element_type=F32)
    y = y + jnp.dot(yb_ref[...], w_ref[o1:o2, :].astype(BF16), preferred_element_type=F32)
    y = y + jnp.dot(yd_ref[...], w_ref[o2:, :].astype(BF16), preferred_element_type=F32)
    o_ref[...] = x_ref[...] + mod_ref[5:6, :] * y


def _outproj(x, mod, y_ssm, y_sb, y_diff, w_out, layer):
    b, s, d = x.shape
    tm = min(PROJ_TILE, s)
    tok = lambda width: pl.BlockSpec((None, tm, width), lambda bi, i: (bi, i, 0))
    return pl.pallas_call(
        _outproj_kernel,
        grid=(b, s // tm),
        in_specs=[
            tok(d),
            pl.BlockSpec((None, N_MOD, d), lambda bi, i: (bi, 0, 0)),
            tok(SSM_D_INNER), tok(SB_WIDTH), tok(DIFF_WIDTH),
            _resident((None,) + w_out.shape[1:], lambda bi, i: (layer, 0, 0)),
        ],
        out_specs=tok(d),
        out_shape=jax.ShapeDtypeStruct((b, s, d), F32),
        compiler_params=_cparams(("parallel", "parallel")),
        name="mixer_outproj",
    )(x, mod, y_ssm, y_sb, y_diff, w_out)


def kernel(x, c, ada_w, ada_b, ffn1_norm, ffn1_w13, ffn1_w2, mix_norm, w_in, ssm_conv_w, ssm_conv_b, ssm_dt_bias, ssm_a_log, ssm_d, ssm_norm, diff_lambda_q1, diff_lambda_k1, diff_lambda_q2, diff_lambda_k2, diff_subln, rel_bias, w_out, ffn2_norm, ffn2_w13, ffn2_w2, final_norm):
    depth = ada_w.shape[0]
    s = x.shape[1]
    mods = _ada_modulation(c, ada_w, ada_b)
    tables = _bias_tables(rel_bias, min(ATT_BLOCK, s))
    w_in_packed = _pack_w_in(w_in)
    for l in range(depth):
        mod = mods[l]
        x = _ffn(x, mod, ffn1_norm[l], ffn1_w13, ffn1_w2, l, mod_row=0)
        lambda_init = 0.8 - 0.6 * math.exp(-0.3 * l)
        z, xbc, dt_raw, sq, sk, sv, dq, dk, dv = _inproj(x, mod, mix_norm[l], w_in_packed, l)
        y_ssm = _ssd(z, xbc, dt_raw, ssm_conv_w[l], ssm_conv_b[l], ssm_dt_bias[l], ssm_a_log[l], ssm_d[l],
                     ssm_norm[l])
        y_sb = _stick_breaking(sq, sk, sv)
        y_diff = _diff_attention(dq, dk, dv, tables, diff_lambda_q1[l], diff_lambda_k1[l], diff_lambda_q2[l],
                                 diff_lambda_k2[l], diff_subln[l], lambda_init)
        x = _outproj(x, mod, y_ssm, y_sb, y_diff, w_out, l)
        x = _ffn(x, mod, ffn2_norm[l], ffn2_w13, ffn2_w2, l, mod_row=6,
                 final_g=final_norm if l == depth - 1 else None)
    return x
```

```python
import functools
import math

import numpy as np
import jax
import jax.numpy as jnp
from jax import lax
from jax.experimental import pallas as pl
from jax.experimental.pallas import tpu as pltpu

F32 = jnp.float32
BF16 = jnp.bfloat16
LOG2E = math.log2(math.e)

HEAD_DIM = 64
SSM_HEADS = 16
SSM_GROUPS = 2
SSM_STATE = 128
SSM_CONV = 4
SSM_CHUNK = 128
SSM_D_INNER = SSM_HEADS * HEAD_DIM
SSM_CONV_DIM = SSM_D_INNER + 2 * SSM_GROUPS * SSM_STATE
SB_HEADS = 8
SB_WIDTH = SB_HEADS * HEAD_DIM
DIFF_HEADS = 4
DIFF_V_DIM = 2 * HEAD_DIM
DIFF_WIDTH = DIFF_HEADS * DIFF_V_DIM
N_MOD = 9
N_REL_BUCKETS = 32
REL_MAX_DIST = 128
EPS = 1e-6

LANES = 128
BF16_SUBLANES = 16
VMEM_LIMIT = 56 * 1024 * 1024

TOKEN_TILE = 512
PROJ_TILE = 1024
FF_CHUNK = 256
ATT_BLOCK = 256

OFF_Z = 0
OFF_XBC = OFF_Z + SSM_D_INNER
OFF_SQ = OFF_XBC + SSM_CONV_DIM
OFF_SK = OFF_SQ + SB_WIDTH
OFF_SV = OFF_SK + SB_WIDTH
OFF_DQ = OFF_SV + SB_WIDTH
OFF_DK = OFF_DQ + DIFF_WIDTH
OFF_DV = OFF_DK + DIFF_WIDTH
OFF_DT = OFF_DV + DIFF_WIDTH
IN_PACKED = OFF_DT + LANES


def _cparams(semantics):
    return pltpu.CompilerParams(dimension_semantics=semantics, vmem_limit_bytes=VMEM_LIMIT)


SSD_STREAMS = 4
SB_STREAMS = 4
DIFF_STREAMS = 4


def _resident(block_shape, index_map):
    return pl.BlockSpec(block_shape, index_map, pipeline_mode=pl.Buffered(1))


def _silu(x):
    hx = 0.5 * x
    return hx + hx * jnp.tanh(hx)


def _softplus(x):
    return jnp.maximum(x, 0.0) + jnp.log1p(jnp.exp(-jnp.abs(x)))


def _norm_modulate(x, g, shift, scale):
    ms = jnp.mean(x * x, axis=-1, keepdims=True)
    y = x * lax.rsqrt(ms + EPS) * g
    return y * (1.0 + scale) + shift


def _ada_kernel(c_ref, w_ref, b_ref, o_ref):
    cond = _silu(c_ref[...])
    w = w_ref[...]
    c_hi = cond.astype(BF16)
    c_lo = (cond - c_hi.astype(F32)).astype(BF16)
    w_hi = w.astype(BF16)
    w_lo = (w - w_hi.astype(F32)).astype(BF16)
    acc = jnp.dot(c_hi, w_hi, preferred_element_type=F32)
    acc = acc + jnp.dot(c_lo, w_hi, preferred_element_type=F32)
    acc = acc + jnp.dot(c_hi, w_lo, preferred_element_type=F32)
    o_ref[...] = acc + b_ref[...]


def _ada_modulation(c, ada_w, ada_b):
    depth, d, nmod = ada_w.shape
    b = c.shape[0]
    rows = 8 * pl.cdiv(b, 8)
    c_pad = jnp.zeros((rows, d), F32).at[:b].set(c)
    out = pl.pallas_call(
        _ada_kernel,
        grid=(depth, nmod // d),
        in_specs=[
            pl.BlockSpec((rows, d), lambda l, j: (0, 0)),
            pl.BlockSpec((None, d, d), lambda l, j: (l, 0, j)),
            pl.BlockSpec((None, 1, d), lambda l, j: (l, 0, j)),
        ],
        out_specs=pl.BlockSpec((None, rows, d), lambda l, j: (l, 0, j)),
        out_shape=jax.ShapeDtypeStruct((depth, rows, nmod), F32),
        compiler_params=_cparams(("arbitrary", "arbitrary")),
        name="ada_modulation",
    )(c_pad, ada_w, ada_b.reshape(depth, 1, nmod))
    return out[:, :b].reshape(depth, b, N_MOD, d)


def _ffn_kernel(x_ref, mod_ref, g_ref, w13_ref, w2_ref, *rest, mod_row, final_norm):
    if final_norm:
        fg_ref, o_ref, acc_ref = rest
    else:
        o_ref, acc_ref = rest
    x = x_ref[...]
    shift = mod_ref[mod_row:mod_row + 1, :]
    scale = mod_ref[mod_row + 1:mod_row + 2, :]
    gate = mod_ref[mod_row + 2:mod_row + 3, :]
    h = _norm_modulate(x, g_ref[...], shift, scale).astype(BF16)
    d_ff = w2_ref.shape[0]
    for j in range(d_ff // FF_CHUNK):
        cols = slice(j * FF_CHUNK, (j + 1) * FF_CHUNK)
        w1 = w13_ref[:, cols].astype(BF16)
        w3 = w13_ref[:, d_ff + j * FF_CHUNK:d_ff + (j + 1) * FF_CHUNK].astype(BF16)
        a = jnp.dot(h, w1, preferred_element_type=F32)
        u = jnp.dot(h, w3, preferred_element_type=F32)
        act = (_silu(a) * u).astype(BF16)
        part = jnp.dot(act, w2_ref[cols, :].astype(BF16), preferred_element_type=F32)
        if j == 0:
            acc_ref[...] = part
        else:
            acc_ref[...] += part
    y = x + (0.5 * gate) * acc_ref[...]
    if final_norm:
        ms = jnp.mean(y * y, axis=-1, keepdims=True)
        y = y * lax.rsqrt(ms + EPS) * fg_ref[...]
    o_ref[...] = y


def _ffn(x, mod, g, w13, w2, layer, mod_row, final_g=None):
    b, s, d = x.shape
    d_ff = w2.shape[1]
    assert d_ff % FF_CHUNK == 0 and d_ff % LANES == 0
    tm = min(TOKEN_TILE, s)
    final_norm = final_g is not None
    in_specs = [
        pl.BlockSpec((None, tm, d), lambda bi, i: (bi, i, 0)),
        pl.BlockSpec((None, N_MOD, d), lambda bi, i: (bi, 0, 0)),
        _resident((1, d), lambda bi, i: (0, 0)),
        _resident((None, d, 2 * d_ff), lambda bi, i: (layer, 0, 0)),
        _resident((None, d_ff, d), lambda bi, i: (layer, 0, 0)),
    ]
    args = [x, mod, g.reshape(1, d), w13, w2]
    if final_norm:
        in_specs.append(_resident((1, d), lambda bi, i: (0, 0)))
        args.append(final_g.reshape(1, d))
    return pl.pallas_call(
        functools.partial(_ffn_kernel, mod_row=mod_row, final_norm=final_norm),
        grid=(b, s // tm),
        in_specs=in_specs,
        out_specs=pl.BlockSpec((None, tm, d), lambda bi, i: (bi, i, 0)),
        out_shape=jax.ShapeDtypeStruct((b, s, d), F32),
        scratch_shapes=[pltpu.VMEM((tm, d), F32)],
        compiler_params=_cparams(("parallel", "parallel")),
        name="ffn_final" if final_norm else "ffn",
    )(*args)


def _inproj_kernel(x_ref, mod_ref, g_ref, w_ref, z_ref, xbc_ref, dt_ref,
                   sq_ref, sk_ref, sv_ref, dq_ref, dk_ref, dv_ref):
    x = x_ref[...]
    h = _norm_modulate(x, g_ref[...], mod_ref[3:4, :], mod_ref[4:5, :]).astype(BF16)

    def proj(off, width):
        return jnp.dot(h, w_ref[:, off:off + width], preferred_element_type=F32)

    qk_scale = HEAD_DIM ** -0.5 * LOG2E
    z_ref[...] = proj(OFF_Z, SSM_D_INNER).astype(BF16)
    xbc_ref[...] = proj(OFF_XBC, SSM_CONV_DIM).astype(BF16)
    dt_ref[...] = proj(OFF_DT, LANES)
    sq_ref[...] = (proj(OFF_SQ, SB_WIDTH) * qk_scale).astype(BF16)
    sk_ref[...] = proj(OFF_SK, SB_WIDTH).astype(BF16)
    sv_ref[...] = proj(OFF_SV, SB_WIDTH).astype(BF16)
    dq_ref[...] = (proj(OFF_DQ, DIFF_WIDTH) * qk_scale).astype(BF16)
    dk_ref[...] = proj(OFF_DK, DIFF_WIDTH).astype(BF16)
    dv_ref[...] = proj(OFF_DV, DIFF_WIDTH).astype(BF16)


def _pack_kernel(w_ref, o_ref):
    dt0 = SSM_D_INNER + SSM_CONV_DIM
    dt1 = dt0 + SSM_HEADS
    w = w_ref[...]
    rows = w.shape[0]
    o_ref[:, :dt0] = w[:, :dt0].astype(BF16)
    o_ref[:, dt0:OFF_DT] = w[:, dt1:].astype(BF16)
    o_ref[:, OFF_DT:] = jnp.concatenate(
        [w[:, dt0:dt1], jnp.zeros((rows, LANES - SSM_HEADS), F32)], axis=1).astype(BF16)


def _pack_w_in(w_in):
    depth, d, n_in = w_in.shape
    rows = 128
    return pl.pallas_call(
        _pack_kernel,
        grid=(depth, d // rows),
        in_specs=[pl.BlockSpec((None, rows, n_in), lambda l, r: (l, r, 0))],
        out_specs=pl.BlockSpec((None, rows, IN_PACKED), lambda l, r: (l, r, 0)),
        out_shape=jax.ShapeDtypeStruct((depth, d, IN_PACKED), BF16),
        compiler_params=_cparams(("arbitrary", "arbitrary")),
        name="pack_w_in",
    )(w_in)


def _inproj(x, mod, g, w, layer):
    b, s, d = x.shape
    tm = min(PROJ_TILE, s)
    widths = (SSM_D_INNER, SSM_CONV_DIM, LANES) + (SB_WIDTH,) * 3 + (DIFF_WIDTH,) * 3
    dtypes = (BF16, BF16, F32) + (BF16,) * 6
    return pl.pallas_call(
        _inproj_kernel,
        grid=(b, s // tm),
        in_specs=[
            pl.BlockSpec((None, tm, d), lambda bi, i: (bi, i, 0)),
            pl.BlockSpec((None, N_MOD, d), lambda bi, i: (bi, 0, 0)),
            _resident((1, d), lambda bi, i: (0, 0)),
            _resident((None, d, IN_PACKED), lambda bi, i: (layer, 0, 0)),
        ],
        out_specs=[pl.BlockSpec((None, tm, wd), lambda bi, i: (bi, i, 0)) for wd in widths],
        out_shape=[jax.ShapeDtypeStruct((b, s, wd), dt) for wd, dt in zip(widths, dtypes)],
        compiler_params=_cparams(("parallel", "parallel")),
        name="mixer_inproj",
    )(x, mod, g.reshape(1, d), w)


def _ssd_kernel(z_ref, xbc_ref, dtr_ref, cw_ref, cb_ref, dtb_ref, alog_ref, dsk_ref, ng_ref, e_ref,
                y_ref, tail_ref, state_ref):
    rows = range(z_ref.shape[0])
    convs = [_ssd_conv(xbc_ref.at[bb], cw_ref, cb_ref, tail_ref.at[bb], state_ref.at[bb]) for bb in rows]
    for bb in rows:
        _ssd_scan(convs[bb], z_ref.at[bb], dtr_ref.at[bb], dtb_ref, alog_ref, dsk_ref, ng_ref, e_ref,
                  y_ref.at[bb], state_ref.at[bb])


def _ssd_conv(xbc_ref, cw_ref, cb_ref, tail_ref, state_ref):
    L = xbc_ref.shape[0]
    pad = tail_ref.shape[0]
    c = pl.program_id(1)

    @pl.when(c == 0)
    def _():
        tail_ref[...] = jnp.zeros(tail_ref.shape, BF16)
        state_ref[...] = jnp.zeros(state_ref.shape, F32)

    u = xbc_ref[...]
    u_ext = jnp.concatenate([tail_ref[...], u], axis=0)
    tail_ref[...] = u[L - pad:, :]
    trow = lax.broadcasted_iota(jnp.int32, (L, pad + L), 0)
    tcol = lax.broadcasted_iota(jnp.int32, (L, pad + L), 1)
    shifts = jnp.concatenate(
        [jnp.where(tcol == trow + (pad - (SSM_CONV - 1) + k), 1.0, 0.0) for k in range(SSM_CONV)], axis=0)
    shifted = jnp.dot(shifts.astype(BF16), u_ext, preferred_element_type=F32)
    conv = cb_ref[...]
    for k in range(SSM_CONV):
        conv = conv + cw_ref[k:k + 1, :] * shifted[k * L:(k + 1) * L]
    return _silu(conv)


def _ssd_scan(xa, z_ref, dtr_ref, dtb_ref, alog_ref, dsk_ref, ng_ref, e_ref, y_ref, state_ref):
    L = z_ref.shape[0]
    xs = xa[:, :SSM_D_INNER]
    bmat = xa[:, SSM_D_INNER:SSM_D_INNER + SSM_GROUPS * SSM_STATE]
    cmat = xa[:, SSM_D_INNER + SSM_GROUPS * SSM_STATE:]

    dt = _softplus(dtr_ref[...] + dtb_ref[...])
    a = -jnp.exp(alog_ref[...])
    da = dt * a
    row = lax.broadcasted_iota(jnp.int32, (L, L), 0)
    col = lax.broadcasted_iota(jnp.int32, (L, L), 1)
    causal = row >= col
    tri = causal.astype(F32).astype(BF16)
    da_1 = da.astype(BF16)
    rest = da - da_1.astype(F32)
    da_2 = rest.astype(BF16)
    da_3 = (rest - da_2.astype(F32)).astype(BF16)
    a_cum = jnp.dot(jnp.concatenate([tri, tri, tri], axis=1), jnp.concatenate([da_1, da_2, da_3], axis=0),
                    preferred_element_type=F32)
    a_cum_t = a_cum.T
    exp_a = jnp.exp(a_cum)
    decay_to_end = jnp.exp(a_cum[L - 1:L, :] - a_cum)

    stacked = jnp.concatenate([dt, exp_a, decay_to_end], axis=0)
    hi = stacked.astype(BF16)
    lo = (stacked - hi.astype(F32)).astype(BF16)
    spread = jnp.dot(jnp.concatenate([hi, lo], axis=1), e_ref[...], preferred_element_type=F32)
    dt_e = spread[0:L]
    exp_a_e = spread[L:2 * L]
    dte_e = spread[2 * L:3 * L]

    xd = xs * dt_e
    xd_b = xd.astype(BF16)
    xdd_b = (xd * dte_e).astype(BF16)
    prev_b = state_ref[...].astype(BF16)
    lane = lax.broadcasted_iota(jnp.int32, (L, LANES), 1)
    heads_per_group = SSM_HEADS // SSM_GROUPS
    group_width = heads_per_group * HEAD_DIM

    y_diag_parts = []
    y_off_parts = []
    s_new_parts = []
    for g in range(SSM_GROUPS):
        bg = bmat[:, g * SSM_STATE:(g + 1) * SSM_STATE]
        cg_b = cmat[:, g * SSM_STATE:(g + 1) * SSM_STATE].astype(BF16)
        bg_b = bg.astype(BF16)
        bg_t_b = bg.T.astype(BF16)
        gs = slice(g * group_width, (g + 1) * group_width)
        cb = lax.dot_general(cg_b, bg_b, (((1,), (1,)), ((), ())), preferred_element_type=F32)
        y_off_parts.append(jnp.dot(cg_b, prev_b[:, gs], preferred_element_type=F32))
        s_new_parts.append(jnp.dot(bg_t_b, xdd_b[:, gs], preferred_element_type=F32))
        for hp in range(heads_per_group // 2):
            ms = []
            for h in (g * heads_per_group + 2 * hp, g * heads_per_group + 2 * hp + 1):
                seg = a_cum[:, h:h + 1] - a_cum_t[h:h + 1, :]
                decay = jnp.exp(jnp.where(causal, seg, -jnp.inf))
                ms.append((cb * decay).astype(BF16))
            pair = (g * heads_per_group) // 2 + hp
            yp = jnp.dot(jnp.concatenate(ms, axis=0), xd_b[:, pair * LANES:(pair + 1) * LANES],
                         preferred_element_type=F32)
            y_diag_parts.append(jnp.where(lane < HEAD_DIM, yp[:L], yp[L:]))
    y_diag = jnp.concatenate(y_diag_parts, axis=1)
    y_off = jnp.concatenate(y_off_parts, axis=1)
    s_new = jnp.concatenate(s_new_parts, axis=1)

    y = y_diag + y_off * exp_a_e + dsk_ref[...] * xs
    state_ref[...] = state_ref[...] * exp_a_e[L - 1:L, :] + s_new

    y = y * _silu(z_ref[...].astype(F32))
    outs = []
    for g in range(SSM_GROUPS):
        yg = y[:, g * group_width:(g + 1) * group_width]
        ms = jnp.mean(yg * yg, axis=-1, keepdims=True)
        outs.append(yg * lax.rsqrt(ms + EPS))
    y_ref[...] = (jnp.concatenate(outs, axis=1) * ng_ref[...]).astype(BF16)


def _head_spread_matrix():
    e = np.zeros((2 * LANES, SSM_D_INNER), np.float32)
    for h in range(SSM_HEADS):
        e[h, h * HEAD_DIM:(h + 1) * HEAD_DIM] = 1.0
        e[LANES + h, h * HEAD_DIM:(h + 1) * HEAD_DIM] = 1.0
    return jnp.asarray(e, BF16)


def _pad_lanes(v):
    return jnp.zeros((1, LANES), F32).at[0, :v.shape[0]].set(v)


def _ssd(z, xbc, dt_raw, conv_w, conv_b, dt_bias, a_log, d_skip, norm_g):
    b, s, _ = z.shape
    L = SSM_CHUNK
    nb = SSD_STREAMS if b % SSD_STREAMS == 0 else 1
    row = lambda width: _resident((1, width), lambda bi, c: (0, 0))
    return pl.pallas_call(
        _ssd_kernel,
        grid=(b // nb, s // L),
        in_specs=[
            pl.BlockSpec((nb, L, SSM_D_INNER), lambda bi, c: (bi, c, 0)),
            pl.BlockSpec((nb, L, SSM_CONV_DIM), lambda bi, c: (bi, c, 0)),
            pl.BlockSpec((nb, L, LANES), lambda bi, c: (bi, c, 0)),
            _resident((SSM_CONV, SSM_CONV_DIM), lambda bi, c: (0, 0)),
            row(SSM_CONV_DIM), row(LANES), row(LANES), row(SSM_D_INNER), row(SSM_D_INNER),
            _resident((2 * LANES, SSM_D_INNER), lambda bi, c: (0, 0)),
        ],
        out_specs=pl.BlockSpec((nb, L, SSM_D_INNER), lambda bi, c: (bi, c, 0)),
        out_shape=jax.ShapeDtypeStruct((b, s, SSM_D_INNER), BF16),
        scratch_shapes=[pltpu.VMEM((nb, BF16_SUBLANES, SSM_CONV_DIM), BF16),
                        pltpu.VMEM((nb, SSM_STATE, SSM_D_INNER), F32)],
        compiler_params=_cparams(("parallel", "arbitrary")),
        name="ssd_mixer",
    )(z, xbc, dt_raw, conv_w.T, conv_b.reshape(1, -1), _pad_lanes(dt_bias), _pad_lanes(a_log),
      jnp.repeat(d_skip, HEAD_DIM).reshape(1, -1), norm_g.reshape(1, -1), _head_spread_matrix())


def _split_lane_halves(x2):
    xf = x2.astype(F32)
    lane = lax.broadcasted_iota(jnp.int32, xf.shape, 1)
    lo = jnp.where(lane < HEAD_DIM, xf, 0.0)
    hi = jnp.where(lane >= HEAD_DIM, xf, 0.0)
    return jnp.concatenate([lo, hi], axis=0).astype(BF16)


def _scores_t(kb, qs):
    return lax.dot_general(kb, qs, (((1,), (1,)), ((), ())), preferred_element_type=F32)


def _pv_t(vb, wt):
    return lax.dot_general(vb, wt, (((0,), (0,)), ((), ())), preferred_element_type=F32)


def _sb_kernel(q_ref, k_ref, v_ref, o_ref, z_ref, cs_ref, tot_ref, acc_ref, carry_ref):
    T = q_ref.shape[0]
    n_streams = q_ref.shape[1] // LANES
    i = pl.program_id(2)
    qs = [_split_lane_halves(q_ref[:, p * LANES:(p + 1) * LANES]) for p in range(n_streams)]

    key = lax.broadcasted_iota(jnp.int32, (T, 2 * T), 0)
    qry = lax.broadcasted_iota(jnp.int32, (T, 2 * T), 1)
    strict = key < jnp.where(qry >= T, qry - T, qry)
    ur = lax.broadcasted_iota(jnp.int32, (T, T), 0)
    uc = lax.broadcasted_iota(jnp.int32, (T, T), 1)
    neg_suffix = jnp.where(uc >= ur, -1.0, 0.0).astype(BF16)
    streams = range(n_streams)
    sign_bit = jnp.uint32(0x80000000)

    def scores(j_lo, nb):
        rows = pl.ds(pl.multiple_of(j_lo * T, T), nb * T)
        return [_scores_t(k_ref[rows, p * LANES:(p + 1) * LANES], qs[p]) for p in streams]

    def softplus2(z, diagonal):
        neg_abs = lax.bitcast_convert_type(lax.bitcast_convert_type(z, jnp.uint32) | sign_bit, F32)
        sp = jnp.maximum(z, 0.0) + jnp.log(1.0 + jnp.exp2(neg_abs)) * LOG2E
        if diagonal:
            sp = jnp.where(strict, sp, 0.0)
        return sp.astype(BF16)

    def suffix_sums(sp, nb):
        parts = [jnp.dot(neg_suffix, sp[u * T:(u + 1) * T], preferred_element_type=F32) for u in range(nb)]
        rows = [None] * nb
        total = parts[nb - 1][0:1, :]
        for u in reversed(range(nb - 1)):
            rows[u] = total
            total = total + parts[u][0:1, :]
        rows[nb - 1] = total
        return parts, (rows[0] if nb == 1 else jnp.concatenate(rows, axis=0))

    def weights(z, parts, after_rows, carry, diagonal):
        nb = len(parts)
        ws = []
        for u in range(nb):
            later = carry if u == nb - 1 else carry + after_rows[u:u + 1, :]
            w = jnp.exp2(z[u * T:(u + 1) * T] + (parts[u] + later))
            if diagonal:
                w = jnp.where(strict, w, 0.0)
            ws.append(w.astype(BF16))
        return ws[0] if nb == 1 else jnp.concatenate(ws, axis=0)

    def accumulate(j_lo, nb, ws, tots):
        rows = pl.ds(pl.multiple_of(j_lo * T, T), nb * T)
        for p in streams:
            acc_ref[p] += _pv_t(v_ref[rows, p * LANES:(p + 1) * LANES], ws[p])
            carry_ref[p] += tots[p]

    def direct_step(j_lo, nb, diagonal=False):
        zs = scores(j_lo, nb)
        cts = [suffix_sums(softplus2(z, diagonal), nb) for z in zs]
        ws = [weights(zs[p], cts[p][0], cts[p][1], carry_ref[p], diagonal) for p in streams]
        accumulate(j_lo, nb, ws, [ct[1][nb - 1:nb, :] for ct in cts])

    def finish_produce(slot, zs, with_diagonal=False):
        for p in streams:
            z_ref[slot, p] = zs[p]
        for p in streams:
            if with_diagonal:
                sp = jnp.concatenate([softplus2(z_ref[slot, p, 0:T, :], False),
                                      softplus2(z_ref[slot, p, T:2 * T, :], True)], axis=0)
            else:
                sp = softplus2(z_ref[slot, p], False)
            parts, rows = suffix_sums(sp, 2)
            if with_diagonal:
                parts[1] = jnp.where(strict, parts[1], -jnp.inf)
            cs_ref[slot, p] = jnp.concatenate(parts, axis=0)
            tot_ref[slot, p] = rows

    for p in streams:
        acc_ref[p] = jnp.zeros((LANES, 2 * T), F32)
        carry_ref[p] = jnp.zeros((1, 2 * T), F32)
    n_double = (i + 1) // 2
    pair_lo = lambda t: i - 1 - 2 * t

    def consume(t, slot, produce_next):
        ws = [weights(z_ref[slot, p], [cs_ref[slot, p, 0:T, :], cs_ref[slot, p, T:2 * T, :]],
                      tot_ref[slot, p], carry_ref[p], False) for p in streams]
        tots = [tot_ref[slot, p, 1:2, :] for p in streams]
        if produce_next:
            finish_produce(1 - slot, scores(pair_lo(t + 1), 2))
        accumulate(pair_lo(t), 2, ws, tots)

    @pl.when(n_double > 0)
    def _():
        finish_produce(0, scores(pair_lo(0), 2), with_diagonal=True)

    n_twice = jnp.maximum(n_double - 1, 0) // 2

    def body(u, _):
        consume(2 * u, 0, True)
        consume(2 * u + 1, 1, True)
        return 0

    lax.fori_loop(0, n_twice, body, 0)
    left = n_double - 2 * n_twice

    @pl.when(left == 2)
    def _():
        consume(2 * n_twice, 0, True)
        consume(2 * n_twice + 1, 1, False)

    @pl.when(left == 1)
    def _():
        consume(2 * n_twice, 0, False)

    @pl.when(jnp.logical_and(i + 1 - 2 * n_double > 0, i > 0))
    def _():
        direct_step(0, 1)

    @pl.when(i == 0)
    def _():
        direct_step(0, 1, diagonal=True)

    for p in range(n_streams):
        acc = acc_ref[p]
        pair_t = jnp.concatenate([acc[:HEAD_DIM, :T], acc[HEAD_DIM:, T:]], axis=0)
        o_ref[:, p * LANES:(p + 1) * LANES] = pair_t.T.astype(BF16)


def _stick_breaking(q, k, v):
    b, s, width = q.shape
    T = min(ATT_BLOCK, s)
    gw = SB_STREAMS * LANES
    return pl.pallas_call(
        _sb_kernel,
        grid=(b, width // gw, s // T),
        in_specs=[
            pl.BlockSpec((None, T, gw), lambda bi, g, i: (bi, i, g)),
            pl.BlockSpec((None, s, gw), lambda bi, g, i: (bi, 0, g)),
            pl.BlockSpec((None, s, gw), lambda bi, g, i: (bi, 0, g)),
        ],
        out_specs=pl.BlockSpec((None, T, gw), lambda bi, g, i: (bi, i, g)),
        out_shape=jax.ShapeDtypeStruct((b, s, width), BF16),
        scratch_shapes=[
            pltpu.VMEM((2, SB_STREAMS, 2 * T, 2 * T), F32),
            pltpu.VMEM((2, SB_STREAMS, 2 * T, 2 * T), F32),
            pltpu.VMEM((2, SB_STREAMS, 2, 2 * T), F32),
            pltpu.VMEM((SB_STREAMS, LANES, 2 * T), F32),
            pltpu.VMEM((SB_STREAMS, 1, 2 * T), F32),
        ],
        compiler_params=_cparams(("parallel", "parallel", "arbitrary")),
        name="stick_breaking_attention",
    )(q, k, v)


def _bucket_starts():
    max_exact = N_REL_BUCKETS // 2
    dist = np.arange(0, 4 * REL_MAX_DIST, dtype=np.float64)
    ratio = np.log(np.maximum(dist, max_exact) / max_exact) / math.log(REL_MAX_DIST / max_exact)
    scaled = ratio * (N_REL_BUCKETS - max_exact)
    frac = np.abs(scaled - np.round(scaled))
    interior = (dist > max_exact) & (dist < REL_MAX_DIST)
    assert frac[interior].min() > 1e-3, "a bucket edge sits on an integer distance"
    large = np.minimum(max_exact + np.floor(scaled + 1e-9).astype(np.int64), N_REL_BUCKETS - 1)
    bucket = np.where(dist < max_exact, dist.astype(np.int64), large)
    assert np.all(np.diff(bucket) >= 0) and bucket[-1] == N_REL_BUCKETS - 1
    return [int(np.argmax(bucket >= bkt)) for bkt in range(N_REL_BUCKETS)]


def _bias_table_kernel(rel_ref, o_ref, *, starts):
    h = pl.program_id(0)
    T = o_ref.shape[-1]
    key = lax.broadcasted_iota(jnp.int32, (T, T), 0)
    qry = lax.broadcasted_iota(jnp.int32, (T, T), 1)
    far = rel_ref[N_REL_BUCKETS - 1, h]
    for o_blk in range(o_ref.shape[0]):
        dist = qry - key + o_blk * T
        bias = jnp.full((T, T), rel_ref[0, h], F32)
        for bkt in range(1, N_REL_BUCKETS):
            bias = jnp.where(dist >= starts[bkt], rel_ref[bkt, h], bias)
        o_ref[o_blk] = jnp.where(dist >= 0, (bias - far) * LOG2E, -jnp.inf)


def _bias_tables(rel_bias, T):
    starts = _bucket_starts()
    assert T + 1 >= starts[-1]
    return pl.pallas_call(
        functools.partial(_bias_table_kernel, starts=starts),
        grid=(DIFF_HEADS,),
        in_specs=[pl.BlockSpec(memory_space=pltpu.SMEM)],
        out_specs=pl.BlockSpec((None, 2, T, T), lambda h: (h, 0, 0, 0)),
        out_shape=jax.ShapeDtypeStruct((DIFF_HEADS, 2, T, T), F32),
        compiler_params=_cparams(("arbitrary",)),
        name="t5_bias_tables",
    )(rel_bias)


def _diff_kernel(q_ref, k_ref, v_ref, tb_ref, lq1_ref, lk1_ref, lq2_ref, lk2_ref, sg_ref, o_ref,
                 s_ref, m_ref, l_ref, acc_ref, *, lambda_init):
    T = q_ref.shape[0]
    n_streams = q_ref.shape[1] // LANES
    i = pl.program_id(2)
    qs = [_split_lane_halves(q_ref[:, h * LANES:(h + 1) * LANES]) for h in range(n_streams)]

    heads = range(n_streams)

    def scores(j_lo, nb):
        rows = pl.ds(pl.multiple_of(j_lo * T, T), nb * T)
        return [_scores_t(k_ref[rows, h * LANES:(h + 1) * LANES], qs[h]) for h in heads]

    def softmax_part(ss, table_rows):
        out = []
        for h in heads:
            read = ss[h] if callable(ss[h]) else (lambda v=ss[h]: v)
            if table_rows is not None:
                biases = [tb_ref[h, o] for o in table_rows]
                bias = biases[0] if len(biases) == 1 else jnp.concatenate(biases, axis=0)
                s = read() + jnp.concatenate([bias, bias], axis=1)
                read = lambda v=s: v
            m_old = m_ref[h]
            m_new = jnp.maximum(m_old, jnp.max(read(), axis=0, keepdims=True))
            alpha = jnp.exp2(m_old - m_new)
            p = jnp.exp2(read() - m_new)
            m_ref[h] = m_new
            l_ref[h] = alpha * l_ref[h] + jnp.sum(p, axis=0, keepdims=True)
            out.append((alpha, p.astype(BF16)))
        return out

    def accumulate(j_lo, nb, aps):
        rows = pl.ds(pl.multiple_of(j_lo * T, T), nb * T)
        for h in heads:
            alpha, p = aps[h]
            acc_ref[h] = alpha * acc_ref[h] + _pv_t(v_ref[rows, h * LANES:(h + 1) * LANES], p)

    for h in heads:
        m_ref[h] = jnp.full((1, 2 * T), -jnp.inf, F32)
        l_ref[h] = jnp.zeros((1, 2 * T), F32)
        acc_ref[h] = jnp.zeros((DIFF_V_DIM, 2 * T), F32)
    n_far = jnp.maximum(i - 1, 0)
    n_double = n_far // 2
    first = n_far - 2 * n_double
    pair_lo = lambda t: first + 2 * t

    def produce(t, slot):
        for h, s in enumerate(scores(pair_lo(t), 2)):
            s_ref[slot, h] = s

    def consume(t, slot, produce_next, table_rows=None):
        aps = softmax_part([lambda h=h: s_ref[slot, h] for h in heads], table_rows)
        if produce_next:
            produce(t + 1, 1 - slot)
        accumulate(pair_lo(t), 2, aps)

    @pl.when(first > 0)
    def _():
        accumulate(0, 1, softmax_part(scores(0, 1), None))

    @pl.when(i > 0)
    def _():
        produce(0, 0)

    n_twice = n_double // 2

    def body(u, _):
        consume(2 * u, 0, True)
        consume(2 * u + 1, 1, True)
        return 0

    lax.fori_loop(0, n_twice, body, 0)
    left = n_double - 2 * n_twice

    @pl.when(jnp.logical_and(i > 0, left == 1))
    def _():
        consume(2 * n_twice, 0, True)
        consume(2 * n_twice + 1, 1, False, [1, 0])

    @pl.when(jnp.logical_and(i > 0, left == 0))
    def _():
        consume(2 * n_twice, 0, False, [1, 0])

    @pl.when(i == 0)
    def _():
        accumulate(0, 1, softmax_part(scores(0, 1), [0]))

    lam = (jnp.exp(jnp.sum(lq1_ref[...] * lk1_ref[...], axis=-1, keepdims=True))
           - jnp.exp(jnp.sum(lq2_ref[...] * lk2_ref[...], axis=-1, keepdims=True)) + lambda_init)
    for h in range(n_streams):
        l, acc = l_ref[h], acc_ref[h]
        o_t = acc[:, :T] / l[:, :T] - lam * (acc[:, T:] / l[:, T:])
        o = o_t.T
        ms = jnp.mean(o * o, axis=-1, keepdims=True)
        o = o * lax.rsqrt(ms + EPS) * sg_ref[...] * (1.0 - lambda_init)
        o_ref[:, h * LANES:(h + 1) * LANES] = o.astype(BF16)


def _diff_attention(q, k, v, tables, lq1, lk1, lq2, lk2, subln_g, lambda_init):
    b, s, width = q.shape
    T = tables.shape[-1]
    vec = lambda n: _resident((1, n), lambda bi, g, i: (0, 0))
    gw = DIFF_STREAMS * LANES
    return pl.pallas_call(
        functools.partial(_diff_kernel, lambda_init=lambda_init),
        grid=(b, width // gw, s // T),
        in_specs=[
            pl.BlockSpec((None, T, gw), lambda bi, g, i: (bi, i, g)),
            pl.BlockSpec((None, s, gw), lambda bi, g, i: (bi, 0, g)),
            pl.BlockSpec((None, s, gw), lambda bi, g, i: (bi, 0, g)),
            pl.BlockSpec((DIFF_STREAMS,) + tables.shape[1:], lambda bi, g, i: (g, 0, 0, 0)),
            vec(HEAD_DIM), vec(HEAD_DIM), vec(HEAD_DIM), vec(HEAD_DIM), vec(DIFF_V_DIM),
        ],
        out_specs=pl.BlockSpec((None, T, gw), lambda bi, g, i: (bi, i, g)),
        out_shape=jax.ShapeDtypeStruct((b, s, width), BF16),
        scratch_shapes=[
            pltpu.VMEM((2, DIFF_STREAMS, 2 * T, 2 * T), F32),
            pltpu.VMEM((DIFF_STREAMS, 1, 2 * T), F32),
            pltpu.VMEM((DIFF_STREAMS, 1, 2 * T), F32),
            pltpu.VMEM((DIFF_STREAMS, DIFF_V_DIM, 2 * T), F32),
        ],
        compiler_params=_cparams(("parallel", "parallel", "arbitrary")),
        name="differential_attention",
    )(q, k, v, tables, lq1.reshape(1, -1), lk1.reshape(1, -1), lq2.reshape(1, -1), lk2.reshape(1, -1),
      subln_g.reshape(1, -1))


def _outproj_kernel(x_ref, mod_ref, ys_ref, yb_ref, yd_ref, w_ref, o_ref):
    o1 = SSM_D_INNER
    o2 = o1 + SB_WIDTH
    y = jnp.dot(ys_ref[...], w_ref[0:o1, :].astype(BF16), preferred_element_type=F32)
    y = y + jnp.dot(yb_ref[...], w_ref[o1:o2, :].astype(BF16), preferred_element_type=F32)
    y = y + jnp.dot(yd_ref[...], w_ref[o2:, :].astype(BF16), preferred_element_type=F32)
    o_ref[...] = x_ref[...] + mod_ref[5:6, :] * y


def _outproj(x, mod, y_ssm, y_sb, y_diff, w_out, layer):
    b, s, d = x.shape
    tm = min(PROJ_TILE, s)
    tok = lambda width: pl.BlockSpec((None, tm, width), lambda bi, i: (bi, i, 0))
    return pl.pallas_call(
        _outproj_kernel,
        grid=(b, s // tm),
        in_specs=[
            tok(d),
            pl.BlockSpec((None, N_MOD, d), lambda bi, i: (bi, 0, 0)),
            tok(SSM_D_INNER), tok(SB_WIDTH), tok(DIFF_WIDTH),
            _resident((None,) + w_out.shape[1:], lambda bi, i: (layer, 0, 0)),
        ],
        out_specs=tok(d),
        out_shape=jax.ShapeDtypeStruct((b, s, d), F32),
        compiler_params=_cparams(("parallel", "parallel")),
        name="mixer_outproj",
    )(x, mod, y_ssm, y_sb, y_diff, w_out)


def kernel(x, c, ada_w, ada_b, ffn1_norm, ffn1_w13, ffn1_w2, mix_norm, w_in, ssm_conv_w, ssm_conv_b, ssm_dt_bias, ssm_a_log, ssm_d, ssm_norm, diff_lambda_q1, diff_lambda_k1, diff_lambda_q2, diff_lambda_k2, diff_subln, rel_bias, w_out, ffn2_norm, ffn2_w13, ffn2_w2, final_norm):
    depth = ada_w.shape[0]
    s = x.shape[1]
    mods = _ada_modulation(c, ada_w, ada_b)
    tables = _bias_tables(rel_bias, min(ATT_BLOCK, s))
    w_in_packed = _pack_w_in(w_in)
    for l in range(depth):
        mod = mods[l]
        x = _ffn(x, mod, ffn1_norm[l], ffn1_w13, ffn1_w2, l, mod_row=0)
        lambda_init = 0.8 - 0.6 * math.exp(-0.3 * l)
        z, xbc, dt_raw, sq, sk, sv, dq, dk, dv = _inproj(x, mod, mix_norm[l], w_in_packed, l)
        y_ssm = _ssd(z, xbc, dt_raw, ssm_conv_w[l], ssm_conv_b[l], ssm_dt_bias[l], ssm_a_log[l], ssm_d[l],
                     ssm_norm[l])
        y_sb = _stick_breaking(sq, sk, sv)
        y_diff = _diff_attention(dq, dk, dv, tables, diff_lambda_q1[l], diff_lambda_k1[l], diff_lambda_q2[l],
                                 diff_lambda_k2[l], diff_subln[l], lambda_init)
        x = _outproj(x, mod, y_ssm, y_sb, y_diff, w_out, l)
        x = _ffn(x, mod, ffn2_norm[l], ffn2_w13, ffn2_w2, l, mod_row=6,
                 final_g=final_norm if l == depth - 1 else None)
    return x
```
